```python
import jax, jax.numpy as jnp
from jax import lax
import numpy as np

D_MODEL = 2048
BATCH = 16
SEQ = 256
DEPTH = 1
DEC_BATCH = 8
DEC_SEQ = 1024
PAST_LEN = 256

GRID_W = 64
NORM_EPS = 1e-6
M_HEADS = 4
M_QK = 128
M_V = 256
M_CHUNK = 64
M_QK_W = M_HEADS * M_QK
M_V_W = M_HEADS * M_V
M_GATES = 4 * M_HEADS
A_HEADS = 16
A_KV = 4
A_GROUP = A_HEADS // A_KV
A_HD = 64
A_Q_W = A_HEADS * A_HD
A_KV_W = A_KV * A_HD
WINDOW = 128
ROPE_THETA = 10000.0
D_FF = 5632
N_MOD = 9
IN_SIZES = (M_QK_W, M_QK_W, M_V_W, M_V_W, M_GATES, A_Q_W, A_KV_W, A_KV_W, D_MODEL, D_MODEL)
IN_COLS = 2 * M_QK_W + 2 * M_V_W + M_GATES + A_Q_W + 2 * A_KV_W + 2 * D_MODEL

kernel_name = 'hybrid_mlstm_swa_flow_step'


def rms_norm(x, w):
    xf = x.astype(jnp.float32)
    y = xf * lax.rsqrt(jnp.mean(xf * xf, axis=-1, keepdims=True) + NORM_EPS)
    return (y * w.astype(jnp.float32)).astype(x.dtype)


def modulation(cond, ada_w, ada_b):
    mod = jax.nn.silu(cond) @ ada_w + ada_b
    return jnp.split(mod[..., None, :], N_MOD, axis=-1)


def modulate(h, shift, scale):
    return h * (1 + scale) + shift


def swiglu(h, wi, wo):
    g, u = jnp.split(h @ wi, 2, axis=-1)
    return (jax.nn.silu(g) * u) @ wo


def ffn_sublayer(x, shift, scale, gate, norm_w, wi, wo):
    return x + 0.5 * gate * swiglu(modulate(rms_norm(x, norm_w), shift, scale), wi, wo)


def mixer_inputs(x, shift, scale, norm_w, w_in):
    z = modulate(rms_norm(x, norm_w), shift, scale) @ w_in
    idx, acc = [], 0
    for s in IN_SIZES[:-1]:
        acc += s
        idx.append(acc)
    return jnp.split(z, idx, axis=-1)


def rope_1d(x, pos):
    nf = x.shape[-1] // 2
    inv = ROPE_THETA ** (-jnp.arange(nf, dtype=jnp.float32) / nf)
    ang = pos.astype(jnp.float32)[:, None] * inv
    cos = jnp.cos(ang)[:, None, :]
    sin = jnp.sin(ang)[:, None, :]
    xf = x.astype(jnp.float32)
    x1, x2 = xf[..., :nf], xf[..., nf:]
    return jnp.concatenate([x1 * cos - x2 * sin, x1 * sin + x2 * cos], axis=-1).astype(x.dtype)


def axial_rope(x):
    n_tok = x.shape[1]
    rows = n_tok // GRID_W
    row = jnp.repeat(jnp.arange(rows), GRID_W)
    col = jnp.tile(jnp.arange(GRID_W), rows)
    half = A_HD // 2
    return jnp.concatenate([rope_1d(x[..., :half], row), rope_1d(x[..., half:], col)], axis=-1)


def mlstm_scan(q, k, v, ig, lf, C0, n0, m0):
    bsz, nh, n_tok, _ = q.shape
    nc = n_tok // M_CHUNK

    def chunks(a):
        return jnp.moveaxis(a.reshape(a.shape[:2] + (nc, M_CHUNK) + a.shape[3:]), 2, 0)

    tril = jnp.tril(jnp.ones((M_CHUNK, M_CHUNK), dtype=bool))

    def step(carry, inp):
        C, n, m = carry
        qc, kc, vc, ic, fc = inp
        b = jnp.cumsum(fc, axis=-1)
        dmat = jnp.where(tril, b[..., :, None] - b[..., None, :] + ic[..., None, :], -jnp.inf)
        inter = b + m[..., None]
        m_t = jnp.maximum(inter, jnp.max(dmat, axis=-1))
        s = jnp.einsum('bhtd,bhsd->bhts', qc, kc) * jnp.exp(dmat - m_t[..., None])
        decay = jnp.exp(inter - m_t)
        num = jnp.einsum('bhts,bhsv->bhtv', s, vc) + decay[..., None] * jnp.einsum('bhvd,bhtd->bhtv', C, qc)
        den = jnp.sum(s, axis=-1) + decay * jnp.einsum('bhd,bhtd->bht', n, qc)
        h = num / jnp.maximum(jnp.abs(den), jnp.exp(-m_t))[..., None]
        b_last = b[..., -1]
        g = b_last[..., None] - b + ic
        m_new = jnp.maximum(b_last + m, jnp.max(g, axis=-1))
        wk = jnp.exp(g - m_new[..., None])
        carry_decay = jnp.exp(b_last + m - m_new)
        C_new = carry_decay[..., None, None] * C + jnp.einsum('bhsv,bhsd->bhvd', vc * wk[..., None], kc)
        n_new = carry_decay[..., None] * n + jnp.einsum('bhs,bhsd->bhd', wk, kc)
        return (C_new, n_new, m_new), h

    (C, n, m), h = lax.scan(step, (C0, n0, m0), (chunks(q), chunks(k), chunks(v), chunks(ig), chunks(lf)))
    h = jnp.moveaxis(h, 0, 2).reshape(bsz, nh, n_tok, M_V)
    return h, C, n, m


def mlstm_branch(mq, mk, mv, mo, mg, gate_b, norm_w, C0, n0, m0):
    bsz, n_tok, _ = mq.shape
    f32 = jnp.float32

    def heads(a, d):
        return a.reshape(bsz, n_tok, M_HEADS, d).transpose(0, 2, 1, 3).astype(f32)

    q = heads(mq, M_QK) * (M_QK ** -0.5)
    k = heads(mk, M_QK)
    v = heads(mv, M_V)
    g = (mg.astype(f32) + gate_b.astype(f32)).transpose(0, 2, 1)
    i_f, f_f, i_b, f_b = jnp.split(g, 4, axis=1)
    C0, n0, m0 = C0.astype(f32), n0.astype(f32), m0.astype(f32)
    h_f, C_f, n_f, m_f = mlstm_scan(q, k, v, i_f, jax.nn.log_sigmoid(f_f), C0[:, 0], n0[:, 0], m0[:, 0])

    def rev(a):
        return jnp.flip(a, axis=2)

    h_b, C_b, n_b, m_b = mlstm_scan(rev(q), rev(k), rev(v), rev(i_b), rev(jax.nn.log_sigmoid(f_b)),
                                    C0[:, 1], n0[:, 1], m0[:, 1])
    h = h_f + rev(h_b)
    h = h * lax.rsqrt(jnp.mean(h * h, axis=-1, keepdims=True) + NORM_EPS)
    h = h.transpose(0, 2, 1, 3).reshape(bsz, n_tok, M_V_W) * norm_w.astype(f32)
    h = h * jax.nn.sigmoid(mo.astype(f32))
    states = (jnp.stack([C_f, C_b], axis=1), jnp.stack([n_f, n_b], axis=1), jnp.stack([m_f, m_b], axis=1))
    return h.astype(mq.dtype), states


def attn_qkv(aq, ak, av, q_norm_w, k_norm_w):
    bsz, n_tok, _ = aq.shape
    q = rms_norm(aq.reshape(bsz, n_tok, A_HEADS, A_HD), q_norm_w)
    k = rms_norm(ak.reshape(bsz, n_tok, A_KV, A_HD), k_norm_w)
    v = av.reshape(bsz, n_tok, A_KV, A_HD)
    return q, k, v


def context_attention(q, k, v, sink):
    bsz, n_tok = q.shape[:2]
    qg = q.reshape(bsz, n_tok, A_KV, A_GROUP, A_HD)
    s = jnp.einsum('btkgd,bskd->bkgts', qg, k, preferred_element_type=jnp.float32) * (A_HD ** -0.5)
    sk = jnp.broadcast_to(sink.astype(jnp.float32).reshape(1, A_KV, A_GROUP, 1, 1), s.shape[:-1] + (1,))
    p = jax.nn.softmax(jnp.concatenate([s, sk], axis=-1), axis=-1)[..., :-1]
    o = jnp.einsum('bkgts,bskd->btkgd', p.astype(v.dtype), v)
    return o.reshape(bsz, n_tok, A_Q_W)


def latent_attention(q, k, v, ck, cv, sink):
    bsz, n_tok = q.shape[:2]
    W = WINDOW
    nb = n_tok // W
    n_ctx = ck.shape[1]
    qb = q.reshape(bsz, nb, W, A_KV, A_GROUP, A_HD)
    pad = ((0, 0), (W, W), (0, 0), (0, 0))
    kb = jnp.pad(k, pad).reshape(bsz, nb + 2, W, A_KV, A_HD)
    vb = jnp.pad(v, pad).reshape(bsz, nb + 2, W, A_KV, A_HD)
    kw = jnp.concatenate([kb[:, :-2], kb[:, 1:-1], kb[:, 2:]], axis=2)
    vw = jnp.concatenate([vb[:, :-2], vb[:, 1:-1], vb[:, 2:]], axis=2)
    scale = A_HD ** -0.5
    s_loc = jnp.einsum('bnikgd,bnjkd->bkgnij', qb, kw, preferred_element_type=jnp.float32) * scale
    blk = jnp.arange(nb)[:, None, None]
    qpos = blk * W + jnp.arange(W)[None, :, None]
    kpos = (blk - 1) * W + jnp.arange(3 * W)[None, None, :]
    valid = (jnp.abs(kpos - qpos) <= WINDOW) & (kpos >= 0) & (kpos < n_tok)
    s_loc = jnp.where(valid, s_loc, -jnp.inf)
    s_ctx = jnp.einsum('bnikgd,bskd->bkgnis', qb, ck, preferred_element_type=jnp.float32) * scale
    sk = jnp.broadcast_to(sink.astype(jnp.float32).reshape(1, A_KV, A_GROUP, 1, 1, 1), s_loc.shape[:-1] + (1,))
    p = jax.nn.softmax(jnp.concatenate([s_loc, s_ctx, sk], axis=-1), axis=-1)
    p_loc = p[..., :3 * W].astype(v.dtype)
    p_ctx = p[..., 3 * W:3 * W + n_ctx].astype(v.dtype)
    o = jnp.einsum('bkgnij,bnjkd->bnikgd', p_loc, vw) + jnp.einsum('bkgnis,bskd->bnikgd', p_ctx, cv)
    return o.reshape(bsz, n_tok, A_Q_W)


def merge_branches(hm, ha, gm, ga, w_proj_m, w_proj_a, w_out):
    return (jax.nn.sigmoid(gm) * (hm @ w_proj_m) + jax.nn.sigmoid(ga) * (ha @ w_proj_a)) @ w_out


def setup_inputs(seed: int = 0) -> dict:
    key = jax.random.key(seed)
    ks = jax.random.split(key, 40)
    f32 = jnp.float32

    def nrm(k, shape, scale):
        return jax.random.normal(k, shape, f32) * scale

    gate_b = jnp.concatenate([
        nrm(ks[30], (DEPTH, M_HEADS), 0.1),
        3.0 + nrm(ks[31], (DEPTH, M_HEADS), 0.5),
        nrm(ks[32], (DEPTH, M_HEADS), 0.1),
        3.0 + nrm(ks[33], (DEPTH, M_HEADS), 0.5)], axis=-1)
    return {
        'x_prompt': nrm(ks[0], (BATCH, SEQ, D_MODEL), 1.0),
        'x_sample': nrm(ks[1], (DEC_BATCH, DEC_SEQ, D_MODEL), 1.0),
        'cache_attn_k': nrm(ks[2], (DEC_BATCH, DEPTH, PAST_LEN, A_KV, A_HD), 1.0),
        'cache_attn_v': nrm(ks[3], (DEC_BATCH, DEPTH, PAST_LEN, A_KV, A_HD), 1.0),
        'state_mlstm_C': nrm(ks[4], (DEC_BATCH, DEPTH, 2, M_HEADS, M_V, M_QK), 1.0),
        'state_mlstm_n': nrm(ks[5], (DEC_BATCH, DEPTH, 2, M_HEADS, M_QK), 1.0),
        'state_mlstm_m': nrm(ks[6], (DEC_BATCH, DEPTH, 2, M_HEADS), 1.0),
        'c': nrm(ks[7], (DEC_BATCH, D_MODEL), 1.0),
        'c_ctx': nrm(ks[8], (D_MODEL,), 1.0),
        'ada_w': nrm(ks[9], (DEPTH, D_MODEL, N_MOD * D_MODEL), D_MODEL ** -0.5),
        'ada_b': nrm(ks[10], (DEPTH, N_MOD * D_MODEL), 0.01),
        'norm_ffn1_w': 1.0 + nrm(ks[11], (DEPTH, D_MODEL), 0.05),
        'ffn1_wi': nrm(ks[12], (DEPTH, D_MODEL, 2 * D_FF), D_MODEL ** -0.5),
        'ffn1_wo': nrm(ks[13], (DEPTH, D_FF, D_MODEL), D_FF ** -0.5),
        'norm_mix_w': 1.0 + nrm(ks[14], (DEPTH, D_MODEL), 0.05),
        'w_in': nrm(ks[15], (DEPTH, D_MODEL, IN_COLS), D_MODEL ** -0.5),
        'mlstm_gate_b': gate_b,
        'mlstm_norm_w': 1.0 + nrm(ks[16], (DEPTH, M_V_W), 0.05),
        'attn_q_norm_w': 1.0 + nrm(ks[17], (DEPTH, A_HD), 0.05),
        'attn_k_norm_w': 1.0 + nrm(ks[18], (DEPTH, A_HD), 0.05),
        'attn_sink': nrm(ks[19], (DEPTH, A_HEADS), 0.5),
        'w_proj_m': nrm(ks[20], (DEPTH, M_V_W, D_MODEL), M_V_W ** -0.5),
        'w_proj_a': nrm(ks[21], (DEPTH, A_Q_W, D_MODEL), A_Q_W ** -0.5),
        'w_out': nrm(ks[22], (DEPTH, D_MODEL, D_MODEL), D_MODEL ** -0.5),
        'norm_ffn2_w': 1.0 + nrm(ks[23], (DEPTH, D_MODEL), 0.05),
        'ffn2_wi': nrm(ks[24], (DEPTH, D_MODEL, 2 * D_FF), D_MODEL ** -0.5),
        'ffn2_wo': nrm(ks[25], (DEPTH, D_FF, D_MODEL), D_FF ** -0.5),
    }


def reference(x_prompt, x_sample, cache_attn_k, cache_attn_v, state_mlstm_C, state_mlstm_n, state_mlstm_m,
              c, c_ctx, ada_w, ada_b, norm_ffn1_w, ffn1_wi, ffn1_wo, norm_mix_w, w_in, mlstm_gate_b,
              mlstm_norm_w, attn_q_norm_w, attn_k_norm_w, attn_sink, w_proj_m, w_proj_a, w_out,
              norm_ffn2_w, ffn2_wi, ffn2_wo):
    f32 = jnp.float32
    xc = x_prompt
    bp = x_prompt.shape[0]
    zero_C = jnp.zeros((bp, 2, M_HEADS, M_V, M_QK), f32)
    zero_n = jnp.zeros((bp, 2, M_HEADS, M_QK), f32)
    zero_m = jnp.zeros((bp, 2, M_HEADS), f32)
    ks_out, vs_out, Cs_out, ns_out, ms_out = [], [], [], [], []
    for l in range(DEPTH):
        sh1, sc1, g1, sh2, sc2, g2, sh3, sc3, g3 = modulation(c_ctx, ada_w[l], ada_b[l])
        xc = ffn_sublayer(xc, sh1, sc1, g1, norm_ffn1_w[l], ffn1_wi[l], ffn1_wo[l])
        mq, mk, mv, mo, mg, aq, ak, av, gm, ga = mixer_inputs(xc, sh2, sc2, norm_mix_w[l], w_in[l])
        hm, (C_l, n_l, m_l) = mlstm_branch(mq, mk, mv, mo, mg, mlstm_gate_b[l], mlstm_norm_w[l],
                                           zero_C, zero_n, zero_m)
        q, k, v = attn_qkv(aq, ak, av, attn_q_norm_w[l], attn_k_norm_w[l])
        ha = context_attention(q, k, v, attn_sink[l])
        xc = xc + g2 * merge_branches(hm, ha, gm, ga, w_proj_m[l], w_proj_a[l], w_out[l])
        xc = ffn_sublayer(xc, sh3, sc3, g3, norm_ffn2_w[l], ffn2_wi[l], ffn2_wo[l])
        ks_out.append(k)
        vs_out.append(v)
        Cs_out.append(C_l)
        ns_out.append(n_l)
        ms_out.append(m_l)
    y_prompt = xc
    new_attn_k = jnp.stack(ks_out, axis=1)
    new_attn_v = jnp.stack(vs_out, axis=1)
    new_mlstm_C = jnp.stack(Cs_out, axis=1)
    new_mlstm_n = jnp.stack(ns_out, axis=1)
    new_mlstm_m = jnp.stack(ms_out, axis=1)

    xs = x_sample
    for l in range(DEPTH):
        sh1, sc1, g1, sh2, sc2, g2, sh3, sc3, g3 = modulation(c, ada_w[l], ada_b[l])
        xs = ffn_sublayer(xs, sh1, sc1, g1, norm_ffn1_w[l], ffn1_wi[l], ffn1_wo[l])
        mq, mk, mv, mo, mg, aq, ak, av, gm, ga = mixer_inputs(xs, sh2, sc2, norm_mix_w[l], w_in[l])
        hm, _ = mlstm_branch(mq, mk, mv, mo, mg, mlstm_gate_b[l], mlstm_norm_w[l],
                             state_mlstm_C[:, l], state_mlstm_n[:, l], state_mlstm_m[:, l])
        q, k, v = attn_qkv(aq, ak, av, attn_q_norm_w[l], attn_k_norm_w[l])
        q = axial_rope(q)
        k = axial_rope(k)
        ha = latent_attention(q, k, v, cache_attn_k[:, l], cache_attn_v[:, l], attn_sink[l])
        xs = xs + g2 * merge_branches(hm, ha, gm, ga, w_proj_m[l], w_proj_a[l], w_out[l])
        xs = ffn_sublayer(xs, sh3, sc3, g3, norm_ffn2_w[l], ffn2_wi[l], ffn2_wo[l])
    y_sample = xs
    return (y_prompt, y_sample, new_attn_k, new_attn_v, new_mlstm_C, new_mlstm_n, new_mlstm_m)
```

```python
import functools

import jax
import jax.numpy as jnp
from jax import lax
from jax.experimental import pallas as pl
from jax.experimental.pallas import tpu as pltpu

F32 = jnp.float32
BF16 = jnp.bfloat16

D_MODEL = 2048
D_FF = 5632
N_MOD = 9
NORM_EPS = 1e-6
GRID_W = 64
ROPE_THETA = 10000.0
M_HEADS = 4
M_QK = 128
M_V = 256
M_GATES = 4 * M_HEADS
A_HEADS = 16
A_KV = 4
A_GROUP = A_HEADS // A_KV
A_HD = 64
A_Q_W = A_HEADS * A_HD
A_KV_W = A_KV * A_HD
WINDOW = 128

Z_GM, Z_GA, Z_MV, Z_MO, Z_AQ, Z_MQ, Z_MK, Z_AK, Z_AV = 0, 2048, 4096, 5120, 6144, 7168, 7680, 8192, 8448
Z_COLS = 8704
GATE_LANES = 128
GATE_ROWS = 8

MLSTM_CHUNK = 256

VMEM_LIMIT = 56 * 1024 * 1024


def _cparams(*sem):
    return pltpu.CompilerParams(dimension_semantics=sem, vmem_limit_bytes=VMEM_LIMIT)


def _sigmoid(x):
    return 1.0 / (1.0 + jnp.exp(-x))


def _norm_modulate(x, norm_w, shift, scale):
    ms = jnp.mean(x * x, axis=-1, keepdims=True)
    y = x * lax.rsqrt(ms + NORM_EPS) * norm_w
    return y * (1.0 + scale) + shift


def _mod_kernel(c_ref, w_ref, b_ref, o_ref):
    c = c_ref[...]
    s = (c * _sigmoid(c)).astype(BF16)
    o_ref[...] = jnp.dot(s, w_ref[...].astype(BF16), preferred_element_type=F32) + b_ref[...]


def _modulation(cond, ada_w, ada_b, tn=1024):
    rows, d = cond.shape
    n = ada_w.shape[1]
    return pl.pallas_call(
        _mod_kernel,
        grid=(n // tn,),
        in_specs=[
            pl.BlockSpec((rows, d), lambda j: (0, 0)),
            pl.BlockSpec((d, tn), lambda j: (0, j)),
            pl.BlockSpec((1, tn), lambda j: (0, j)),
        ],
        out_specs=pl.BlockSpec((rows, tn), lambda j: (0, j)),
        out_shape=jax.ShapeDtypeStruct((rows, n), F32),
        compiler_params=_cparams("arbitrary"),
        name="modulation",
    )(cond, ada_w, ada_b)


def _ffn_kernel(x_ref, mod_ref, nw_ref, wg_ref, wu_ref, wo_ref, o_ref, h_ref, *, k_shift, k_scale, k_gate):
    j = pl.program_id(1)

    @pl.when(j == 0)
    def _():
        h = _norm_modulate(x_ref[...], nw_ref[...], mod_ref[0, k_shift:k_shift + 1, :],
                           mod_ref[0, k_scale:k_scale + 1, :])
        h_ref[...] = h.astype(BF16)
        o_ref[...] = jnp.zeros_like(o_ref)

    h = h_ref[...]
    g = jnp.dot(h, wg_ref[...], preferred_element_type=F32)
    u = jnp.dot(h, wu_ref[...], preferred_element_type=F32)
    a = (g * _sigmoid(g) * u).astype(BF16)
    o_ref[...] += jnp.dot(a, wo_ref[...], preferred_element_type=F32)

    @pl.when(j == pl.num_programs(1) - 1)
    def _():
        gate = mod_ref[0, k_gate:k_gate + 1, :]
        o_ref[...] = x_ref[...] + 0.5 * gate * o_ref[...]


def _ffn(x, mod3, row_of_tile, norm_w, wi, wo, k_shift, tm, tf=512):
    n_tok, d = x.shape
    nf = D_FF // tf
    kern = functools.partial(_ffn_kernel, k_shift=k_shift, k_scale=k_shift + 1, k_gate=k_shift + 2)
    return pl.pallas_call(
        kern,
        grid=(n_tok // tm, nf),
        in_specs=[
            pl.BlockSpec((tm, d), lambda i, j: (i, 0)),
            pl.BlockSpec((1, N_MOD, d), lambda i, j: (row_of_tile(i, tm), 0, 0)),
            pl.BlockSpec((1, d), lambda i, j: (0, 0)),
            pl.BlockSpec((d, tf), lambda i, j: (0, j)),
            pl.BlockSpec((d, tf), lambda i, j: (0, j + nf)),
            pl.BlockSpec((tf, d), lambda i, j: (j, 0)),
        ],
        out_specs=pl.BlockSpec((tm, d), lambda i, j: (i, 0)),
        out_shape=jax.ShapeDtypeStruct((n_tok, d), F32),
        scratch_shapes=[pltpu.VMEM((tm, d), BF16)],
        compiler_params=_cparams("parallel", "arbitrary"),
        name="ffn",
    )(x, mod3, norm_w, wi, wi, wo)


def _proj_kernel(x_ref, mod_ref, nw_ref, w_ref, wgate_ref, z_ref, g_ref, h_ref):
    j = pl.program_id(1)

    @pl.when(j == 0)
    def _():
        h = _norm_modulate(x_ref[...], nw_ref[...], mod_ref[0, 3:4, :], mod_ref[0, 4:5, :]).astype(BF16)
        h_ref[...] = h
        g_ref[...] = jnp.dot(h, wgate_ref[...], preferred_element_type=F32)

    z_ref[...] = jnp.dot(h_ref[...], w_ref[...], preferred_element_type=F32).astype(BF16)


def _mixer_proj(x, mod3, row_of_tile, norm_w, w_main, w_gate, tm, tn=512):
    n_tok, d = x.shape
    return pl.pallas_call(
        _proj_kernel,
        grid=(n_tok // tm, Z_COLS // tn),
        in_specs=[
            pl.BlockSpec((tm, d), lambda i, j: (i, 0)),
            pl.BlockSpec((1, N_MOD, d), lambda i, j: (row_of_tile(i, tm), 0, 0)),
            pl.BlockSpec((1, d), lambda i, j: (0, 0)),
            pl.BlockSpec((d, tn), lambda i, j: (0, j)),
            pl.BlockSpec((d, GATE_LANES), lambda i, j: (0, 0)),
        ],
        out_specs=[
            pl.BlockSpec((tm, tn), lambda i, j: (i, j)),
            pl.BlockSpec((tm, GATE_LANES), lambda i, j: (i, 0)),
        ],
        out_shape=[
            jax.ShapeDtypeStruct((n_tok, Z_COLS), BF16),
            jax.ShapeDtypeStruct((n_tok, GATE_LANES), F32),
        ],
        scratch_shapes=[pltpu.VMEM((tm, d), BF16)],
        compiler_params=_cparams("parallel", "arbitrary"),
        name="mixer_proj",
    )(x, mod3, norm_w, w_main, w_gate)


def _cumsum_lanes(x, reverse):
    n = x.shape[-1]
    lane = lax.broadcasted_iota(jnp.int32, x.shape, 1)
    s = 1
    while s < n:
        if reverse:
            x = x + jnp.where(lane < n - s, pltpu.roll(x, n - s, 1), 0.0)
        else:
            x = x + jnp.where(lane >= s, pltpu.roll(x, s, 1), 0.0)
        s *= 2
    return x


def _row_to_col(r, eye):
    return jnp.sum(jnp.where(eye, r, 0.0), axis=1, keepdims=True)


def _log_sigmoid(x):
    return jnp.minimum(x, 0.0) - jnp.log1p(jnp.exp(-jnp.abs(x)))


def _mlstm_chunk(q, qf, k, kf, v, vf, ig, b, b_last, state, causal, eye, update_state):
    C, n, m = state
    b_col = _row_to_col(b, eye)
    dmat = jnp.where(causal, b_col - b + ig, -jnp.inf)
    inter = b_col + m
    m_t = jnp.maximum(inter, jnp.max(dmat, axis=1, keepdims=True))
    s = lax.dot_general(q, k, (((1,), (1,)), ((), ())), preferred_element_type=F32) * jnp.exp(dmat - m_t)
    num = jnp.dot(s.astype(BF16), v, preferred_element_type=F32)
    den = jnp.sum(s, axis=1, keepdims=True)
    if C is not None:
        decay = jnp.exp(inter - m_t)
        num = num + decay * lax.dot_general(q, C.astype(BF16), (((1,), (1,)), ((), ())),
                                            preferred_element_type=F32)
        den = den + decay * jnp.sum(qf * n, axis=1, keepdims=True)
    h = num / jnp.maximum(jnp.abs(den), jnp.exp(-m_t))
    if not update_state:
        return h, state
    g = b_last - b + ig
    m_new = jnp.maximum(b_last + m, jnp.max(g, axis=1, keepdims=True))
    wk_col = _row_to_col(jnp.exp(g - m_new), eye)
    c_add = lax.dot_general((vf * wk_col).astype(BF16), k, (((0,), (0,)), ((), ())),
                            preferred_element_type=F32)
    n_add = jnp.sum(kf * wk_col, axis=0, keepdims=True)
    if C is not None:
        carry = jnp.exp(b_last + m - m_new)
        c_add = carry * C + c_add
        n_add = carry * n + n_add
    return h, (c_add, n_add, m_new)


def _mlstm_kernel(*refs, n_tok, has_state):
    if has_state:
        (q_ref, k_ref, v_ref, og_ref, gr_ref, gb_ref, nw_ref, c0_ref, n0_ref, m0_ref,
         o_ref, hf_ref, hb_ref) = refs
    else:
        (q_ref, k_ref, v_ref, og_ref, gr_ref, gb_ref, nw_ref,
         o_ref, c_out, n_out, m_out, hf_ref, hb_ref) = refs
    L = min(MLSTM_CHUNK, n_tok)
    nc = n_tok // L
    row = lax.broadcasted_iota(jnp.int32, (L, L), 0)
    col = lax.broadcasted_iota(jnp.int32, (L, L), 1)
    eye = row == col
    causal = (col <= row, col >= row)

    if has_state:
        state = [(c0_ref[0, d, 0], n0_ref[0, d, 0], m0_ref[0, d, 0]) for d in range(2)]
    else:
        state = [(None, None, jnp.zeros((1, 1), F32)) for _ in range(2)]

    for step in range(nc):
        for d in range(2):
            c = step if d == 0 else nc - 1 - step
            sl = pl.ds(c * L, L)
            q = q_ref[0, sl, :]
            qf = q.astype(F32) * (M_QK ** -0.5)
            k = k_ref[0, sl, :]
            v = v_ref[0, sl, :]
            gates = gr_ref[0, 0, :, sl] + gb_ref[0]
            ig = gates[2 * d:2 * d + 1, :]
            b = _cumsum_lanes(_log_sigmoid(gates[2 * d + 1:2 * d + 2, :]), reverse=(d == 1))
            b_last = b[:, L - 1:L] if d == 0 else b[:, 0:1]
            update = (not has_state) or step < nc - 1
            h, state[d] = _mlstm_chunk(qf.astype(BF16), qf, k, k.astype(F32), v, v.astype(F32), ig, b, b_last,
                                       state[d], causal[d], eye, update)
            (hf_ref if d == 0 else hb_ref)[sl, :] = h

    h = hf_ref[...] + hb_ref[...]
    h = h * lax.rsqrt(jnp.mean(h * h, axis=-1, keepdims=True) + NORM_EPS) * nw_ref[...]
    o_ref[0] = (h * _sigmoid(og_ref[0].astype(F32))).astype(BF16)

    if not has_state:
        for d in range(2):
            c_out[0, d, 0] = state[d][0]
            n_out[0, d, 0] = state[d][1]
            m_out[0, d, 0] = state[d][2]


def _mlstm(z3, gate_rows, gate_bias, norm_w, states):
    bsz, n_tok, _ = z3.shape
    has_state = states is not None
    in_specs = [
        pl.BlockSpec((1, n_tok, M_QK), lambda b, h: (b, 0, Z_MQ // M_QK + h)),
        pl.BlockSpec((1, n_tok, M_QK), lambda b, h: (b, 0, Z_MK // M_QK + h)),
        pl.BlockSpec((1, n_tok, M_V), lambda b, h: (b, 0, Z_MV // M_V + h)),
        pl.BlockSpec((1, n_tok, M_V), lambda b, h: (b, 0, Z_MO // M_V + h)),
        pl.BlockSpec((1, 1, GATE_ROWS, n_tok), lambda b, h: (b, h, 0, 0)),
        pl.BlockSpec((1, GATE_ROWS, 1), lambda b, h: (h, 0, 0)),
        pl.BlockSpec((1, M_V), lambda b, h: (0, h)),
    ]
    state_specs = [
        pl.BlockSpec((1, 2, 1, M_V, M_QK), lambda b, h: (b, 0, h, 0, 0)),
        pl.BlockSpec((1, 2, 1, 1, M_QK), lambda b, h: (b, 0, h, 0, 0)),
        pl.BlockSpec((1, 2, 1, 1, 1), lambda b, h: (b, 0, h, 0, 0)),
    ]
    out_specs = [pl.BlockSpec((1, n_tok, M_V), lambda b, h: (b, 0, h))]
    out_shape = [jax.ShapeDtypeStruct((bsz, n_tok, M_HEADS * M_V), BF16)]
    args = [z3, z3, z3, z3, gate_rows, gate_bias, norm_w]
    if has_state:
        in_specs += state_specs
        args += list(states)
    else:
        out_specs += state_specs
        out_shape += [
            jax.ShapeDtypeStruct((bsz, 2, M_HEADS, M_V, M_QK), F32),
            jax.ShapeDtypeStruct((bsz, 2, M_HEADS, 1, M_QK), F32),
            jax.ShapeDtypeStruct((bsz, 2, M_HEADS, 1, 1), F32),
        ]
    return pl.pallas_call(
        functools.partial(_mlstm_kernel, n_tok=n_tok, has_state=has_state),
        grid=(bsz, M_HEADS),
        in_specs=in_specs,
        out_specs=out_specs,
        out_shape=out_shape,
        scratch_shapes=[pltpu.VMEM((n_tok, M_V), F32), pltpu.VMEM((n_tok, M_V), F32)],
        compiler_params=_cparams("parallel", "arbitrary"),
        name="mlstm",
    )(*args)


def _group_rms(x, w_row, lane_head):
    x2 = x * x
    scale = jnp.zeros_like(x)
    for g in range(A_KV_W // A_HD):
        msk = lane_head == g
        ss = jnp.sum(jnp.where(msk, x2, 0.0), axis=1, keepdims=True)
        scale = jnp.where(msk, lax.rsqrt(ss * (1.0 / A_HD) + NORM_EPS), scale)
    return x * scale * w_row


def _rope(x, cos, sin_signed, lane_lo):
    n = x.shape[-1]
    partner = jnp.where(lane_lo, pltpu.roll(x, n - 16, 1), pltpu.roll(x, 16, 1))
    return x * cos + partner * sin_signed


def _softmax_attend(q, sink_col, parts):
    scores = []
    m = sink_col
    for k, _, bias in parts:
        s = lax.dot_general(q, k, (((1,), (1,)), ((), ())), preferred_element_type=F32)
        if bias is not None:
            s = s + bias
        scores.append(s)
        m = jnp.maximum(m, jnp.max(s, axis=1, keepdims=True))
    den = jnp.exp(sink_col - m)
    acc = None
    for s, (_, v, _) in zip(scores, parts):
        p = jnp.exp(s - m)
        den = den + jnp.sum(p, axis=1, keepdims=True)
        pv = jnp.dot(p.astype(BF16), v, preferred_element_type=F32)
        acc = pv if acc is None else acc + pv
    return acc / den


def _attn_kernel(*refs, n_tok, latent):
    if latent:
        (sink_ref, q_ref, k_ref, v_ref, qw_ref, kw_ref, cos_ref, sin_ref, ck_ref, cv_ref,
         o_ref, qh_ref, kp_ref, vp_ref) = refs
    else:
        (sink_ref, q_ref, k_ref, v_ref, qw_ref, kw_ref,
         o_ref, ko_ref, vo_ref) = refs

    lane = lax.broadcasted_iota(jnp.int32, (1, A_KV_W), 1)
    lane_head = lane // A_HD
    lane_lo = (lane % 32) < 16

    k_all = _group_rms(k_ref[0].astype(F32), kw_ref[...], lane_head)
    v_all = v_ref[0]
    if latent:
        k_all = _rope(k_all, cos_ref[...], sin_ref[...], lane_lo)
    else:
        ko_ref[0] = k_all
        vo_ref[0] = v_all.astype(F32)

    q_blk = WINDOW if latent else n_tok
    m_rows = A_GROUP * q_blk
    head_of_row = lax.broadcasted_iota(jnp.int32, (m_rows, 1), 0) // q_blk

    def sink_column(g):
        col = jnp.zeros((m_rows, 1), F32)
        for a in range(A_GROUP):
            col = jnp.where(head_of_row == a, sink_ref[A_GROUP * g + a], col)
        return col

    def unstack(o):
        return jnp.concatenate([o[a * q_blk:(a + 1) * q_blk, :] for a in range(A_GROUP)], axis=1).astype(BF16)

    if latent:
        nb = n_tok // WINDOW
        kw = 3 * WINDOW
        zpad = jnp.zeros((WINDOW, A_HD), BF16)
        r = lax.broadcasted_iota(jnp.int32, (m_rows, kw), 0) % WINDOW
        c = lax.broadcasted_iota(jnp.int32, (m_rows, kw), 1)
        band = jnp.where(c >= r, jnp.where(c <= r + 2 * WINDOW, 0.0, -jnp.inf), -jnp.inf)
        c_row = lax.broadcasted_iota(jnp.int32, (1, kw), 1)

    for g in range(A_KV):
        qg = q_ref[0, :, A_KV_W * g:A_KV_W * (g + 1)].astype(F32)
        qg = _group_rms(qg, qw_ref[...], lane_head)
        if latent:
            qg = _rope(qg, cos_ref[...], sin_ref[...], lane_lo)
        qg = qg * (A_HD ** -0.5)
        kg = k_all[:, A_HD * g:A_HD * (g + 1)].astype(BF16)
        vg = v_all[:, A_HD * g:A_HD * (g + 1)]
        sink_col = sink_column(g)
        if not latent:
            q_stack = jnp.concatenate([qg[:, A_HD * a:A_HD * (a + 1)] for a in range(A_GROUP)], axis=0)
            o = _softmax_attend(q_stack.astype(BF16), sink_col, [(kg, vg, None)])
            o_ref[0, :, A_KV_W * g:A_KV_W * (g + 1)] = unstack(o)
            continue

        for a in range(A_GROUP):
            qh_ref[a] = qg[:, A_HD * a:A_HD * (a + 1)].astype(BF16)
        kp_ref[0:WINDOW, :] = zpad
        kp_ref[WINDOW:WINDOW + n_tok, :] = kg
        kp_ref[WINDOW + n_tok:, :] = zpad
        vp_ref[0:WINDOW, :] = zpad
        vp_ref[WINDOW:WINDOW + n_tok, :] = vg
        vp_ref[WINDOW + n_tok:, :] = zpad
        ckg = ck_ref[0][:, A_HD * g:A_HD * (g + 1)].astype(BF16)
        cvg = cv_ref[0][:, A_HD * g:A_HD * (g + 1)].astype(BF16)

        def q_block(i, carry, g=g, sink_col=sink_col, ckg=ckg, cvg=cvg):
            r0 = pl.multiple_of(i * WINDOW, WINDOW)
            q_stack = jnp.concatenate([qh_ref[a, pl.ds(r0, WINDOW), :] for a in range(A_GROUP)], axis=0)
            k_loc = kp_ref[pl.ds(r0, kw), :]
            v_loc = vp_ref[pl.ds(r0, kw), :]
            kpos = c_row + (i - 1) * WINDOW
            in_range = jnp.where(kpos >= 0, jnp.where(kpos < n_tok, 0.0, -jnp.inf), -jnp.inf)
            o = _softmax_attend(q_stack, sink_col, [(k_loc, v_loc, band + in_range), (ckg, cvg, None)])
            o_ref[0, pl.ds(r0, WINDOW), A_KV_W * g:A_KV_W * (g + 1)] = unstack(o)
            return carry

        lax.fori_loop(0, nb, q_block, 0)


def _attention(z3, sink, q_norm_w, k_norm_w, rope_tabs, cache):
    bsz, n_tok, _ = z3.shape
    latent = cache is not None
    in_specs = [
        pl.BlockSpec(memory_space=pltpu.SMEM),
        pl.BlockSpec((1, n_tok, A_Q_W), lambda b: (b, 0, Z_AQ // A_Q_W)),
        pl.BlockSpec((1, n_tok, A_KV_W), lambda b: (b, 0, Z_AK // A_KV_W)),
        pl.BlockSpec((1, n_tok, A_KV_W), lambda b: (b, 0, Z_AV // A_KV_W)),
        pl.BlockSpec((1, A_KV_W), lambda b: (0, 0)),
        pl.BlockSpec((1, A_KV_W), lambda b: (0, 0)),
    ]
    args = [sink, z3, z3, z3, q_norm_w, k_norm_w]
    out_specs = [pl.BlockSpec((1, n_tok, A_Q_W), lambda b: (b, 0, 0))]
    out_shape = [jax.ShapeDtypeStruct((bsz, n_tok, A_Q_W), BF16)]
    scratch = []
    if latent:
        n_ctx = cache[0].shape[1]
        in_specs += [
            pl.BlockSpec((n_tok, A_KV_W), lambda b: (0, 0)),
            pl.BlockSpec((n_tok, A_KV_W), lambda b: (0, 0)),
            pl.BlockSpec((1, n_ctx, A_KV_W), lambda b: (b, 0, 0)),
            pl.BlockSpec((1, n_ctx, A_KV_W), lambda b: (b, 0, 0)),
        ]
        args += [rope_tabs[0], rope_tabs[1], cache[0], cache[1]]
        scratch = [
            pltpu.VMEM((A_GROUP, n_tok, A_HD), BF16),
            pltpu.VMEM((n_tok + 2 * WINDOW, A_HD), BF16),
            pltpu.VMEM((n_tok + 2 * WINDOW, A_HD), BF16),
        ]
    else:
        out_specs += [pl.BlockSpec((1, n_tok, A_KV_W), lambda b: (b, 0, 0))] * 2
        out_shape += [jax.ShapeDtypeStruct((bsz, n_tok, A_KV_W), F32)] * 2
    return pl.pallas_call(
        functools.partial(_attn_kernel, n_tok=n_tok, latent=latent),
        grid=(bsz,),
        in_specs=in_specs,
        out_specs=out_specs,
        out_shape=out_shape,
        scratch_shapes=scratch,
        compiler_params=_cparams("parallel"),
        name="attention",
    )(*args)


def _merge_kernel(x_ref, mod_ref, hm_ref, ha_ref, gm_ref, ga_ref, wpm_ref, wpa_ref, wo_ref, o_ref):
    pm = jnp.dot(hm_ref[...], wpm_ref[...], preferred_element_type=F32)
    pa = jnp.dot(ha_ref[...], wpa_ref[...], preferred_element_type=F32)
    u = _sigmoid(gm_ref[...].astype(F32)) * pm + _sigmoid(ga_ref[...].astype(F32)) * pa
    mix = jnp.dot(u.astype(BF16), wo_ref[...], preferred_element_type=F32)
    o_ref[...] = x_ref[...] + mod_ref[0, 5:6, :] * mix


def _merge(x, mod3, row_of_tile, hm, ha, z, w_proj_m, w_proj_a, w_out, tm):
    n_tok, d = x.shape
    resident = functools.partial(pl.BlockSpec, pipeline_mode=pl.Buffered(1))
    return pl.pallas_call(
        _merge_kernel,
        grid=(n_tok // tm,),
        in_specs=[
            pl.BlockSpec((tm, d), lambda i: (i, 0)),
            pl.BlockSpec((1, N_MOD, d), lambda i: (row_of_tile(i, tm), 0, 0)),
            pl.BlockSpec((tm, M_HEADS * M_V), lambda i: (i, 0)),
            pl.BlockSpec((tm, A_Q_W), lambda i: (i, 0)),
            pl.BlockSpec((tm, d), lambda i: (i, Z_GM // D_MODEL)),
            pl.BlockSpec((tm, d), lambda i: (i, Z_GA // D_MODEL)),
            resident((M_HEADS * M_V, d), lambda i: (0, 0)),
            resident((A_Q_W, d), lambda i: (0, 0)),
            resident((d, d), lambda i: (0, 0)),
        ],
        out_specs=pl.BlockSpec((tm, d), lambda i: (i, 0)),
        out_shape=jax.ShapeDtypeStruct((n_tok, d), F32),
        compiler_params=_cparams("parallel"),
        name="merge",
    )(x, mod3, hm, ha, z, z, w_proj_m, w_proj_a, w_out)


def _rope_tables(n_tok):
    nf = A_HD // 4
    inv = ROPE_THETA ** (-jnp.arange(nf, dtype=F32) / nf)
    tok = jnp.arange(n_tok)
    pos = jnp.stack([tok // GRID_W, tok % GRID_W], axis=1).astype(F32)
    ang = pos[:, :, None] * inv
    cos = jnp.cos(ang)
    sin = jnp.sin(ang)
    cos_h = jnp.concatenate([cos, cos], axis=-1).reshape(n_tok, A_HD)
    sin_h = jnp.concatenate([-sin, sin], axis=-1).reshape(n_tok, A_HD)
    return jnp.tile(cos_h, (1, A_KV)), jnp.tile(sin_h, (1, A_KV))


def _trunk(x, mod3, row_of_tile, wts, states, rope_tabs, cache, tm_ffn, tm_proj, tm_merge):
    bsz, n_tok, d = x.shape
    x2 = x.reshape(bsz * n_tok, d)
    x2 = _ffn(x2, mod3, row_of_tile, wts["norm1"], wts["wi1"], wts["wo1"], 0, tm_ffn)
    z, gates = _mixer_proj(x2, mod3, row_of_tile, wts["norm_mix"], wts["w_main"], wts["w_gate"], tm_proj)
    z3 = z.reshape(bsz, n_tok, Z_COLS)
    g = gates[:, :M_GATES].reshape(bsz, n_tok, 4, M_HEADS)
    g = jnp.transpose(g, (0, 3, 2, 1))
    g = jnp.pad(g, ((0, 0), (0, 0), (0, GATE_ROWS - 4), (0, 0)))
    m_out = _mlstm(z3, g, wts["gate_bias"], wts["mlstm_norm"], states)
    a_out = _attention(z3, wts["sink"], wts["q_norm"], wts["k_norm"], rope_tabs, cache)
    hm = m_out[0].reshape(bsz * n_tok, M_HEADS * M_V)
    ha = a_out[0].reshape(bsz * n_tok, A_Q_W)
    x2 = _merge(x2, mod3, row_of_tile, hm, ha, z, wts["w_proj_m"], wts["w_proj_a"], wts["w_out"], tm_merge)
    x2 = _ffn(x2, mod3, row_of_tile, wts["norm2"], wts["wi2"], wts["wo2"], 6, tm_ffn)
    return x2.reshape(bsz, n_tok, d), m_out[1:], a_out[1:]


def kernel(x_prompt, x_sample, cache_attn_k, cache_attn_v, state_mlstm_C, state_mlstm_n, state_mlstm_m, c, c_ctx, ada_w, ada_b, norm_ffn1_w, ffn1_wi, ffn1_wo, norm_mix_w, w_in, mlstm_gate_b, mlstm_norm_w, attn_q_norm_w, attn_k_norm_w, attn_sink, w_proj_m, w_proj_a, w_out, norm_ffn2_w, ffn2_wi, ffn2_wo):
    bp, tp, d = x_prompt.shape
    bs, ts, _ = x_sample.shape
    n_ctx = cache_attn_k.shape[2]
    l = 0

    n_rows = 16
    cond = jnp.concatenate([c_ctx[None, :], c, jnp.zeros((n_rows - 1 - bs, d), F32)], axis=0)
    mod3 = _modulation(cond, ada_w[l], ada_b[l][None, :]).reshape(n_rows, N_MOD, d)

    w = w_in[l]
    offs = {}
    acc = 0
    for name, size in (("mq", 512), ("mk", 512), ("mv", 1024), ("mo", 1024), ("mg", M_GATES), ("aq", 1024),
                       ("ak", 256), ("av", 256), ("gm", 2048), ("ga", 2048)):
        offs[name] = (acc, acc + size)
        acc += size
    seg = lambda nm: w[:, offs[nm][0]:offs[nm][1]]
    w_main = jnp.concatenate([seg(nm) for nm in ("gm", "ga", "mv", "mo", "aq", "mq", "mk", "ak", "av")],
                             axis=1).astype(BF16)
    w_gate = jnp.pad(seg("mg"), ((0, 0), (0, GATE_LANES - M_GATES))).astype(BF16)
    gate_bias = jnp.pad(mlstm_gate_b[l].reshape(4, M_HEADS).T, ((0, 0), (0, GATE_ROWS - 4)))[:, :, None]
    wts = dict(
        norm1=norm_ffn1_w[l][None, :], wi1=ffn1_wi[l].astype(BF16), wo1=ffn1_wo[l].astype(BF16),
        norm_mix=norm_mix_w[l][None, :], w_main=w_main, w_gate=w_gate,
        gate_bias=gate_bias, mlstm_norm=mlstm_norm_w[l][None, :],
        sink=attn_sink[l], q_norm=jnp.tile(attn_q_norm_w[l], A_KV)[None, :],
        k_norm=jnp.tile(attn_k_norm_w[l], A_KV)[None, :],
        w_proj_m=w_proj_m[l].astype(BF16), w_proj_a=w_proj_a[l].astype(BF16), w_out=w_out[l].astype(BF16),
        norm2=norm_ffn2_w[l][None, :], wi2=ffn2_wi[l].astype(BF16), wo2=ffn2_wo[l].astype(BF16),
    )

    y_prompt, (c_new, n_new, m_new), (k_new, v_new) = _trunk(
        x_prompt, mod3, lambda i, tm: 0, wts, None, None, None, tm_ffn=512, tm_proj=1024, tm_merge=256)
    new_attn_k = k_new.reshape(bp, 1, tp, A_KV, A_HD)
    new_attn_v = v_new.reshape(bp, 1, tp, A_KV, A_HD)
    new_c = c_new[:, None]
    new_n = n_new.reshape(bp, 1, 2, M_HEADS, M_QK)
    new_m = m_new.reshape(bp, 1, 2, M_HEADS)

    states = (state_mlstm_C[:, l], state_mlstm_n[:, l][:, :, :, None, :], state_mlstm_m[:, l][:, :, :, None, None])
    cache = (cache_attn_k[:, l].reshape(bs, n_ctx, A_KV_W), cache_attn_v[:, l].reshape(bs, n_ctx, A_KV_W))
    y_sample, _, _ = _trunk(
        x_sample, mod3, lambda i, tm: 1 + (i * tm) // ts, wts, states, _rope_tables(ts), cache,
        tm_ffn=512, tm_proj=1024, tm_merge=256)

    return (y_prompt, y_sample, new_attn_k, new_attn_v, new_c, new_n, new_m)
```

```python
import functools

import jax
import jax.numpy as jnp
from jax import lax
from jax.experimental import pallas as pl
from jax.experimental.pallas import tpu as pltpu

F32 = jnp.float32
BF16 = jnp.bfloat16
LOG2E = 1.4426950408889634

D_MODEL = 2048
D_FF = 5632
N_MOD = 9
NORM_EPS = 1e-6
GRID_W = 64
ROPE_THETA = 10000.0
M_HEADS = 4
M_QK = 128
M_V = 256
M_GATES = 4 * M_HEADS
A_HEADS = 16
A_KV = 4
A_GROUP = A_HEADS // A_KV
A_HD = 64
A_Q_W = A_HEADS * A_HD
A_KV_W = A_KV * A_HD
WINDOW = 128

Z_GM, Z_GA, Z_MV, Z_MO, Z_AQ, Z_MQ, Z_MK = 0, 2048, 4096, 5120, 6144, 7168, 7680
Z_COLS = 8192
GATE_LANES = 128
AUX_COLS = GATE_LANES + 2 * A_KV_W
GATE_ROWS = 2 * M_HEADS

MLSTM_BLOCK = 256
XPOSE_ROWS = 128

VMEM_LIMIT = 56 * 1024 * 1024
NT_DIMS = (((1,), (1,)), ((), ()))
TN_DIMS = (((0,), (0,)), ((), ()))


def _cparams(*sem):
    return pltpu.CompilerParams(dimension_semantics=sem, vmem_limit_bytes=VMEM_LIMIT)


def _sigmoid(x):
    return 1.0 / (1.0 + jnp.exp(-x))


def _norm_modulate(x, norm_w, shift, scale):
    ms = jnp.mean(x * x, axis=-1, keepdims=True)
    y = x * lax.rsqrt(ms + NORM_EPS) * norm_w
    return y * (1.0 + scale) + shift


def _mod_kernel(c_ref, w_ref, b_ref, o_ref):
    c = c_ref[...]
    s = (c * _sigmoid(c)).astype(BF16)
    o_ref[...] = jnp.dot(s, w_ref[...].astype(BF16), preferred_element_type=F32) + b_ref[...]


def _modulation(cond, ada_w, ada_b, tn=1024):
    rows, d = cond.shape
    n = ada_w.shape[1]
    return pl.pallas_call(
        _mod_kernel,
        grid=(n // tn,),
        in_specs=[
            pl.BlockSpec((rows, d), lambda j: (0, 0)),
            pl.BlockSpec((d, tn), lambda j: (0, j)),
            pl.BlockSpec((1, tn), lambda j: (0, j)),
        ],
        out_specs=pl.BlockSpec((rows, tn), lambda j: (0, j)),
        out_shape=jax.ShapeDtypeStruct((rows, n), F32),
        compiler_params=_cparams("arbitrary"),
        name="modulation",
    )(cond, ada_w, ada_b)


def _ffn_kernel(x_ref, mod_ref, nw_ref, wg_ref, wu_ref, wo_ref, o_ref, h_ref, *, k_shift, k_scale, k_gate):
    j = pl.program_id(1)

    @pl.when(j == 0)
    def _():
        h = _norm_modulate(x_ref[...], nw_ref[...], mod_ref[0, k_shift:k_shift + 1, :],
                           mod_ref[0, k_scale:k_scale + 1, :])
        h_ref[...] = h.astype(BF16)
        o_ref[...] = jnp.zeros_like(o_ref)

    h = h_ref[...]
    g = jnp.dot(h, wg_ref[...], preferred_element_type=F32)
    u = jnp.dot(h, wu_ref[...], preferred_element_type=F32)
    a = (g * _sigmoid(g) * u).astype(BF16)
    o_ref[...] += jnp.dot(a, wo_ref[...], preferred_element_type=F32)

    @pl.when(j == pl.num_programs(1) - 1)
    def _():
        gate = mod_ref[0, k_gate:k_gate + 1, :]
        o_ref[...] = x_ref[...] + 0.5 * gate * o_ref[...]


def _ffn(x, mod3, row_of_tile, norm_w, wi, wo, k_shift, tm, tf=512):
    n_tok, d = x.shape
    nf = D_FF // tf
    kern = functools.partial(_ffn_kernel, k_shift=k_shift, k_scale=k_shift + 1, k_gate=k_shift + 2)
    return pl.pallas_call(
        kern,
        grid=(n_tok // tm, nf),
        in_specs=[
            pl.BlockSpec((tm, d), lambda i, j: (i, 0), pipeline_mode=pl.Buffered(1)),
            pl.BlockSpec((1, N_MOD, d), lambda i, j: (row_of_tile(i, tm), 0, 0)),
            pl.BlockSpec((1, d), lambda i, j: (0, 0)),
            pl.BlockSpec((d, tf), lambda i, j: (0, j)),
            pl.BlockSpec((d, tf), lambda i, j: (0, j + nf)),
            pl.BlockSpec((tf, d), lambda i, j: (j, 0)),
        ],
        out_specs=pl.BlockSpec((tm, d), lambda i, j: (i, 0)),
        out_shape=jax.ShapeDtypeStruct((n_tok, d), F32),
        scratch_shapes=[pltpu.VMEM((tm, d), BF16)],
        compiler_params=_cparams("parallel", "arbitrary"),
        name="ffn",
    )(x, mod3, norm_w, wi, wi, wo)


def _proj_kernel(x_ref, mod_ref, nw_ref, w_ref, waux_ref, z_ref, g_ref, kv_ref, h_ref):
    j = pl.program_id(1)

    @pl.when(j == 0)
    def _():
        h = _norm_modulate(x_ref[...], nw_ref[...], mod_ref[0, 3:4, :], mod_ref[0, 4:5, :]).astype(BF16)
        h_ref[...] = h
        aux = jnp.dot(h, waux_ref[...], preferred_element_type=F32)
        g_ref[...] = aux[:, :GATE_LANES]
        kv_ref[...] = aux[:, GATE_LANES:].astype(BF16)

    half = z_ref.shape[1] // 2
    h = h_ref[...]
    z_ref[:, :half] = jnp.dot(h, w_ref[:, :half], preferred_element_type=F32).astype(BF16)
    z_ref[:, half:] = jnp.dot(h, w_ref[:, half:], preferred_element_type=F32).astype(BF16)


def _mixer_proj(x, mod3, row_of_tile, norm_w, w_main, w_aux, tm, tn=1024):
    n_tok, d = x.shape
    once = functools.partial(pl.BlockSpec, pipeline_mode=pl.Buffered(1))
    return pl.pallas_call(
        _proj_kernel,
        grid=(n_tok // tm, Z_COLS // tn),
        in_specs=[
            once((tm, d), lambda i, j: (i, 0)),
            pl.BlockSpec((1, N_MOD, d), lambda i, j: (row_of_tile(i, tm), 0, 0)),
            pl.BlockSpec((1, d), lambda i, j: (0, 0)),
            pl.BlockSpec((d, tn), lambda i, j: (0, j)),
            once((d, AUX_COLS), lambda i, j: (0, 0)),
        ],
        out_specs=[
            pl.BlockSpec((tm, tn), lambda i, j: (i, j)),
            pl.BlockSpec((tm, GATE_LANES), lambda i, j: (i, 0)),
            pl.BlockSpec((tm, 2 * A_KV_W), lambda i, j: (i, 0)),
        ],
        out_shape=[
            jax.ShapeDtypeStruct((n_tok, Z_COLS), BF16),
            jax.ShapeDtypeStruct((n_tok, GATE_LANES), F32),
            jax.ShapeDtypeStruct((n_tok, 2 * A_KV_W), BF16),
        ],
        scratch_shapes=[pltpu.VMEM((tm, d), BF16)],
        compiler_params=_cparams("parallel", "arbitrary"),
        name="mixer_proj",
    )(x, mod3, norm_w, w_main, w_aux)


def _scan_lanes(x, op, fill, reverse):
    n = x.shape[-1]
    lane = lax.broadcasted_iota(jnp.int32, x.shape, 1)
    s = 1
    while s < n:
        if reverse:
            x = op(x, jnp.where(lane < n - s, pltpu.roll(x, n - s, 1), fill))
        else:
            x = op(x, jnp.where(lane >= s, pltpu.roll(x, s, 1), fill))
        s *= 2
    return x


def _dir_scan(x, op, fill, fwd_rows):
    return jnp.where(fwd_rows, _scan_lanes(x, op, fill, False), _scan_lanes(x, op, fill, True))


def _log_sigmoid(x):
    return jnp.minimum(x, 0.0) - jnp.log1p(jnp.exp(-jnp.abs(x)))


def _mlstm_kernel(*refs, n_tok, has_state):
    if has_state:
        (q_ref, k_ref, v_ref, og_ref, gi_ref, gf_ref, bi_ref, bf_ref, nw_ref, c0_ref, n0_ref, m0_ref,
         o_ref) = refs
    else:
        (q_ref, k_ref, v_ref, og_ref, gi_ref, gf_ref, bi_ref, bf_ref, nw_ref,
         o_ref, c_out, n_out, m_out) = refs
    blk = min(MLSTM_BLOCK, n_tok)
    nq = n_tok // blk

    fwd_rows = lax.broadcasted_iota(jnp.int32, (GATE_ROWS, 1), 0) < M_HEADS
    ig = gi_ref[0] + bi_ref[...]
    cum_f = _dir_scan(_log_sigmoid(gf_ref[0] + bf_ref[...]), jnp.add, 0.0, fwd_rows)
    a = ig - cum_f
    m0 = m0_ref[0] if has_state else jnp.zeros((GATE_ROWS, 1), F32)
    mx = jnp.maximum(_dir_scan(a, jnp.maximum, -jnp.inf, fwd_rows), m0)
    stats = [-mx, cum_f + mx]
    if not has_state:
        mx_end = jnp.where(fwd_rows, mx[:, n_tok - 1:n_tok], mx[:, 0:1])
        cum_end = jnp.where(fwd_rows, cum_f[:, n_tok - 1:n_tok], cum_f[:, 0:1])
        stats.append(jnp.exp(a - mx_end))
        m_out[0] = cum_end + mx_end
    pad = jnp.zeros((XPOSE_ROWS - GATE_ROWS * len(stats), n_tok), F32)
    cols = jnp.concatenate(stats + [pad], axis=0).T

    row_i = lax.broadcasted_iota(jnp.int32, (blk, blk), 0)
    col_i = lax.broadcasted_iota(jnp.int32, (blk, blk), 1)
    visible = (col_i <= row_i, col_i >= row_i)

    for h in range(M_HEADS):
        qk_cols = slice(h * M_QK, (h + 1) * M_QK)
        v_cols = slice(h * M_V, (h + 1) * M_V)
        qf = q_ref[0, :, qk_cols].astype(F32) * (M_QK ** -0.5)
        q = qf.astype(BF16)
        k = k_ref[0, :, qk_cols]
        v = v_ref[0, :, v_cols]
        for qi in range(nq):
            rows = slice(qi * blk, (qi + 1) * blk)
            h_sum = None
            for d in range(2):
                r = M_HEADS * d + h
                lo, hi = (0, (qi + 1) * blk) if d == 0 else (qi * blk, n_tok)
                dg = hi - lo - blk if d == 0 else 0
                u_col = cols[rows, r:r + 1]
                mt_col = cols[rows, GATE_ROWS + r:GATE_ROWS + r + 1]
                dm = u_col + a[r:r + 1, lo:hi]
                parts = [dm[:, :dg]] if dg > 0 else []
                parts.append(jnp.where(visible[d], dm[:, dg:dg + blk], -jnp.inf))
                if dg + blk < hi - lo:
                    parts.append(dm[:, dg + blk:])
                dm = jnp.concatenate(parts, axis=1) if len(parts) > 1 else parts[0]
                s = lax.dot_general(q[rows], k[lo:hi], NT_DIMS, preferred_element_type=F32) * jnp.exp(dm)
                num = jnp.dot(s.astype(BF16), v[lo:hi], preferred_element_type=F32)
                den = jnp.sum(s, axis=1, keepdims=True)
                if has_state:
                    decay = jnp.exp(u_col + m0[r:r + 1, :])
                    num = num + decay * lax.dot_general(q[rows], c0_ref[0, r].astype(BF16), NT_DIMS,
                                                        preferred_element_type=F32)
                    den = den + decay * jnp.sum(qf[rows] * n0_ref[0, r:r + 1, :], axis=1, keepdims=True)
                h_dir = num / jnp.maximum(jnp.abs(den), jnp.exp(-mt_col))
                h_sum = h_dir if h_sum is None else h_sum + h_dir
            hn = h_sum * lax.rsqrt(jnp.mean(h_sum * h_sum, axis=-1, keepdims=True) + NORM_EPS) * nw_ref[:, v_cols]
            o_ref[0, rows, v_cols] = (hn * _sigmoid(og_ref[0, rows, v_cols].astype(F32))).astype(BF16)
        if not has_state:
            kf = k.astype(F32)
            vf = v.astype(F32)
            for d in range(2):
                r = M_HEADS * d + h
                wk_col = cols[:, 2 * GATE_ROWS + r:2 * GATE_ROWS + r + 1]
                c_out[0, r] = lax.dot_general((vf * wk_col).astype(BF16), k, TN_DIMS, preferred_element_type=F32)
                n_out[0, r:r + 1, :] = jnp.sum(kf * wk_col, axis=0, keepdims=True)


def _mlstm(z3, gate_i, gate_f, bias_i, bias_f, norm_w, states):
    bsz, n_tok, _ = z3.shape
    has_state = states is not None
    qk_w, v_w = M_HEADS * M_QK, M_HEADS * M_V
    in_specs = [
        pl.BlockSpec((1, n_tok, qk_w), lambda b: (b, 0, Z_MQ // qk_w)),
        pl.BlockSpec((1, n_tok, qk_w), lambda b: (b, 0, Z_MK // qk_w)),
        pl.BlockSpec((1, n_tok, v_w), lambda b: (b, 0, Z_MV // v_w)),
        pl.BlockSpec((1, n_tok, v_w), lambda b: (b, 0, Z_MO // v_w)),
        pl.BlockSpec((1, GATE_ROWS, n_tok), lambda b: (b, 0, 0)),
        pl.BlockSpec((1, GATE_ROWS, n_tok), lambda b: (b, 0, 0)),
        pl.BlockSpec((GATE_ROWS, 1), lambda b: (0, 0)),
        pl.BlockSpec((GATE_ROWS, 1), lambda b: (0, 0)),
        pl.BlockSpec((1, v_w), lambda b: (0, 0)),
    ]
    state_specs = [
        pl.BlockSpec((1, GATE_ROWS, M_V, M_QK), lambda b: (b, 0, 0, 0)),
        pl.BlockSpec((1, GATE_ROWS, M_QK), lambda b: (b, 0, 0)),
        pl.BlockSpec((1, GATE_ROWS, 1), lambda b: (b, 0, 0)),
    ]
    out_specs = [pl.BlockSpec((1, n_tok, v_w), lambda b: (b, 0, 0))]
    out_shape = [jax.ShapeDtypeStruct((bsz, n_tok, v_w), BF16)]
    args = [z3, z3, z3, z3, gate_i, gate_f, bias_i, bias_f, norm_w]
    if has_state:
        in_specs += state_specs
        args += list(states)
    else:
        out_specs += state_specs
        out_shape += [
            jax.ShapeDtypeStruct((bsz, GATE_ROWS, M_V, M_QK), F32),
            jax.ShapeDtypeStruct((bsz, GATE_ROWS, M_QK), F32),
            jax.ShapeDtypeStruct((bsz, GATE_ROWS, 1), F32),
        ]
    return pl.pallas_call(
        functools.partial(_mlstm_kernel, n_tok=n_tok, has_state=has_state),
        grid=(bsz,),
        in_specs=in_specs,
        out_specs=out_specs,
        out_shape=out_shape,
        compiler_params=_cparams("parallel"),
        name="mlstm",
    )(*args)


def _group_rms(x, w_row, ones_bd):
    x2 = x * x
    hi = x2.astype(BF16)
    lo = (x2 - hi.astype(F32)).astype(BF16)
    ss = jnp.dot(hi, ones_bd, preferred_element_type=F32) + jnp.dot(lo, ones_bd, preferred_element_type=F32)
    return x * lax.rsqrt(ss * (1.0 / A_HD) + NORM_EPS) * w_row


def _rope(x, cos, sin_signed, lane_lo):
    n = x.shape[-1]
    partner = jnp.where(lane_lo, pltpu.roll(x, n - 16, 1), pltpu.roll(x, 16, 1))
    return x * cos + partner * sin_signed


def _attend_block(q, sink_col, k_loc, v_loc, kinds, masks, k_ctx=None, v_ctx=None):
    s_loc = lax.dot_general(q, k_loc, NT_DIMS, preferred_element_type=F32)
    tiles = []
    for j, kind in enumerate(kinds):
        t = s_loc[:, j * WINDOW:(j + 1) * WINDOW]
        tiles.append(t if kind == "cur" else jnp.where(masks[kind], t, -jnp.inf))
    n_loc = len(tiles)
    if k_ctx is not None:
        s_ctx = lax.dot_general(q, k_ctx, NT_DIMS, preferred_element_type=F32)
        tiles += [s_ctx[:, j * WINDOW:(j + 1) * WINDOW] for j in range(k_ctx.shape[0] // WINDOW)]
    tile_max = tiles[0]
    for t in tiles[1:]:
        tile_max = jnp.maximum(tile_max, t)
    m = jnp.maximum(jnp.max(tile_max, axis=1, keepdims=True), sink_col)
    p = [jnp.exp2(t - m).astype(BF16) for t in tiles]
    acc = jnp.dot(jnp.concatenate(p[:n_loc], axis=1), v_loc, preferred_element_type=F32)
    if k_ctx is not None:
        acc = acc + jnp.dot(jnp.concatenate(p[n_loc:], axis=1), v_ctx, preferred_element_type=F32)
    den = acc[:, A_HD:A_HD + 1] + jnp.exp2(sink_col - m)
    return acc[:, :A_HD] / den


def _attn_kernel(*refs, n_tok, latent):
    if latent:
        (sink_ref, q_ref, k_ref, v_ref, qw_ref, kw_ref, cos_ref, sin_ref, ck_ref, cv_ref,
         o_ref, qh_ref, kg_ref, va_ref) = refs
    else:
        (sink_ref, q_ref, k_ref, v_ref, qw_ref, kw_ref,
         o_ref, ko_ref, vo_ref) = refs

    lane = lax.broadcasted_iota(jnp.int32, (1, A_KV_W), 1)
    lane_lo = (lane % 32) < 16
    bd_r = lax.broadcasted_iota(jnp.int32, (A_KV_W, A_KV_W), 0) // A_HD
    bd_c = lax.broadcasted_iota(jnp.int32, (A_KV_W, A_KV_W), 1) // A_HD
    ones_bd = jnp.where(bd_r == bd_c, 1.0, 0.0).astype(BF16)

    k_all = _group_rms(k_ref[0].astype(F32), kw_ref[...], ones_bd)
    v_all = v_ref[0]
    if latent:
        k_all = _rope(k_all, cos_ref[...], sin_ref[...], lane_lo)
    else:
        ko_ref[0] = k_all
        vo_ref[0] = v_all.astype(F32)

    q_blk = WINDOW if latent else n_tok
    m_rows = A_GROUP * q_blk
    head_of_row = lax.broadcasted_iota(jnp.int32, (m_rows, 1), 0) // q_blk
    r_in = lax.broadcasted_iota(jnp.int32, (m_rows, WINDOW), 0) % WINDOW
    c_in = lax.broadcasted_iota(jnp.int32, (m_rows, WINDOW), 1)
    masks = {"prev": c_in >= r_in, "next": c_in <= r_in}
    ones_v = jnp.ones((n_tok, A_HD), BF16)

    def sink_column(g):
        col = jnp.zeros((m_rows, 1), F32)
        for a in range(A_GROUP):
            col = jnp.where(head_of_row == a, sink_ref[A_GROUP * g + a] * LOG2E, col)
        return col

    def unstack(o):
        return jnp.concatenate([o[a * q_blk:(a + 1) * q_blk, :] for a in range(A_GROUP)], axis=1).astype(BF16)

    for g in range(A_KV):
        g_cols = slice(A_KV_W * g, A_KV_W * (g + 1))
        h_cols = slice(A_HD * g, A_HD * (g + 1))
        qg = _group_rms(q_ref[0, :, g_cols].astype(F32), qw_ref[...], ones_bd)
        if latent:
            qg = _rope(qg, cos_ref[...], sin_ref[...], lane_lo)
        qg = qg * (A_HD ** -0.5 * LOG2E)
        kg = k_all[:, h_cols].astype(BF16)
        va = jnp.concatenate([v_all[:, h_cols], ones_v], axis=1)
        sink_col = sink_column(g)
        if not latent:
            q_stack = jnp.concatenate([qg[:, A_HD * a:A_HD * (a + 1)] for a in range(A_GROUP)], axis=0)
            o = _attend_block(q_stack.astype(BF16), sink_col, kg, va, ["cur"] * (n_tok // WINDOW), masks)
            o_ref[0, :, g_cols] = unstack(o)
            continue

        for a in range(A_GROUP):
            qh_ref[a] = qg[:, A_HD * a:A_HD * (a + 1)].astype(BF16)
        kg_ref[...] = kg
        va_ref[...] = va
        n_ctx = ck_ref.shape[1]
        ckg = ck_ref[0][:, h_cols].astype(BF16)
        cva = jnp.concatenate([cv_ref[0][:, h_cols].astype(BF16), jnp.ones((n_ctx, A_HD), BF16)], axis=1)
        nb = n_tok // WINDOW

        def q_block(i, kinds, k_start, g_cols=g_cols, sink_col=sink_col, ckg=ckg, cva=cva, qg=qg, kg=kg, va=va):
            n_keys = len(kinds) * WINDOW
            if isinstance(i, int):
                r0 = i * WINDOW
                q_stack = jnp.concatenate([qg[r0:r0 + WINDOW, A_HD * a:A_HD * (a + 1)] for a in range(A_GROUP)],
                                          axis=0).astype(BF16)
                k_loc = kg[k_start:k_start + n_keys]
                v_loc = va[k_start:k_start + n_keys]
            else:
                r0 = pl.multiple_of(i * WINDOW, WINDOW)
                k_start = pl.multiple_of(k_start, WINDOW)
                q_stack = jnp.concatenate([qh_ref[a, pl.ds(r0, WINDOW), :] for a in range(A_GROUP)], axis=0)
                k_loc = kg_ref[pl.ds(k_start, n_keys), :]
                v_loc = va_ref[pl.ds(k_start, n_keys), :]
            o = _attend_block(q_stack, sink_col, k_loc, v_loc, kinds, masks, ckg, cva)
            o_ref[0, pl.ds(r0, WINDOW), g_cols] = unstack(o)

        def interior(i, carry, q_block=q_block):
            q_block(i, ["prev", "cur", "next"], (i - 1) * WINDOW)
            return carry

        lax.fori_loop(1, nb - 1, interior, 0, unroll=2)
        q_block(0, ["cur", "next"], 0)
        q_block(nb - 1, ["prev", "cur"], (nb - 2) * WINDOW)


def _attention(z3, kv3, sink, q_norm_w, k_norm_w, rope_tabs, cache):
    bsz, n_tok, _ = z3.shape
    latent = cache is not None
    in_specs = [
        pl.BlockSpec(memory_space=pltpu.SMEM),
        pl.BlockSpec((1, n_tok, A_Q_W), lambda b: (b, 0, Z_AQ // A_Q_W)),
        pl.BlockSpec((1, n_tok, A_KV_W), lambda b: (b, 0, 0)),
        pl.BlockSpec((1, n_tok, A_KV_W), lambda b: (b, 0, 1)),
        pl.BlockSpec((1, A_KV_W), lambda b: (0, 0)),
        pl.BlockSpec((1, A_KV_W), lambda b: (0, 0)),
    ]
    args = [sink, z3, kv3, kv3, q_norm_w, k_norm_w]
    out_specs = [pl.BlockSpec((1, n_tok, A_Q_W), lambda b: (b, 0, 0))]
    out_shape = [jax.ShapeDtypeStruct((bsz, n_tok, A_Q_W), BF16)]
    scratch = []
    if latent:
        assert n_tok // WINDOW >= 3
        n_ctx = cache[0].shape[1]
        in_specs += [
            pl.BlockSpec((n_tok, A_KV_W), lambda b: (0, 0)),
            pl.BlockSpec((n_tok, A_KV_W), lambda b: (0, 0)),
            pl.BlockSpec((1, n_ctx, A_KV_W), lambda b: (b, 0, 0)),
            pl.BlockSpec((1, n_ctx, A_KV_W), lambda b: (b, 0, 0)),
        ]
        args += [rope_tabs[0], rope_tabs[1], cache[0], cache[1]]
        scratch = [
            pltpu.VMEM((A_GROUP, n_tok, A_HD), BF16),
            pltpu.VMEM((n_tok, A_HD), BF16),
            pltpu.VMEM((n_tok, 2 * A_HD), BF16),
        ]
    else:
        out_specs += [pl.BlockSpec((1, n_tok, A_KV_W), lambda b: (b, 0, 0))] * 2
        out_shape += [jax.ShapeDtypeStruct((bsz, n_tok, A_KV_W), F32)] * 2
    return pl.pallas_call(
        functools.partial(_attn_kernel, n_tok=n_tok, latent=latent),
        grid=(bsz,),
        in_specs=in_specs,
        out_specs=out_specs,
        out_shape=out_shape,
        scratch_shapes=scratch,
        compiler_params=_cparams("parallel"),
        name="attention",
    )(*args)


def _merge_kernel(x_ref, mod_ref, hm_ref, ha_ref, gm_ref, ga_ref, wpm_ref, wpa_ref, wo_ref, o_ref):
    pm = jnp.dot(hm_ref[...], wpm_ref[...], preferred_element_type=F32)
    pa = jnp.dot(ha_ref[...], wpa_ref[...], preferred_element_type=F32)
    u = _sigmoid(gm_ref[...].astype(F32)) * pm + _sigmoid(ga_ref[...].astype(F32)) * pa
    mix = jnp.dot(u.astype(BF16), wo_ref[...], preferred_element_type=F32)
    o_ref[...] = x_ref[...] + mod_ref[0, 5:6, :] * mix


def _merge(x, mod3, row_of_tile, hm, ha, z, w_proj_m, w_proj_a, w_out, tm):
    n_tok, d = x.shape
    resident = functools.partial(pl.BlockSpec, pipeline_mode=pl.Buffered(1))
    return pl.pallas_call(
        _merge_kernel,
        grid=(n_tok // tm,),
        in_specs=[
            pl.BlockSpec((tm, d), lambda i: (i, 0)),
            pl.BlockSpec((1, N_MOD, d), lambda i: (row_of_tile(i, tm), 0, 0)),
            pl.BlockSpec((tm, M_HEADS * M_V), lambda i: (i, 0)),
            pl.BlockSpec((tm, A_Q_W), lambda i: (i, 0)),
            pl.BlockSpec((tm, d), lambda i: (i, Z_GM // D_MODEL)),
            pl.BlockSpec((tm, d), lambda i: (i, Z_GA // D_MODEL)),
            resident((M_HEADS * M_V, d), lambda i: (0, 0)),
            resident((A_Q_W, d), lambda i: (0, 0)),
            resident((d, d), lambda i: (0, 0)),
        ],
        out_specs=pl.BlockSpec((tm, d), lambda i: (i, 0)),
        out_shape=jax.ShapeDtypeStruct((n_tok, d), F32),
        compiler_params=_cparams("parallel"),
        name="merge",
    )(x, mod3, hm, ha, z, z, w_proj_m, w_proj_a, w_out)


def _rope_tables(n_tok):
    nf = A_HD // 4
    inv = ROPE_THETA ** (-jnp.arange(nf, dtype=F32) / nf)
    tok = jnp.arange(n_tok)
    pos = jnp.stack([tok // GRID_W, tok % GRID_W], axis=1).astype(F32)
    ang = pos[:, :, None] * inv
    cos = jnp.cos(ang)
    sin = jnp.sin(ang)
    cos_h = jnp.concatenate([cos, cos], axis=-1).reshape(n_tok, A_HD)
    sin_h = jnp.concatenate([-sin, sin], axis=-1).reshape(n_tok, A_HD)
    return jnp.tile(cos_h, (1, A_KV)), jnp.tile(sin_h, (1, A_KV))


def _trunk(x, mod3, row_of_tile, wts, states, rope_tabs, cache, tm_ffn, tm_proj, tm_merge):
    bsz, n_tok, d = x.shape
    x2 = x.reshape(bsz * n_tok, d)
    x2 = _ffn(x2, mod3, row_of_tile, wts["norm1"], wts["wi1"], wts["wo1"], 0, tm_ffn)
    z, gates, kv = _mixer_proj(x2, mod3, row_of_tile, wts["norm_mix"], wts["w_main"], wts["w_aux"], tm_proj)
    z3 = z.reshape(bsz, n_tok, Z_COLS)
    g = jnp.transpose(gates[:, :M_GATES].reshape(bsz, n_tok, 2, 2, M_HEADS), (0, 2, 3, 4, 1))
    gate_i = g[:, :, 0].reshape(bsz, GATE_ROWS, n_tok)
    gate_f = g[:, :, 1].reshape(bsz, GATE_ROWS, n_tok)
    m_out = _mlstm(z3, gate_i, gate_f, wts["bias_i"], wts["bias_f"], wts["mlstm_norm"], states)
    a_out = _attention(z3, kv.reshape(bsz, n_tok, 2 * A_KV_W), wts["sink"], wts["q_norm"], wts["k_norm"],
                       rope_tabs, cache)
    hm = m_out[0].reshape(bsz * n_tok, M_HEADS * M_V)
    ha = a_out[0].reshape(bsz * n_tok, A_Q_W)
    x2 = _merge(x2, mod3, row_of_tile, hm, ha, z, wts["w_proj_m"], wts["w_proj_a"], wts["w_out"], tm_merge)
    x2 = _ffn(x2, mod3, row_of_tile, wts["norm2"], wts["wi2"], wts["wo2"], 6, tm_ffn)
    return x2.reshape(bsz, n_tok, d), m_out[1:], a_out[1:]


def kernel(x_prompt, x_sample, cache_attn_k, cache_attn_v, state_mlstm_C, state_mlstm_n, state_mlstm_m, c, c_ctx, ada_w, ada_b, norm_ffn1_w, ffn1_wi, ffn1_wo, norm_mix_w, w_in, mlstm_gate_b, mlstm_norm_w, attn_q_norm_w, attn_k_norm_w, attn_sink, w_proj_m, w_proj_a, w_out, norm_ffn2_w, ffn2_wi, ffn2_wo):
    bp, tp, d = x_prompt.shape
    bs, ts, _ = x_sample.shape
    n_ctx = cache_attn_k.shape[2]
    l = 0

    n_rows = 16
    cond = jnp.concatenate([c_ctx[None, :], c, jnp.zeros((n_rows - 1 - bs, d), F32)], axis=0)
    mod3 = _modulation(cond, ada_w[l], ada_b[l][None, :]).reshape(n_rows, N_MOD, d)

    w = w_in[l]
    offs = {}
    acc = 0
    for name, size in (("mq", 512), ("mk", 512), ("mv", 1024), ("mo", 1024), ("mg", M_GATES), ("aq", 1024),
                       ("ak", 256), ("av", 256), ("gm", 2048), ("ga", 2048)):
        offs[name] = (acc, acc + size)
        acc += size
    seg = lambda nm: w[:, offs[nm][0]:offs[nm][1]]
    w_main = jnp.concatenate([seg(nm) for nm in ("gm", "ga", "mv", "mo", "aq", "mq", "mk")], axis=1).astype(BF16)
    w_aux = jnp.concatenate([jnp.pad(seg("mg"), ((0, 0), (0, GATE_LANES - M_GATES))), seg("ak"), seg("av")],
                            axis=1).astype(BF16)
    gate_b = mlstm_gate_b[l].reshape(2, 2, M_HEADS)
    wts = dict(
        norm1=norm_ffn1_w[l][None, :], wi1=ffn1_wi[l].astype(BF16), wo1=ffn1_wo[l].astype(BF16),
        norm_mix=norm_mix_w[l][None, :], w_main=w_main, w_aux=w_aux,
        bias_i=gate_b[:, 0].reshape(GATE_ROWS, 1), bias_f=gate_b[:, 1].reshape(GATE_ROWS, 1),
        mlstm_norm=mlstm_norm_w[l][None, :],
        sink=attn_sink[l], q_norm=jnp.tile(attn_q_norm_w[l], A_KV)[None, :],
        k_norm=jnp.tile(attn_k_norm_w[l], A_KV)[None, :],
        w_proj_m=w_proj_m[l].astype(BF16), w_proj_a=w_proj_a[l].astype(BF16), w_out=w_out[l].astype(BF16),
        norm2=norm_ffn2_w[l][None, :], wi2=ffn2_wi[l].astype(BF16), wo2=ffn2_wo[l].astype(BF16),
    )

    y_prompt, (c_new, n_new, m_new), (k_new, v_new) = _trunk(
        x_prompt, mod3, lambda i, tm: 0, wts, None, None, None, tm_ffn=1024, tm_proj=1024, tm_merge=256)
    new_attn_k = k_new.reshape(bp, 1, tp, A_KV, A_HD)
    new_attn_v = v_new.reshape(bp, 1, tp, A_KV, A_HD)
    new_c = c_new.reshape(bp, 1, 2, M_HEADS, M_V, M_QK)
    new_n = n_new.reshape(bp, 1, 2, M_HEADS, M_QK)
    new_m = m_new.reshape(bp, 1, 2, M_HEADS)

    states = (state_mlstm_C[:, l].reshape(bs, GATE_ROWS, M_V, M_QK),
              state_mlstm_n[:, l].reshape(bs, GATE_ROWS, M_QK),
              state_mlstm_m[:, l].reshape(bs, GATE_ROWS, 1))
    cache = (cache_attn_k[:, l].reshape(bs, n_ctx, A_KV_W), cache_attn_v[:, l].reshape(bs, n_ctx, A_KV_W))
    y_sample, _, _ = _trunk(
        x_sample, mod3, lambda i, tm: 1 + (i * tm) // ts, wts, states, _rope_tables(ts), cache,
        tm_ffn=1024, tm_proj=1024, tm_merge=256)

    return (y_prompt, y_sample, new_attn_k, new_attn_v, new_c, new_n, new_m)
```

```python
import functools

import jax
import jax.numpy as jnp
from jax import lax
from jax.experimental import pallas as pl
from jax.experimental.pallas import tpu as pltpu

F32 = jnp.float32
BF16 = jnp.bfloat16
LOG2E = 1.4426950408889634

D_MODEL = 2048
D_FF = 5632
N_MOD = 9
NORM_EPS = 1e-6
GRID_W = 64
ROPE_THETA = 10000.0
M_HEADS = 4
M_QK = 128
M_V = 256
M_GATES = 4 * M_HEADS
A_HEADS = 16
A_KV = 4
A_GROUP = A_HEADS // A_KV
A_HD = 64
A_Q_W = A_HEADS * A_HD
A_KV_W = A_KV * A_HD
WINDOW = 128

Z_GM, Z_GA, Z_MV, Z_MO, Z_AQ, Z_MQ, Z_MK = 0, 2048, 4096, 5120, 6144, 7168, 7680
Z_COLS = 8192
GATE_LANES = 128
AUX_COLS = GATE_LANES + 2 * A_KV_W
GATE_ROWS = 2 * M_HEADS

FFN_TF = 512
PROJ_TN = 1024
MLSTM_BLOCK = 256
XPOSE_ROWS = 128

VMEM_LIMIT = 56 * 1024 * 1024
NT_DIMS = (((1,), (1,)), ((), ()))
TN_DIMS = (((0,), (0,)), ((), ()))


def _cparams(*sem):
    return pltpu.CompilerParams(dimension_semantics=sem, vmem_limit_bytes=VMEM_LIMIT)


def _sigmoid(x):
    return 1.0 / (1.0 + jnp.exp(-x))


def _norm_modulate(x, norm_w, shift, scale):
    ms = jnp.mean(x * x, axis=-1, keepdims=True)
    y = x * lax.rsqrt(ms + NORM_EPS) * norm_w
    return y * (1.0 + scale) + shift


def _mod_kernel(c_ref, w_ref, b_ref, o_ref):
    c = c_ref[...]
    s = (c * _sigmoid(c)).astype(BF16)
    o_ref[...] = jnp.dot(s, w_ref[...].astype(BF16), preferred_element_type=F32) + b_ref[...]


def _modulation(cond, ada_w, ada_b, tn=1024):
    rows, d = cond.shape
    n = ada_w.shape[1]
    return pl.pallas_call(
        _mod_kernel,
        grid=(n // tn,),
        in_specs=[
            pl.BlockSpec((rows, d), lambda j: (0, 0)),
            pl.BlockSpec((d, tn), lambda j: (0, j)),
            pl.BlockSpec((1, tn), lambda j: (0, j)),
        ],
        out_specs=pl.BlockSpec((rows, tn), lambda j: (0, j)),
        out_shape=jax.ShapeDtypeStruct((rows, n), F32),
        compiler_params=_cparams("arbitrary"),
        name="modulation",
    )(cond, ada_w, ada_b)


FFN_NORM_ROWS = 64


def _ffn_kernel(x_ref, xn_ref, mod_ref, modn_ref, nw_ref, wg_ref, wu_ref, wo_ref, o_ref, h0_ref, h1_ref, *,
                k_shift):
    i = pl.program_id(0)
    j = pl.program_id(1)

    def norm(x, m_ref):
        return _norm_modulate(x, nw_ref[...], m_ref[0, k_shift:k_shift + 1, :],
                              m_ref[0, k_shift + 1:k_shift + 2, :]).astype(BF16)

    @pl.when(jnp.logical_and(i == 0, j == 0))
    def _():
        h0_ref[...] = norm(x_ref[...], mod_ref)

    @pl.when(j == 0)
    def _():
        o_ref[...] = jnp.zeros_like(o_ref)

    def step(h_ref, hn_ref):
        chunk = jnp.minimum(j, x_ref.shape[0] // FFN_NORM_ROWS - 1)
        rows = pl.ds(pl.multiple_of(chunk * FFN_NORM_ROWS, FFN_NORM_ROWS), FFN_NORM_ROWS)
        hn_ref[rows, :] = norm(xn_ref[rows, :], modn_ref)
        h = h_ref[...]
        g = jnp.dot(h, wg_ref[0], preferred_element_type=F32)
        u = jnp.dot(h, wu_ref[0], preferred_element_type=F32)
        a = (g * _sigmoid(g) * u).astype(BF16)
        o_ref[...] += jnp.dot(a, wo_ref[...], preferred_element_type=F32)

    pl.when(i % 2 == 0)(lambda: step(h0_ref, h1_ref))
    pl.when(i % 2 == 1)(lambda: step(h1_ref, h0_ref))

    @pl.when(j == pl.num_programs(1) - 1)
    def _():
        gate = mod_ref[0, k_shift + 2:k_shift + 3, :]
        o_ref[...] = x_ref[...] + 0.5 * gate * o_ref[...]


def _ffn(x, mod3, row_of_tile, norm_w, wi, wo, k_shift, tm):
    n_tok, d = x.shape
    tf = FFN_TF
    nf = D_FF // tf
    nt = n_tok // tm
    assert nf >= tm // FFN_NORM_ROWS
    nxt = lambda i: jnp.minimum(i + 1, nt - 1)
    return pl.pallas_call(
        functools.partial(_ffn_kernel, k_shift=k_shift),
        grid=(nt, nf),
        in_specs=[
            pl.BlockSpec((tm, d), lambda i, j: (i, 0)),
            pl.BlockSpec((tm, d), lambda i, j: (nxt(i), 0)),
            pl.BlockSpec((1, N_MOD, d), lambda i, j: (row_of_tile(i, tm), 0, 0)),
            pl.BlockSpec((1, N_MOD, d), lambda i, j: (row_of_tile(nxt(i), tm), 0, 0)),
            pl.BlockSpec((1, d), lambda i, j: (0, 0)),
            pl.BlockSpec((1, d, tf), lambda i, j: (j, 0, 0)),
            pl.BlockSpec((1, d, tf), lambda i, j: (j + nf, 0, 0)),
            pl.BlockSpec((tf, d), lambda i, j: (j, 0)),
        ],
        out_specs=pl.BlockSpec((tm, d), lambda i, j: (i, 0)),
        out_shape=jax.ShapeDtypeStruct((n_tok, d), F32),
        scratch_shapes=[pltpu.VMEM((tm, d), BF16), pltpu.VMEM((tm, d), BF16)],
        compiler_params=_cparams("arbitrary", "arbitrary"),
        name="ffn",
    )(x, x, mod3, mod3, norm_w, wi, wi, wo)


def _proj_kernel(x_ref, mod_ref, nw_ref, w_ref, waux_ref, z_ref, g_ref, kv_ref, h_ref):
    j = pl.program_id(1)

    @pl.when(j == 0)
    def _():
        h = _norm_modulate(x_ref[...], nw_ref[...], mod_ref[0, 3:4, :], mod_ref[0, 4:5, :]).astype(BF16)
        h_ref[...] = h
        aux = jnp.dot(h, waux_ref[...], preferred_element_type=F32)
        g_ref[...] = aux[:, :GATE_LANES]
        kv_ref[...] = aux[:, GATE_LANES:].astype(BF16)

    half = z_ref.shape[1] // 2
    h = h_ref[...]
    z_ref[:, :half] = jnp.dot(h, w_ref[0, :, :half], preferred_element_type=F32).astype(BF16)
    z_ref[:, half:] = jnp.dot(h, w_ref[0, :, half:], preferred_element_type=F32).astype(BF16)


def _mixer_proj(x, mod3, row_of_tile, norm_w, w_main, w_aux, tm):
    n_tok, d = x.shape
    tn = PROJ_TN
    once = functools.partial(pl.BlockSpec, pipeline_mode=pl.Buffered(1))
    return pl.pallas_call(
        _proj_kernel,
        grid=(n_tok // tm, Z_COLS // tn),
        in_specs=[
            pl.BlockSpec((tm, d), lambda i, j: (i, 0)),
            pl.BlockSpec((1, N_MOD, d), lambda i, j: (row_of_tile(i, tm), 0, 0)),
            pl.BlockSpec((1, d), lambda i, j: (0, 0)),
            pl.BlockSpec((1, d, tn), lambda i, j: (j, 0, 0)),
            once((d, AUX_COLS), lambda i, j: (0, 0)),
        ],
        out_specs=[
            pl.BlockSpec((tm, tn), lambda i, j: (i, j)),
            pl.BlockSpec((tm, GATE_LANES), lambda i, j: (i, 0)),
            pl.BlockSpec((tm, 2 * A_KV_W), lambda i, j: (i, 0)),
        ],
        out_shape=[
            jax.ShapeDtypeStruct((n_tok, Z_COLS), BF16),
            jax.ShapeDtypeStruct((n_tok, GATE_LANES), F32),
            jax.ShapeDtypeStruct((n_tok, 2 * A_KV_W), BF16),
        ],
        scratch_shapes=[pltpu.VMEM((tm, d), BF16)],
        compiler_params=_cparams("parallel", "arbitrary"),
        name="mixer_proj",
    )(x, mod3, norm_w, w_main, w_aux)


def _scan_lanes(x, op, fill, reverse):
    n = x.shape[-1]
    lane = lax.broadcasted_iota(jnp.int32, x.shape, 1)
    s = 1
    while s < n:
        if reverse:
            x = op(x, jnp.where(lane < n - s, pltpu.roll(x, n - s, 1), fill))
        else:
            x = op(x, jnp.where(lane >= s, pltpu.roll(x, s, 1), fill))
        s *= 2
    return x


def _dir_scan(x, op, fill, fwd_rows):
    return jnp.where(fwd_rows, _scan_lanes(x, op, fill, False), _scan_lanes(x, op, fill, True))


def _log_sigmoid(x):
    return jnp.minimum(x, 0.0) - jnp.log1p(jnp.exp(-jnp.abs(x)))


def _mlstm_kernel(*refs, n_tok, has_state):
    if has_state:
        (q_ref, k_ref, v_ref, og_ref, gi_ref, gf_ref, bi_ref, bf_ref, nw_ref, c0_ref, n0_ref, m0_ref,
         o_ref) = refs
    else:
        (q_ref, k_ref, v_ref, og_ref, gi_ref, gf_ref, bi_ref, bf_ref, nw_ref,
         o_ref, c_out, n_out, m_out) = refs
    blk = min(MLSTM_BLOCK, n_tok)
    nq = n_tok // blk

    fwd_rows = lax.broadcasted_iota(jnp.int32, (GATE_ROWS, 1), 0) < M_HEADS
    ig = gi_ref[0] + bi_ref[...]
    cum_f = _dir_scan(_log_sigmoid(gf_ref[0] + bf_ref[...]), jnp.add, 0.0, fwd_rows)
    a = ig - cum_f
    m0 = m0_ref[0] if has_state else jnp.zeros((GATE_ROWS, 1), F32)
    mx = jnp.maximum(_dir_scan(a, jnp.maximum, -jnp.inf, fwd_rows), m0)
    stats = [-mx, cum_f + mx]
    if not has_state:
        mx_end = jnp.where(fwd_rows, mx[:, n_tok - 1:n_tok], mx[:, 0:1])
        cum_end = jnp.where(fwd_rows, cum_f[:, n_tok - 1:n_tok], cum_f[:, 0:1])
        stats.append(jnp.exp(a - mx_end))
        m_out[0] = cum_end + mx_end
    pad = jnp.zeros((XPOSE_ROWS - GATE_ROWS * len(stats), n_tok), F32)
    cols = jnp.concatenate(stats + [pad], axis=0).T

    row_i = lax.broadcasted_iota(jnp.int32, (blk, blk), 0)
    col_i = lax.broadcasted_iota(jnp.int32, (blk, blk), 1)
    visible = (col_i <= row_i, col_i >= row_i)

    for h in range(M_HEADS):
        qk_cols = slice(h * M_QK, (h + 1) * M_QK)
        v_cols = slice(h * M_V, (h + 1) * M_V)
        qf = q_ref[0, :, qk_cols].astype(F32) * (M_QK ** -0.5)
        q = qf.astype(BF16)
        k = k_ref[0, :, qk_cols]
        v = v_ref[0, :, v_cols]
        for qi in range(nq):
            rows = slice(qi * blk, (qi + 1) * blk)
            h_sum = None
            for d in range(2):
                r = M_HEADS * d + h
                lo, hi = (0, (qi + 1) * blk) if d == 0 else (qi * blk, n_tok)
                dg = hi - lo - blk if d == 0 else 0
                u_col = cols[rows, r:r + 1]
                mt_col = cols[rows, GATE_ROWS + r:GATE_ROWS + r + 1]
                dm = u_col + a[r:r + 1, lo:hi]
                parts = [dm[:, :dg]] if dg > 0 else []
                parts.append(jnp.where(visible[d], dm[:, dg:dg + blk], -jnp.inf))
                if dg + blk < hi - lo:
                    parts.append(dm[:, dg + blk:])
                dm = jnp.concatenate(parts, axis=1) if len(parts) > 1 else parts[0]
                s = lax.dot_general(q[rows], k[lo:hi], NT_DIMS, preferred_element_type=F32) * jnp.exp(dm)
                num = jnp.dot(s.astype(BF16), v[lo:hi], preferred_element_type=F32)
                den = jnp.sum(s, axis=1, keepdims=True)
                if has_state:
                    decay = jnp.exp(u_col + m0[r:r + 1, :])
                    num = num + decay * lax.dot_general(q[rows], c0_ref[0, r].astype(BF16), NT_DIMS,
                                                        preferred_element_type=F32)
                    den = den + decay * jnp.sum(qf[rows] * n0_ref[0, r:r + 1, :], axis=1, keepdims=True)
                h_dir = num / jnp.maximum(jnp.abs(den), jnp.exp(-mt_col))
                h_sum = h_dir if h_sum is None else h_sum + h_dir
            hn = h_sum * lax.rsqrt(jnp.mean(h_sum * h_sum, axis=-1, keepdims=True) + NORM_EPS) * nw_ref[:, v_cols]
            o_ref[0, rows, v_cols] = (hn * _sigmoid(og_ref[0, rows, v_cols].astype(F32))).astype(BF16)
        if not has_state:
            kf = k.astype(F32)
            vf = v.astype(F32)
            for d in range(2):
                r = M_HEADS * d + h
                wk_col = cols[:, 2 * GATE_ROWS + r:2 * GATE_ROWS + r + 1]
                c_out[0, r] = lax.dot_general((vf * wk_col).astype(BF16), k, TN_DIMS, preferred_element_type=F32)
                n_out[0, r:r + 1, :] = jnp.sum(kf * wk_col, axis=0, keepdims=True)


def _mlstm(z3, gate_i, gate_f, bias_i, bias_f, norm_w, states):
    bsz, n_tok, _ = z3.shape
    has_state = states is not None
    qk_w, v_w = M_HEADS * M_QK, M_HEADS * M_V
    in_specs = [
        pl.BlockSpec((1, n_tok, qk_w), lambda b: (b, 0, Z_MQ // qk_w)),
        pl.BlockSpec((1, n_tok, qk_w), lambda b: (b, 0, Z_MK // qk_w)),
        pl.BlockSpec((1, n_tok, v_w), lambda b: (b, 0, Z_MV // v_w)),
        pl.BlockSpec((1, n_tok, v_w), lambda b: (b, 0, Z_MO // v_w)),
        pl.BlockSpec((1, GATE_ROWS, n_tok), lambda b: (b, 0, 0)),
        pl.BlockSpec((1, GATE_ROWS, n_tok), lambda b: (b, 0, 0)),
        pl.BlockSpec((GATE_ROWS, 1), lambda b: (0, 0)),
        pl.BlockSpec((GATE_ROWS, 1), lambda b: (0, 0)),
        pl.BlockSpec((1, v_w), lambda b: (0, 0)),
    ]
    state_specs = [
        pl.BlockSpec((1, GATE_ROWS, M_V, M_QK), lambda b: (b, 0, 0, 0)),
        pl.BlockSpec((1, GATE_ROWS, M_QK), lambda b: (b, 0, 0)),
        pl.BlockSpec((1, GATE_ROWS, 1), lambda b: (b, 0, 0)),
    ]
    out_specs = [pl.BlockSpec((1, n_tok, v_w), lambda b: (b, 0, 0))]
    out_shape = [jax.ShapeDtypeStruct((bsz, n_tok, v_w), BF16)]
    args = [z3, z3, z3, z3, gate_i, gate_f, bias_i, bias_f, norm_w]
    if has_state:
        in_specs += state_specs
        args += list(states)
    else:
        out_specs += state_specs
        out_shape += [
            jax.ShapeDtypeStruct((bsz, GATE_ROWS, M_V, M_QK), F32),
            jax.ShapeDtypeStruct((bsz, GATE_ROWS, M_QK), F32),
            jax.ShapeDtypeStruct((bsz, GATE_ROWS, 1), F32),
        ]
    return pl.pallas_call(
        functools.partial(_mlstm_kernel, n_tok=n_tok, has_state=has_state),
        grid=(bsz,),
        in_specs=in_specs,
        out_specs=out_specs,
        out_shape=out_shape,
        compiler_params=_cparams("parallel"),
        name="mlstm",
    )(*args)


def _group_rms(x, w_row, ones_bd):
    x2 = x * x
    hi = x2.astype(BF16)
    lo = (x2 - hi.astype(F32)).astype(BF16)
    ss = jnp.dot(hi, ones_bd, preferred_element_type=F32) + jnp.dot(lo, ones_bd, preferred_element_type=F32)
    return x * lax.rsqrt(ss * (1.0 / A_HD) + NORM_EPS) * w_row


def _rope(x, cos, sin_signed, lane_lo):
    n = x.shape[-1]
    partner = jnp.where(lane_lo, pltpu.roll(x, n - 16, 1), pltpu.roll(x, 16, 1))
    return x * cos + partner * sin_signed


def _attend_block(q, sink_col, k_loc, v_loc, kinds, masks, k_ctx=None, v_ctx=None):
    s_loc = lax.dot_general(q, k_loc, NT_DIMS, preferred_element_type=F32)
    tiles = []
    for j, kind in enumerate(kinds):
        t = s_loc[:, j * WINDOW:(j + 1) * WINDOW]
        tiles.append(t if kind == "cur" else jnp.where(masks[kind], t, -jnp.inf))
    n_loc = len(tiles)
    if k_ctx is not None:
        s_ctx = lax.dot_general(q, k_ctx, NT_DIMS, preferred_element_type=F32)
        tiles += [s_ctx[:, j * WINDOW:(j + 1) * WINDOW] for j in range(k_ctx.shape[0] // WINDOW)]
    tile_max = tiles[0]
    for t in tiles[1:]:
        tile_max = jnp.maximum(tile_max, t)
    m = jnp.maximum(jnp.max(tile_max, axis=1, keepdims=True), sink_col)
    p = [jnp.exp2(t - m).astype(BF16) for t in tiles]
    acc = jnp.dot(jnp.concatenate(p[:n_loc], axis=1), v_loc, preferred_element_type=F32)
    if k_ctx is not None:
        acc = acc + jnp.dot(jnp.concatenate(p[n_loc:], axis=1), v_ctx, preferred_element_type=F32)
    den = acc[:, A_HD:A_HD + 1] + jnp.exp2(sink_col - m)
    return acc[:, :A_HD] / den


def _attn_kernel(*refs, n_tok, latent):
    if latent:
        (sink_ref, q_ref, k_ref, v_ref, qw_ref, kw_ref, cos_ref, sin_ref, ck_ref, cv_ref,
         o_ref, qh_ref, kg_ref, va_ref) = refs
    else:
        (sink_ref, q_ref, k_ref, v_ref, qw_ref, kw_ref,
         o_ref, ko_ref, vo_ref) = refs

    lane = lax.broadcasted_iota(jnp.int32, (1, A_KV_W), 1)
    lane_lo = (lane % 32) < 16
    bd_r = lax.broadcasted_iota(jnp.int32, (A_KV_W, A_KV_W), 0) // A_HD
    bd_c = lax.broadcasted_iota(jnp.int32, (A_KV_W, A_KV_W), 1) // A_HD
    ones_bd = jnp.where(bd_r == bd_c, 1.0, 0.0).astype(BF16)

    k_all = _group_rms(k_ref[0].astype(F32), kw_ref[...], ones_bd)
    v_all = v_ref[0]
    if latent:
        k_all = _rope(k_all, cos_ref[...], sin_ref[...], lane_lo)
    else:
        ko_ref[0] = k_all
        vo_ref[0] = v_all.astype(F32)

    q_blk = WINDOW if latent else n_tok
    m_rows = A_GROUP * q_blk
    head_of_row = lax.broadcasted_iota(jnp.int32, (m_rows, 1), 0) // q_blk
    r_in = lax.broadcasted_iota(jnp.int32, (m_rows, WINDOW), 0) % WINDOW
    c_in = lax.broadcasted_iota(jnp.int32, (m_rows, WINDOW), 1)
    masks = {"prev": c_in >= r_in, "next": c_in <= r_in}
    ones_v = jnp.ones((n_tok, A_HD), BF16)

    def sink_column(g):
        col = jnp.zeros((m_rows, 1), F32)
        for a in range(A_GROUP):
            col = jnp.where(head_of_row == a, sink_ref[A_GROUP * g + a] * LOG2E, col)
        return col

    def unstack(o):
        return jnp.concatenate([o[a * q_blk:(a + 1) * q_blk, :] for a in range(A_GROUP)], axis=1).astype(BF16)

    for g in range(A_KV):
        g_cols = slice(A_KV_W * g, A_KV_W * (g + 1))
        h_cols = slice(A_HD * g, A_HD * (g + 1))
        qg = _group_rms(q_ref[0, :, g_cols].astype(F32), qw_ref[...], ones_bd)
        if latent:
            qg = _rope(qg, cos_ref[...], sin_ref[...], lane_lo)
        qg = qg * (A_HD ** -0.5 * LOG2E)
        kg = k_all[:, h_cols].astype(BF16)
        va = jnp.concatenate([v_all[:, h_cols], ones_v], axis=1)
        sink_col = sink_column(g)
        if not latent:
            q_stack = jnp.concatenate([qg[:, A_HD * a:A_HD * (a + 1)] for a in range(A_GROUP)], axis=0)
            o = _attend_block(q_stack.astype(BF16), sink_col, kg, va, ["cur"] * (n_tok // WINDOW), masks)
            o_ref[0, :, g_cols] = unstack(o)
            continue

        for a in range(A_GROUP):
            qh_ref[a] = qg[:, A_HD * a:A_HD * (a + 1)].astype(BF16)
        kg_ref[...] = kg
        va_ref[...] = va
        n_ctx = ck_ref.shape[1]
        ckg = ck_ref[0][:, h_cols].astype(BF16)
        cva = jnp.concatenate([cv_ref[0][:, h_cols].astype(BF16), jnp.ones((n_ctx, A_HD), BF16)], axis=1)
        nb = n_tok // WINDOW

        def q_block(i, kinds, k_start, g_cols=g_cols, sink_col=sink_col, ckg=ckg, cva=cva, qg=qg, kg=kg, va=va):
            n_keys = len(kinds) * WINDOW
            if isinstance(i, int):
                r0 = i * WINDOW
                q_stack = jnp.concatenate([qg[r0:r0 + WINDOW, A_HD * a:A_HD * (a + 1)] for a in range(A_GROUP)],
                                          axis=0).astype(BF16)
                k_loc = kg[k_start:k_start + n_keys]
                v_loc = va[k_start:k_start + n_keys]
            else:
                r0 = pl.multiple_of(i * WINDOW, WINDOW)
                k_start = pl.multiple_of(k_start, WINDOW)
                q_stack = jnp.concatenate([qh_ref[a, pl.ds(r0, WINDOW), :] for a in range(A_GROUP)], axis=0)
                k_loc = kg_ref[pl.ds(k_start, n_keys), :]
                v_loc = va_ref[pl.ds(k_start, n_keys), :]
            o = _attend_block(q_stack, sink_col, k_loc, v_loc, kinds, masks, ckg, cva)
            o_ref[0, pl.ds(r0, WINDOW), g_cols] = unstack(o)

        def interior(i, carry, q_block=q_block):
            q_block(i, ["prev", "cur", "next"], (i - 1) * WINDOW)
            return carry

        lax.fori_loop(1, nb - 1, interior, 0, unroll=3)
        q_block(0, ["cur", "next"], 0)
        q_block(nb - 1, ["prev", "cur"], (nb - 2) * WINDOW)


def _attention(z3, kv3, sink, q_norm_w, k_norm_w, rope_tabs, cache):
    bsz, n_tok, _ = z3.shape
    latent = cache is not None
    in_specs = [
        pl.BlockSpec(memory_space=pltpu.SMEM),
        pl.BlockSpec((1, n_tok, A_Q_W), lambda b: (b, 0, Z_AQ // A_Q_W)),
        pl.BlockSpec((1, n_tok, A_KV_W), lambda b: (b, 0, 0)),
        pl.BlockSpec((1, n_tok, A_KV_W), lambda b: (b, 0, 1)),
        pl.BlockSpec((1, A_KV_W), lambda b: (0, 0)),
        pl.BlockSpec((1, A_KV_W), lambda b: (0, 0)),
    ]
    args = [sink, z3, kv3, kv3, q_norm_w, k_norm_w]
    out_specs = [pl.BlockSpec((1, n_tok, A_Q_W), lambda b: (b, 0, 0))]
    out_shape = [jax.ShapeDtypeStruct((bsz, n_tok, A_Q_W), BF16)]
    scratch = []
    if latent:
        assert n_tok // WINDOW >= 3
        n_ctx = cache[0].shape[1]
        in_specs += [
            pl.BlockSpec((n_tok, A_KV_W), lambda b: (0, 0)),
            pl.BlockSpec((n_tok, A_KV_W), lambda b: (0, 0)),
            pl.BlockSpec((1, n_ctx, A_KV_W), lambda b: (b, 0, 0)),
            pl.BlockSpec((1, n_ctx, A_KV_W), lambda b: (b, 0, 0)),
        ]
        args += [rope_tabs[0], rope_tabs[1], cache[0], cache[1]]
        scratch = [
            pltpu.VMEM((A_GROUP, n_tok, A_HD), BF16),
            pltpu.VMEM((n_tok, A_HD), BF16),
            pltpu.VMEM((n_tok, 2 * A_HD), BF16),
        ]
    else:
        out_specs += [pl.BlockSpec((1, n_tok, A_KV_W), lambda b: (b, 0, 0))] * 2
        out_shape += [jax.ShapeDtypeStruct((bsz, n_tok, A_KV_W), F32)] * 2
    return pl.pallas_call(
        functools.partial(_attn_kernel, n_tok=n_tok, latent=latent),
        grid=(bsz,),
        in_specs=in_specs,
        out_specs=out_specs,
        out_shape=out_shape,
        scratch_shapes=scratch,
        compiler_params=_cparams("parallel"),
        name="attention",
    )(*args)


def _merge_kernel(x_ref, mod_ref, hm_ref, ha_ref, gm_ref, ga_ref, wpm_ref, wpa_ref, wo_ref, o_ref):
    pm = jnp.dot(hm_ref[...], wpm_ref[...], preferred_element_type=F32)
    pa = jnp.dot(ha_ref[...], wpa_ref[...], preferred_element_type=F32)
    u = _sigmoid(gm_ref[...].astype(F32)) * pm + _sigmoid(ga_ref[...].astype(F32)) * pa
    mix = jnp.dot(u.astype(BF16), wo_ref[...], preferred_element_type=F32)
    o_ref[...] = x_ref[...] + mod_ref[0, 5:6, :] * mix


def _merge(x, mod3, row_of_tile, hm, ha, z, w_proj_m, w_proj_a, w_out, tm):
    n_tok, d = x.shape
    resident = functools.partial(pl.BlockSpec, pipeline_mode=pl.Buffered(1))
    return pl.pallas_call(
        _merge_kernel,
        grid=(n_tok // tm,),
        in_specs=[
            pl.BlockSpec((tm, d), lambda i: (i, 0)),
            pl.BlockSpec((1, N_MOD, d), lambda i: (row_of_tile(i, tm), 0, 0)),
            pl.BlockSpec((tm, M_HEADS * M_V), lambda i: (i, 0)),
            pl.BlockSpec((tm, A_Q_W), lambda i: (i, 0)),
            pl.BlockSpec((tm, d), lambda i: (i, Z_GM // D_MODEL)),
            pl.BlockSpec((tm, d), lambda i: (i, Z_GA // D_MODEL)),
            resident((M_HEADS * M_V, d), lambda i: (0, 0)),
            resident((A_Q_W, d), lambda i: (0, 0)),
            resident((d, d), lambda i: (0, 0)),
        ],
        out_specs=pl.BlockSpec((tm, d), lambda i: (i, 0)),
        out_shape=jax.ShapeDtypeStruct((n_tok, d), F32),
        compiler_params=_cparams("parallel"),
        name="merge",
    )(x, mod3, hm, ha, z, z, w_proj_m, w_proj_a, w_out)


def _rope_tables(n_tok):
    nf = A_HD // 4
    inv = ROPE_THETA ** (-jnp.arange(nf, dtype=F32) / nf)
    tok = jnp.arange(n_tok)
    pos = jnp.stack([tok // GRID_W, tok % GRID_W], axis=1).astype(F32)
    ang = pos[:, :, None] * inv
    cos = jnp.cos(ang)
    sin = jnp.sin(ang)
    cos_h = jnp.concatenate([cos, cos], axis=-1).reshape(n_tok, A_HD)
    sin_h = jnp.concatenate([-sin, sin], axis=-1).reshape(n_tok, A_HD)
    return jnp.tile(cos_h, (1, A_KV)), jnp.tile(sin_h, (1, A_KV))


def _trunk(x, mod3, row_of_tile, wts, states, rope_tabs, cache, tm_ffn, tm_proj, tm_merge):
    bsz, n_tok, d = x.shape
    x2 = x.reshape(bsz * n_tok, d)
    x2 = _ffn(x2, mod3, row_of_tile, wts["norm1"], wts["wi1"], wts["wo1"], 0, tm_ffn)
    z, gates, kv = _mixer_proj(x2, mod3, row_of_tile, wts["norm_mix"], wts["w_main"], wts["w_aux"], tm_proj)
    z3 = z.reshape(bsz, n_tok, Z_COLS)
    g = jnp.transpose(gates[:, :M_GATES].reshape(bsz, n_tok, 2, 2, M_HEADS), (0, 2, 3, 4, 1))
    gate_i = g[:, :, 0].reshape(bsz, GATE_ROWS, n_tok)
    gate_f = g[:, :, 1].reshape(bsz, GATE_ROWS, n_tok)
    m_out = _mlstm(z3, gate_i, gate_f, wts["bias_i"], wts["bias_f"], wts["mlstm_norm"], states)
    a_out = _attention(z3, kv.reshape(bsz, n_tok, 2 * A_KV_W), wts["sink"], wts["q_norm"], wts["k_norm"],
                       rope_tabs, cache)
    hm = m_out[0].reshape(bsz * n_tok, M_HEADS * M_V)
    ha = a_out[0].reshape(bsz * n_tok, A_Q_W)
    x2 = _merge(x2, mod3, row_of_tile, hm, ha, z, wts["w_proj_m"], wts["w_proj_a"], wts["w_out"], tm_merge)
    x2 = _ffn(x2, mod3, row_of_tile, wts["norm2"], wts["wi2"], wts["wo2"], 6, tm_ffn)
    return x2.reshape(bsz, n_tok, d), m_out[1:], a_out[1:]


def kernel(x_prompt, x_sample, cache_attn_k, cache_attn_v, state_mlstm_C, state_mlstm_n, state_mlstm_m, c, c_ctx, ada_w, ada_b, norm_ffn1_w, ffn1_wi, ffn1_wo, norm_mix_w, w_in, mlstm_gate_b, mlstm_norm_w, attn_q_norm_w, attn_k_norm_w, attn_sink, w_proj_m, w_proj_a, w_out, norm_ffn2_w, ffn2_wi, ffn2_wo):
    bp, tp, d = x_prompt.shape
    bs, ts, _ = x_sample.shape
    n_ctx = cache_attn_k.shape[2]
    l = 0

    n_rows = 16
    cond = jnp.concatenate([c_ctx[None, :], c, jnp.zeros((n_rows - 1 - bs, d), F32)], axis=0)
    mod3 = _modulation(cond, ada_w[l], ada_b[l][None, :]).reshape(n_rows, N_MOD, d)

    w = w_in[l]
    offs = {}
    acc = 0
    for name, size in (("mq", 512), ("mk", 512), ("mv", 1024), ("mo", 1024), ("mg", M_GATES), ("aq", 1024),
                       ("ak", 256), ("av", 256), ("gm", 2048), ("ga", 2048)):
        offs[name] = (acc, acc + size)
        acc += size
    seg = lambda nm: w[:, offs[nm][0]:offs[nm][1]]
    w_main = jnp.concatenate([seg(nm) for nm in ("gm", "ga", "mv", "mo", "aq", "mq", "mk")], axis=1).astype(BF16)
    w_aux = jnp.concatenate([jnp.pad(seg("mg"), ((0, 0), (0, GATE_LANES - M_GATES))), seg("ak"), seg("av")],
                            axis=1).astype(BF16)
    gate_b = mlstm_gate_b[l].reshape(2, 2, M_HEADS)

    def col_blocks(wt, width):
        k, n = wt.shape
        return jnp.transpose(wt.astype(BF16).reshape(k, n // width, width), (1, 0, 2))

    wts = dict(
        norm1=norm_ffn1_w[l][None, :], wi1=col_blocks(ffn1_wi[l], FFN_TF), wo1=ffn1_wo[l].astype(BF16),
        norm_mix=norm_mix_w[l][None, :], w_main=col_blocks(w_main, PROJ_TN), w_aux=w_aux,
        bias_i=gate_b[:, 0].reshape(GATE_ROWS, 1), bias_f=gate_b[:, 1].reshape(GATE_ROWS, 1),
        mlstm_norm=mlstm_norm_w[l][None, :],
        sink=attn_sink[l], q_norm=jnp.tile(attn_q_norm_w[l], A_KV)[None, :],
        k_norm=jnp.tile(attn_k_norm_w[l], A_KV)[None, :],
        w_proj_m=w_proj_m[l].astype(BF16), w_proj_a=w_proj_a[l].astype(BF16), w_out=w_out[l].astype(BF16),
        norm2=norm_ffn2_w[l][None, :], wi2=col_blocks(ffn2_wi[l], FFN_TF), wo2=ffn2_wo[l].astype(BF16),
    )

    y_prompt, (c_new, n_new, m_new), (k_new, v_new) = _trunk(
        x_prompt, mod3, lambda i, tm: 0, wts, None, None, None, tm_ffn=512, tm_proj=1024, tm_merge=256)
    new_attn_k = k_new.reshape(bp, 1, tp, A_KV, A_HD)
    new_attn_v = v_new.reshape(bp, 1, tp, A_KV, A_HD)
    new_c = c_new.reshape(bp, 1, 2, M_HEADS, M_V, M_QK)
    new_n = n_new.reshape(bp, 1, 2, M_HEADS, M_QK)
    new_m = m_new.reshape(bp, 1, 2, M_HEADS)

    states = (state_mlstm_C[:, l].reshape(bs, GATE_ROWS, M_V, M_QK),
              state_mlstm_n[:, l].reshape(bs, GATE_ROWS, M_QK),
              state_mlstm_m[:, l].reshape(bs, GATE_ROWS, 1))
    cache = (cache_attn_k[:, l].reshape(bs, n_ctx, A_KV_W), cache_attn_v[:, l].reshape(bs, n_ctx, A_KV_W))
    y_sample, _, _ = _trunk(
        x_sample, mod3, lambda i, tm: 1 + (i * tm) // ts, wts, states, _rope_tables(ts), cache,
        tm_ffn=512, tm_proj=1024, tm_merge=256)

    return (y_prompt, y_sample, new_attn_k, new_attn_v, new_c, new_n, new_m)
```

```python
import functools

import jax
import jax.numpy as jnp
from jax import lax
from jax.experimental import pallas as pl
from jax.experimental.pallas import tpu as pltpu

F32 = jnp.float32
BF16 = jnp.bfloat16
LOG2E = 1.4426950408889634

D_MODEL = 2048
D_FF = 5632
N_MOD = 9
NORM_EPS = 1e-6
GRID_W = 64
ROPE_THETA = 10000.0
M_HEADS = 4
M_QK = 128
M_V = 256
M_GATES = 4 * M_HEADS
A_HEADS = 16
A_KV = 4
A_GROUP = A_HEADS // A_KV
A_HD = 64
A_Q_W = A_HEADS * A_HD
A_KV_W = A_KV * A_HD
WINDOW = 128

Z_GM, Z_GA, Z_MV, Z_MO, Z_AQ, Z_MQ, Z_MK = 0, 2048, 4096, 5120, 6144, 7168, 7680
Z_COLS = 8192
GATE_LANES = 128
AUX_COLS = GATE_LANES + 2 * A_KV_W
GATE_ROWS = 2 * M_HEADS

FFN_TF = 256
PROJ_TN = 1024
MLSTM_BLOCK = 256
XPOSE_ROWS = 128

VMEM_LIMIT = 56 * 1024 * 1024
NT_DIMS = (((1,), (1,)), ((), ()))
TN_DIMS = (((0,), (0,)), ((), ()))


def _cparams(*sem):
    return pltpu.CompilerParams(dimension_semantics=sem, vmem_limit_bytes=VMEM_LIMIT)


def _sigmoid(x):
    return 1.0 / (1.0 + jnp.exp(-x))


def _norm_modulate(x, norm_w, shift, scale):
    ms = jnp.mean(x * x, axis=-1, keepdims=True)
    y = x * lax.rsqrt(ms + NORM_EPS) * norm_w
    return y * (1.0 + scale) + shift


def _mod_kernel(c_ref, w_ref, b_ref, o_ref):
    c = c_ref[...]
    s = (c * _sigmoid(c)).astype(BF16)
    o_ref[...] = jnp.dot(s, w_ref[...].astype(BF16), preferred_element_type=F32) + b_ref[...]


def _modulation(cond, ada_w, ada_b, tn=1024):
    rows, d = cond.shape
    n = ada_w.shape[1]
    return pl.pallas_call(
        _mod_kernel,
        grid=(n // tn,),
        in_specs=[
            pl.BlockSpec((rows, d), lambda j: (0, 0)),
            pl.BlockSpec((d, tn), lambda j: (0, j)),
            pl.BlockSpec((1, tn), lambda j: (0, j)),
        ],
        out_specs=pl.BlockSpec((rows, tn), lambda j: (0, j)),
        out_shape=jax.ShapeDtypeStruct((rows, n), F32),
        compiler_params=_cparams("arbitrary"),
        name="modulation",
    )(cond, ada_w, ada_b)


def _ffn_kernel(x_ref, mod_ref, nw_ref, wg_ref, wu_ref, wo_ref, o_ref, h_ref, *, k_shift):
    j = pl.program_id(1)

    @pl.when(j == 0)
    def _():
        h = _norm_modulate(x_ref[...], nw_ref[...], mod_ref[0, k_shift:k_shift + 1, :],
                           mod_ref[0, k_shift + 1:k_shift + 2, :])
        h_ref[...] = h.astype(BF16)
        o_ref[...] = jnp.zeros_like(o_ref)

    h = h_ref[...]
    g = jnp.dot(h, wg_ref[...], preferred_element_type=F32)
    u = jnp.dot(h, wu_ref[...], preferred_element_type=F32)
    a = (g * _sigmoid(g) * u).astype(BF16)
    o_ref[...] += jnp.dot(a, wo_ref[...], preferred_element_type=F32)

    @pl.when(j == pl.num_programs(1) - 1)
    def _():
        gate = mod_ref[0, k_shift + 2:k_shift + 3, :]
        o_ref[...] = x_ref[...] + 0.5 * gate * o_ref[...]


def _ffn(x, mod3, row_of_tile, norm_w, wi, wo, k_shift, tm):
    n_tok, d = x.shape
    tf = FFN_TF
    nf = D_FF // tf
    return pl.pallas_call(
        functools.partial(_ffn_kernel, k_shift=k_shift),
        grid=(n_tok // tm, nf),
        in_specs=[
            pl.BlockSpec((tm, d), lambda i, j: (i, 0)),
            pl.BlockSpec((1, N_MOD, d), lambda i, j: (row_of_tile(i, tm), 0, 0)),
            pl.BlockSpec((1, d), lambda i, j: (0, 0)),
            pl.BlockSpec((d, tf), lambda i, j: (0, j)),
            pl.BlockSpec((d, tf), lambda i, j: (0, j + nf)),
            pl.BlockSpec((tf, d), lambda i, j: (j, 0)),
        ],
        out_specs=pl.BlockSpec((tm, d), lambda i, j: (i, 0)),
        out_shape=jax.ShapeDtypeStruct((n_tok, d), F32),
        scratch_shapes=[pltpu.VMEM((tm, d), BF16)],
        compiler_params=_cparams("parallel", "arbitrary"),
        name="ffn",
    )(x, mod3, norm_w, wi, wi, wo)


def _proj_kernel(x_ref, mod_ref, nw_ref, w_ref, waux_ref, z_ref, g_ref, kv_ref, h_ref):
    j = pl.program_id(1)

    @pl.when(j == 0)
    def _():
        h = _norm_modulate(x_ref[...], nw_ref[...], mod_ref[0, 3:4, :], mod_ref[0, 4:5, :]).astype(BF16)
        h_ref[...] = h
        aux = jnp.dot(h, waux_ref[...], preferred_element_type=F32)
        g_ref[...] = aux[:, :GATE_LANES]
        kv_ref[...] = aux[:, GATE_LANES:].astype(BF16)

    half = z_ref.shape[1] // 2
    h = h_ref[...]
    z_ref[:, :half] = jnp.dot(h, w_ref[:, :half], preferred_element_type=F32).astype(BF16)
    z_ref[:, half:] = jnp.dot(h, w_ref[:, half:], preferred_element_type=F32).astype(BF16)


def _mixer_proj(x, mod3, row_of_tile, norm_w, w_main, w_aux, tm):
    n_tok, d = x.shape
    tn = PROJ_TN
    once = functools.partial(pl.BlockSpec, pipeline_mode=pl.Buffered(1))
    return pl.pallas_call(
        _proj_kernel,
        grid=(n_tok // tm, Z_COLS // tn),
        in_specs=[
            pl.BlockSpec((tm, d), lambda i, j: (i, 0)),
            pl.BlockSpec((1, N_MOD, d), lambda i, j: (row_of_tile(i, tm), 0, 0)),
            pl.BlockSpec((1, d), lambda i, j: (0, 0)),
            pl.BlockSpec((d, tn), lambda i, j: (0, j)),
            once((d, AUX_COLS), lambda i, j: (0, 0)),
        ],
        out_specs=[
            pl.BlockSpec((tm, tn), lambda i, j: (i, j)),
            pl.BlockSpec((tm, GATE_LANES), lambda i, j: (i, 0)),
            pl.BlockSpec((tm, 2 * A_KV_W), lambda i, j: (i, 0)),
        ],
        out_shape=[
            jax.ShapeDtypeStruct((n_tok, Z_COLS), BF16),
            jax.ShapeDtypeStruct((n_tok, GATE_LANES), F32),
            jax.ShapeDtypeStruct((n_tok, 2 * A_KV_W), BF16),
        ],
        scratch_shapes=[pltpu.VMEM((tm, d), BF16)],
        compiler_params=_cparams("parallel", "arbitrary"),
        name="mixer_proj",
    )(x, mod3, norm_w, w_main, w_aux)


def _scan_lanes(x, op, fill, reverse):
    n = x.shape[-1]
    lane = lax.broadcasted_iota(jnp.int32, x.shape, 1)
    s = 1
    while s < n:
        if reverse:
            x = op(x, jnp.where(lane < n - s, pltpu.roll(x, n - s, 1), fill))
        else:
            x = op(x, jnp.where(lane >= s, pltpu.roll(x, s, 1), fill))
        s *= 2
    return x


def _dir_scan(x, op, fill, fwd_rows):
    return jnp.where(fwd_rows, _scan_lanes(x, op, fill, False), _scan_lanes(x, op, fill, True))


def _log_sigmoid(x):
    return jnp.minimum(x, 0.0) - jnp.log1p(jnp.exp(-jnp.abs(x)))


def _mlstm_kernel(*refs, n_tok, has_state):
    if has_state:
        (q_ref, k_ref, v_ref, og_ref, gi_ref, gf_ref, bi_ref, bf_ref, nw_ref, c0_ref, n0_ref, m0_ref,
         o_ref) = refs
    else:
        (q_ref, k_ref, v_ref, og_ref, gi_ref, gf_ref, bi_ref, bf_ref, nw_ref,
         o_ref, c_out, n_out, m_out) = refs
    blk = min(MLSTM_BLOCK, n_tok)
    nq = n_tok // blk

    fwd_rows = lax.broadcasted_iota(jnp.int32, (GATE_ROWS, 1), 0) < M_HEADS
    ig = gi_ref[0] + bi_ref[...]
    cum_f = _dir_scan(_log_sigmoid(gf_ref[0] + bf_ref[...]), jnp.add, 0.0, fwd_rows)
    a = ig - cum_f
    m0 = m0_ref[0] if has_state else jnp.zeros((GATE_ROWS, 1), F32)
    mx = jnp.maximum(_dir_scan(a, jnp.maximum, -jnp.inf, fwd_rows), m0)
    stats = [-mx, cum_f + mx]
    if not has_state:
        mx_end = jnp.where(fwd_rows, mx[:, n_tok - 1:n_tok], mx[:, 0:1])
        cum_end = jnp.where(fwd_rows, cum_f[:, n_tok - 1:n_tok], cum_f[:, 0:1])
        stats.append(jnp.exp(a - mx_end))
        m_out[0] = cum_end + mx_end
    pad = jnp.zeros((XPOSE_ROWS - GATE_ROWS * len(stats), n_tok), F32)
    cols = jnp.concatenate(stats + [pad], axis=0).T

    row_i = lax.broadcasted_iota(jnp.int32, (blk, blk), 0)
    col_i = lax.broadcasted_iota(jnp.int32, (blk, blk), 1)
    visible = (col_i <= row_i, col_i >= row_i)

    for h in range(M_HEADS):
        qk_cols = slice(h * M_QK, (h + 1) * M_QK)
        v_cols = slice(h * M_V, (h + 1) * M_V)
        qf = q_ref[0, :, qk_cols].astype(F32) * (M_QK ** -0.5)
        q = qf.astype(BF16)
        k = k_ref[0, :, qk_cols]
        v = v_ref[0, :, v_cols]
        for qi in range(nq):
            rows = slice(qi * blk, (qi + 1) * blk)
            h_sum = None
            for d in range(2):
                r = M_HEADS * d + h
                lo, hi = (0, (qi + 1) * blk) if d == 0 else (qi * blk, n_tok)
                dg = hi - lo - blk if d == 0 else 0
                u_col = cols[rows, r:r + 1]
                mt_col = cols[rows, GATE_ROWS + r:GATE_ROWS + r + 1]
                dm = u_col + a[r:r + 1, lo:hi]
                parts = [dm[:, :dg]] if dg > 0 else []
                parts.append(jnp.where(visible[d], dm[:, dg:dg + blk], -jnp.inf))
                if dg + blk < hi - lo:
                    parts.append(dm[:, dg + blk:])
                dm = jnp.concatenate(parts, axis=1) if len(parts) > 1 else parts[0]
                s = lax.dot_general(q[rows], k[lo:hi], NT_DIMS, preferred_element_type=F32) * jnp.exp(dm)
                num = jnp.dot(s.astype(BF16), v[lo:hi], preferred_element_type=F32)
                den = jnp.sum(s, axis=1, keepdims=True)
                if has_state:
                    decay = jnp.exp(u_col + m0[r:r + 1, :])
                    num = num + decay * lax.dot_general(q[rows], c0_ref[0, r].astype(BF16), NT_DIMS,
                                                        preferred_element_type=F32)
                    den = den + decay * jnp.sum(qf[rows] * n0_ref[0, r:r + 1, :], axis=1, keepdims=True)
                h_dir = num / jnp.maximum(jnp.abs(den), jnp.exp(-mt_col))
                h_sum = h_dir if h_sum is None else h_sum + h_dir
            hn = h_sum * lax.rsqrt(jnp.mean(h_sum * h_sum, axis=-1, keepdims=True) + NORM_EPS) * nw_ref[:, v_cols]
            o_ref[0, rows, v_cols] = (hn * _sigmoid(og_ref[0, rows, v_cols].astype(F32))).astype(BF16)
        if not has_state:
            kf = k.astype(F32)
            vf = v.astype(F32)
            for d in range(2):
                r = M_HEADS * d + h
                wk_col = cols[:, 2 * GATE_ROWS + r:2 * GATE_ROWS + r + 1]
                c_out[0, r] = lax.dot_general((vf * wk_col).astype(BF16), k, TN_DIMS, preferred_element_type=F32)
                n_out[0, r:r + 1, :] = jnp.sum(kf * wk_col, axis=0, keepdims=True)


def _mlstm(z3, gate_i, gate_f, bias_i, bias_f, norm_w, states):
    bsz, n_tok, _ = z3.shape
    has_state = states is not None
    qk_w, v_w = M_HEADS * M_QK, M_HEADS * M_V
    in_specs = [
        pl.BlockSpec((1, n_tok, qk_w), lambda b: (b, 0, Z_MQ // qk_w)),
        pl.BlockSpec((1, n_tok, qk_w), lambda b: (b, 0, Z_MK // qk_w)),
        pl.BlockSpec((1, n_tok, v_w), lambda b: (b, 0, Z_MV // v_w)),
        pl.BlockSpec((1, n_tok, v_w), lambda b: (b, 0, Z_MO // v_w)),
        pl.BlockSpec((1, GATE_ROWS, n_tok), lambda b: (b, 0, 0)),
        pl.BlockSpec((1, GATE_ROWS, n_tok), lambda b: (b, 0, 0)),
        pl.BlockSpec((GATE_ROWS, 1), lambda b: (0, 0)),
        pl.BlockSpec((GATE_ROWS, 1), lambda b: (0, 0)),
        pl.BlockSpec((1, v_w), lambda b: (0, 0)),
    ]
    state_specs = [
        pl.BlockSpec((1, GATE_ROWS, M_V, M_QK), lambda b: (b, 0, 0, 0)),
        pl.BlockSpec((1, GATE_ROWS, M_QK), lambda b: (b, 0, 0)),
        pl.BlockSpec((1, GATE_ROWS, 1), lambda b: (b, 0, 0)),
    ]
    out_specs = [pl.BlockSpec((1, n_tok, v_w), lambda b: (b, 0, 0))]
    out_shape = [jax.ShapeDtypeStruct((bsz, n_tok, v_w), BF16)]
    args = [z3, z3, z3, z3, gate_i, gate_f, bias_i, bias_f, norm_w]
    if has_state:
        in_specs += state_specs
        args += list(states)
    else:
        out_specs += state_specs
        out_shape += [
            jax.ShapeDtypeStruct((bsz, GATE_ROWS, M_V, M_QK), F32),
            jax.ShapeDtypeStruct((bsz, GATE_ROWS, M_QK), F32),
            jax.ShapeDtypeStruct((bsz, GATE_ROWS, 1), F32),
        ]
    return pl.pallas_call(
        functools.partial(_mlstm_kernel, n_tok=n_tok, has_state=has_state),
        grid=(bsz,),
        in_specs=in_specs,
        out_specs=out_specs,
        out_shape=out_shape,
        compiler_params=_cparams("parallel"),
        name="mlstm",
    )(*args)


def _group_rms(x, w_row, ones_bd):
    x2 = x * x
    hi = x2.astype(BF16)
    lo = (x2 - hi.astype(F32)).astype(BF16)
    ss = jnp.dot(hi, ones_bd, preferred_element_type=F32) + jnp.dot(lo, ones_bd, preferred_element_type=F32)
    return x * lax.rsqrt(ss * (1.0 / A_HD) + NORM_EPS) * w_row


def _rope(x, cos, sin_signed, lane_lo):
    n = x.shape[-1]
    partner = jnp.where(lane_lo, pltpu.roll(x, n - 16, 1), pltpu.roll(x, 16, 1))
    return x * cos + partner * sin_signed


def _attend_block(q, sink_col, k_loc, v_loc, kinds, masks, k_ctx=None, v_ctx=None):
    s_loc = lax.dot_general(q, k_loc, NT_DIMS, preferred_element_type=F32)
    tiles = []
    for j, kind in enumerate(kinds):
        t = s_loc[:, j * WINDOW:(j + 1) * WINDOW]
        tiles.append(t if kind == "cur" else jnp.where(masks[kind], t, -jnp.inf))
    n_loc = len(tiles)
    if k_ctx is not None:
        s_ctx = lax.dot_general(q, k_ctx, NT_DIMS, preferred_element_type=F32)
        tiles += [s_ctx[:, j * WINDOW:(j + 1) * WINDOW] for j in range(k_ctx.shape[0] // WINDOW)]
    tile_max = tiles[0]
    for t in tiles[1:]:
        tile_max = jnp.maximum(tile_max, t)
    m = jnp.maximum(jnp.max(tile_max, axis=1, keepdims=True), sink_col)
    p = [jnp.exp2(t - m).astype(BF16) for t in tiles]
    acc = jnp.dot(jnp.concatenate(p[:n_loc], axis=1), v_loc, preferred_element_type=F32)
    if k_ctx is not None:
        acc = acc + jnp.dot(jnp.concatenate(p[n_loc:], axis=1), v_ctx, preferred_element_type=F32)
    den = acc[:, A_HD:A_HD + 1] + jnp.exp2(sink_col - m)
    return acc[:, :A_HD] / den


def _attn_kernel(*refs, n_tok, latent):
    if latent:
        (sink_ref, q_ref, k_ref, v_ref, qw_ref, kw_ref, cos_ref, sin_ref, ck_ref, cv_ref,
         o_ref, qh_ref, kg_ref, va_ref) = refs
    else:
        (sink_ref, q_ref, k_ref, v_ref, qw_ref, kw_ref,
         o_ref, ko_ref, vo_ref) = refs

    lane = lax.broadcasted_iota(jnp.int32, (1, A_KV_W), 1)
    lane_lo = (lane % 32) < 16
    bd_r = lax.broadcasted_iota(jnp.int32, (A_KV_W, A_KV_W), 0) // A_HD
    bd_c = lax.broadcasted_iota(jnp.int32, (A_KV_W, A_KV_W), 1) // A_HD
    ones_bd = jnp.where(bd_r == bd_c, 1.0, 0.0).astype(BF16)

    k_all = _group_rms(k_ref[0].astype(F32), kw_ref[...], ones_bd)
    v_all = v_ref[0]
    if latent:
        k_all = _rope(k_all, cos_ref[...], sin_ref[...], lane_lo)
    else:
        ko_ref[0] = k_all
        vo_ref[0] = v_all.astype(F32)

    q_blk = WINDOW if latent else n_tok
    m_rows = A_GROUP * q_blk
    head_of_row = lax.broadcasted_iota(jnp.int32, (m_rows, 1), 0) // q_blk
    r_in = lax.broadcasted_iota(jnp.int32, (m_rows, WINDOW), 0) % WINDOW
    c_in = lax.broadcasted_iota(jnp.int32, (m_rows, WINDOW), 1)
    masks = {"prev": c_in >= r_in, "next": c_in <= r_in}
    ones_v = jnp.ones((n_tok, A_HD), BF16)

    def sink_column(g):
        col = jnp.zeros((m_rows, 1), F32)
        for a in range(A_GROUP):
            col = jnp.where(head_of_row == a, sink_ref[A_GROUP * g + a] * LOG2E, col)
        return col

    def unstack(o):
        return jnp.concatenate([o[a * q_blk:(a + 1) * q_blk, :] for a in range(A_GROUP)], axis=1).astype(BF16)

    for g in range(A_KV):
        g_cols = slice(A_KV_W * g, A_KV_W * (g + 1))
        h_cols = slice(A_HD * g, A_HD * (g + 1))
        qg = _group_rms(q_ref[0, :, g_cols].astype(F32), qw_ref[...], ones_bd)
        if latent:
            qg = _rope(qg, cos_ref[...], sin_ref[...], lane_lo)
        qg = qg * (A_HD ** -0.5 * LOG2E)
        kg = k_all[:, h_cols].astype(BF16)
        va = jnp.concatenate([v_all[:, h_cols], ones_v], axis=1)
        sink_col = sink_column(g)
        if not latent:
            q_stack = jnp.concatenate([qg[:, A_HD * a:A_HD * (a + 1)] for a in range(A_GROUP)], axis=0)
            o = _attend_block(q_stack.astype(BF16), sink_col, kg, va, ["cur"] * (n_tok // WINDOW), masks)
            o_ref[0, :, g_cols] = unstack(o)
            continue

        for a in range(A_GROUP):
            qh_ref[a] = qg[:, A_HD * a:A_HD * (a + 1)].astype(BF16)
        kg_ref[...] = kg
        va_ref[...] = va
        n_ctx = ck_ref.shape[1]
        ckg = ck_ref[0][:, h_cols].astype(BF16)
        cva = jnp.concatenate([cv_ref[0][:, h_cols].astype(BF16), jnp.ones((n_ctx, A_HD), BF16)], axis=1)
        nb = n_tok // WINDOW

        def q_block(i, kinds, k_start, g_cols=g_cols, sink_col=sink_col, ckg=ckg, cva=cva, qg=qg, kg=kg, va=va):
            n_keys = len(kinds) * WINDOW
            if isinstance(i, int):
                r0 = i * WINDOW
                q_stack = jnp.concatenate([qg[r0:r0 + WINDOW, A_HD * a:A_HD * (a + 1)] for a in range(A_GROUP)],
                                          axis=0).astype(BF16)
                k_loc = kg[k_start:k_start + n_keys]
                v_loc = va[k_start:k_start + n_keys]
            else:
                r0 = pl.multiple_of(i * WINDOW, WINDOW)
                k_start = pl.multiple_of(k_start, WINDOW)
                q_stack = jnp.concatenate([qh_ref[a, pl.ds(r0, WINDOW), :] for a in range(A_GROUP)], axis=0)
                k_loc = kg_ref[pl.ds(k_start, n_keys), :]
                v_loc = va_ref[pl.ds(k_start, n_keys), :]
            o = _attend_block(q_stack, sink_col, k_loc, v_loc, kinds, masks, ckg, cva)
            o_ref[0, pl.ds(r0, WINDOW), g_cols] = unstack(o)

        def interior(i, carry, q_block=q_block):
            q_block(i, ["prev", "cur", "next"], (i - 1) * WINDOW)
            return carry

        lax.fori_loop(1, nb - 1, interior, 0, unroll=3)
        q_block(0, ["cur", "next"], 0)
        q_block(nb - 1, ["prev", "cur"], (nb - 2) * WINDOW)


def _attention(z3, kv3, sink, q_norm_w, k_norm_w, rope_tabs, cache):
    bsz, n_tok, _ = z3.shape
    latent = cache is not None
    in_specs = [
        pl.BlockSpec(memory_space=pltpu.SMEM),
        pl.BlockSpec((1, n_tok, A_Q_W), lambda b: (b, 0, Z_AQ // A_Q_W)),
        pl.BlockSpec((1, n_tok, A_KV_W), lambda b: (b, 0, 0)),
        pl.BlockSpec((1, n_tok, A_KV_W), lambda b: (b, 0, 1)),
        pl.BlockSpec((1, A_KV_W), lambda b: (0, 0)),
        pl.BlockSpec((1, A_KV_W), lambda b: (0, 0)),
    ]
    args = [sink, z3, kv3, kv3, q_norm_w, k_norm_w]
    out_specs = [pl.BlockSpec((1, n_tok, A_Q_W), lambda b: (b, 0, 0))]
    out_shape = [jax.ShapeDtypeStruct((bsz, n_tok, A_Q_W), BF16)]
    scratch = []
    if latent:
        assert n_tok // WINDOW >= 3
        n_ctx = cache[0].shape[1]
        in_specs += [
            pl.BlockSpec((n_tok, A_KV_W), lambda b: (0, 0)),
            pl.BlockSpec((n_tok, A_KV_W), lambda b: (0, 0)),
            pl.BlockSpec((1, n_ctx, A_KV_W), lambda b: (b, 0, 0)),
            pl.BlockSpec((1, n_ctx, A_KV_W), lambda b: (b, 0, 0)),
        ]
        args += [rope_tabs[0], rope_tabs[1], cache[0], cache[1]]
        scratch = [
            pltpu.VMEM((A_GROUP, n_tok, A_HD), BF16),
            pltpu.VMEM((n_tok, A_HD), BF16),
            pltpu.VMEM((n_tok, 2 * A_HD), BF16),
        ]
    else:
        out_specs += [pl.BlockSpec((1, n_tok, A_KV_W), lambda b: (b, 0, 0))] * 2
        out_shape += [jax.ShapeDtypeStruct((bsz, n_tok, A_KV_W), F32)] * 2
    return pl.pallas_call(
        functools.partial(_attn_kernel, n_tok=n_tok, latent=latent),
        grid=(bsz,),
        in_specs=in_specs,
        out_specs=out_specs,
        out_shape=out_shape,
        scratch_shapes=scratch,
        compiler_params=_cparams("parallel"),
        name="attention",
    )(*args)


def _merge_kernel(x_ref, mod_ref, hm_ref, ha_ref, gm_ref, ga_ref, wpm_ref, wpa_ref, wo_ref, o_ref):
    pm = jnp.dot(hm_ref[...], wpm_ref[...], preferred_element_type=F32)
    pa = jnp.dot(ha_ref[...], wpa_ref[...], preferred_element_type=F32)
    u = _sigmoid(gm_ref[...].astype(F32)) * pm + _sigmoid(ga_ref[...].astype(F32)) * pa
    mix = jnp.dot(u.astype(BF16), wo_ref[...], preferred_element_type=F32)
    o_ref[...] = x_ref[...] + mod_ref[0, 5:6, :] * mix


def _merge(x, mod3, row_of_tile, hm, ha, z, w_proj_m, w_proj_a, w_out, tm):
    n_tok, d = x.shape
    resident = functools.partial(pl.BlockSpec, pipeline_mode=pl.Buffered(1))
    return pl.pallas_call(
        _merge_kernel,
        grid=(n_tok // tm,),
        in_specs=[
            pl.BlockSpec((tm, d), lambda i: (i, 0)),
            pl.BlockSpec((1, N_MOD, d), lambda i: (row_of_tile(i, tm), 0, 0)),
            pl.BlockSpec((tm, M_HEADS * M_V), lambda i: (i, 0)),
            pl.BlockSpec((tm, A_Q_W), lambda i: (i, 0)),
            pl.BlockSpec((tm, d), lambda i: (i, Z_GM // D_MODEL)),
            pl.BlockSpec((tm, d), lambda i: (i, Z_GA // D_MODEL)),
            resident((M_HEADS * M_V, d), lambda i: (0, 0)),
            resident((A_Q_W, d), lambda i: (0, 0)),
            resident((d, d), lambda i: (0, 0)),
        ],
        out_specs=pl.BlockSpec((tm, d), lambda i: (i, 0)),
        out_shape=jax.ShapeDtypeStruct((n_tok, d), F32),
        compiler_params=_cparams("parallel"),
        name="merge",
    )(x, mod3, hm, ha, z, z, w_proj_m, w_proj_a, w_out)


def _rope_tables(n_tok):
    nf = A_HD // 4
    inv = ROPE_THETA ** (-jnp.arange(nf, dtype=F32) / nf)
    tok = jnp.arange(n_tok)
    pos = jnp.stack([tok // GRID_W, tok % GRID_W], axis=1).astype(F32)
    ang = pos[:, :, None] * inv
    cos = jnp.cos(ang)
    sin = jnp.sin(ang)
    cos_h = jnp.concatenate([cos, cos], axis=-1).reshape(n_tok, A_HD)
    sin_h = jnp.concatenate([-sin, sin], axis=-1).reshape(n_tok, A_HD)
    return jnp.tile(cos_h, (1, A_KV)), jnp.tile(sin_h, (1, A_KV))


def _trunk(x, mod3, row_of_tile, wts, states, rope_tabs, cache, tm_ffn, tm_proj, tm_merge):
    bsz, n_tok, d = x.shape
    x2 = x.reshape(bsz * n_tok, d)
    x2 = _ffn(x2, mod3, row_of_tile, wts["norm1"], wts["wi1"], wts["wo1"], 0, tm_ffn)
    z, gates, kv = _mixer_proj(x2, mod3, row_of_tile, wts["norm_mix"], wts["w_main"], wts["w_aux"], tm_proj)
    z3 = z.reshape(bsz, n_tok, Z_COLS)
    g = jnp.transpose(gates[:, :M_GATES].reshape(bsz, n_tok, 2, 2, M_HEADS), (0, 2, 3, 4, 1))
    gate_i = g[:, :, 0].reshape(bsz, GATE_ROWS, n_tok)
    gate_f = g[:, :, 1].reshape(bsz, GATE_ROWS, n_tok)
    m_out = _mlstm(z3, gate_i, gate_f, wts["bias_i"], wts["bias_f"], wts["mlstm_norm"], states)
    a_out = _attention(z3, kv.reshape(bsz, n_tok, 2 * A_KV_W), wts["sink"], wts["q_norm"], wts["k_norm"],
                       rope_tabs, cache)
    hm = m_out[0].reshape(bsz * n_tok, M_HEADS * M_V)
    ha = a_out[0].reshape(bsz * n_tok, A_Q_W)
    x2 = _merge(x2, mod3, row_of_tile, hm, ha, z, wts["w_proj_m"], wts["w_proj_a"], wts["w_out"], tm_merge)
    x2 = _ffn(x2, mod3, row_of_tile, wts["norm2"], wts["wi2"], wts["wo2"], 6, tm_ffn)
    return x2.reshape(bsz, n_tok, d), m_out[1:], a_out[1:]


def kernel(x_prompt, x_sample, cache_attn_k, cache_attn_v, state_mlstm_C, state_mlstm_n, state_mlstm_m, c, c_ctx, ada_w, ada_b, norm_ffn1_w, ffn1_wi, ffn1_wo, norm_mix_w, w_in, mlstm_gate_b, mlstm_norm_w, attn_q_norm_w, attn_k_norm_w, attn_sink, w_proj_m, w_proj_a, w_out, norm_ffn2_w, ffn2_wi, ffn2_wo):
    bp, tp, d = x_prompt.shape
    bs, ts, _ = x_sample.shape
    n_ctx = cache_attn_k.shape[2]
    l = 0

    n_rows = 16
    cond = jnp.concatenate([c_ctx[None, :], c, jnp.zeros((n_rows - 1 - bs, d), F32)], axis=0)
    mod3 = _modulation(cond, ada_w[l], ada_b[l][None, :]).reshape(n_rows, N_MOD, d)

    w = w_in[l]
    offs = {}
    acc = 0
    for name, size in (("mq", 512), ("mk", 512), ("mv", 1024), ("mo", 1024), ("mg", M_GATES), ("aq", 1024),
                       ("ak", 256), ("av", 256), ("gm", 2048), ("ga", 2048)):
        offs[name] = (acc, acc + size)
        acc += size
    seg = lambda nm: w[:, offs[nm][0]:offs[nm][1]]
    w_main = jnp.concatenate([seg(nm) for nm in ("gm", "ga", "mv", "mo", "aq", "mq", "mk")], axis=1).astype(BF16)
    w_aux = jnp.concatenate([jnp.pad(seg("mg"), ((0, 0), (0, GATE_LANES - M_GATES))), seg("ak"), seg("av")],
                            axis=1).astype(BF16)
    gate_b = mlstm_gate_b[l].reshape(2, 2, M_HEADS)
    wts = dict(
        norm1=norm_ffn1_w[l][None, :], wi1=ffn1_wi[l].astype(BF16), wo1=ffn1_wo[l].astype(BF16),
        norm_mix=norm_mix_w[l][None, :], w_main=w_main, w_aux=w_aux,
        bias_i=gate_b[:, 0].reshape(GATE_ROWS, 1), bias_f=gate_b[:, 1].reshape(GATE_ROWS, 1),
        mlstm_norm=mlstm_norm_w[l][None, :],
        sink=attn_sink[l], q_norm=jnp.tile(attn_q_norm_w[l], A_KV)[None, :],
        k_norm=jnp.tile(attn_k_norm_w[l], A_KV)[None, :],
        w_proj_m=w_proj_m[l].astype(BF16), w_proj_a=w_proj_a[l].astype(BF16), w_out=w_out[l].astype(BF16),
        norm2=norm_ffn2_w[l][None, :], wi2=ffn2_wi[l].astype(BF16), wo2=ffn2_wo[l].astype(BF16),
    )

    y_prompt, (c_new, n_new, m_new), (k_new, v_new) = _trunk(
        x_prompt, mod3, lambda i, tm: 0, wts, None, None, None, tm_ffn=1024, tm_proj=1024, tm_merge=256)
    new_attn_k = k_new.reshape(bp, 1, tp, A_KV, A_HD)
    new_attn_v = v_new.reshape(bp, 1, tp, A_KV, A_HD)
    new_c = c_new.reshape(bp, 1, 2, M_HEADS, M_V, M_QK)
    new_n = n_new.reshape(bp, 1, 2, M_HEADS, M_QK)
    new_m = m_new.reshape(bp, 1, 2, M_HEADS)

    states = (state_mlstm_C[:, l].reshape(bs, GATE_ROWS, M_V, M_QK),
              state_mlstm_n[:, l].reshape(bs, GATE_ROWS, M_QK),
              state_mlstm_m[:, l].reshape(bs, GATE_ROWS, 1))
    cache = (cache_attn_k[:, l].reshape(bs, n_ctx, A_KV_W), cache_attn_v[:, l].reshape(bs, n_ctx, A_KV_W))
    y_sample, _, _ = _trunk(
        x_sample, mod3, lambda i, tm: 1 + (i * tm) // ts, wts, states, _rope_tables(ts), cache,
        tm_ffn=1024, tm_proj=1024, tm_merge=256)

    return (y_prompt, y_sample, new_attn_k, new_attn_v, new_c, new_n, new_m)
```

```python
import functools

import jax
import jax.numpy as jnp
from jax import lax
from jax.experimental import pallas as pl
from jax.experimental.pallas import tpu as pltpu

F32 = jnp.float32
BF16 = jnp.bfloat16
LOG2E = 1.4426950408889634

D_MODEL = 2048
D_FF = 5632
N_MOD = 9
NORM_EPS = 1e-6
GRID_W = 64
ROPE_THETA = 10000.0
M_HEADS = 4
M_QK = 128
M_V = 256
M_GATES = 4 * M_HEADS
A_HEADS = 16
A_KV = 4
A_GROUP = A_HEADS // A_KV
A_HD = 64
A_Q_W = A_HEADS * A_HD
A_KV_W = A_KV * A_HD
WINDOW = 128

Z_GM, Z_GA, Z_MV, Z_MO, Z_AQ, Z_MQ, Z_MK = 0, 2048, 4096, 5120, 6144, 7168, 7680
Z_COLS = 8192
GATE_LANES = 128
AUX_COLS = GATE_LANES + 2 * A_KV_W
GATE_ROWS = 2 * M_HEADS

FFN_TF = 512
PROJ_TN = 1024
MLSTM_BLOCK = 256
XPOSE_ROWS = 128

VMEM_LIMIT = 56 * 1024 * 1024
NT_DIMS = (((1,), (1,)), ((), ()))
TN_DIMS = (((0,), (0,)), ((), ()))


def _cparams(*sem):
    return pltpu.CompilerParams(dimension_semantics=sem, vmem_limit_bytes=VMEM_LIMIT)


def _sigmoid(x):
    return 1.0 / (1.0 + jnp.exp(-x))


def _norm_modulate(x, norm_w, shift, scale):
    ms = jnp.mean(x * x, axis=-1, keepdims=True)
    y = x * lax.rsqrt(ms + NORM_EPS) * norm_w
    return y * (1.0 + scale) + shift


def _mod_kernel(c_ref, w_ref, b_ref, o_ref):
    c = c_ref[...]
    s = (c * _sigmoid(c)).astype(BF16)
    o_ref[...] = jnp.dot(s, w_ref[...].astype(BF16), preferred_element_type=F32) + b_ref[...]


def _modulation(cond, ada_w, ada_b, tn=1024):
    rows, d = cond.shape
    n = ada_w.shape[1]
    return pl.pallas_call(
        _mod_kernel,
        grid=(n // tn,),
        in_specs=[
            pl.BlockSpec((rows, d), lambda j: (0, 0)),
            pl.BlockSpec((d, tn), lambda j: (0, j)),
            pl.BlockSpec((1, tn), lambda j: (0, j)),
        ],
        out_specs=pl.BlockSpec((rows, tn), lambda j: (0, j)),
        out_shape=jax.ShapeDtypeStruct((rows, n), F32),
        compiler_params=_cparams("arbitrary"),
        name="modulation",
    )(cond, ada_w, ada_b)


def _ffn_kernel(x_ref, mod_ref, nw_ref, wg_ref, wu_ref, wo_ref, o_ref, h_ref, *, k_shift):
    j = pl.program_id(1)

    @pl.when(j == 0)
    def _():
        h = _norm_modulate(x_ref[...], nw_ref[...], mod_ref[0, k_shift:k_shift + 1, :],
                           mod_ref[0, k_shift + 1:k_shift + 2, :])
        h_ref[...] = h.astype(BF16)
        o_ref[...] = x_ref[...]

    h = h_ref[...]
    g = jnp.dot(h, wg_ref[...], preferred_element_type=F32)
    u = jnp.dot(h, wu_ref[...], preferred_element_type=F32)
    a = (g * _sigmoid(g) * u).astype(BF16)
    half_gate = 0.5 * mod_ref[0, k_shift + 2:k_shift + 3, :]
    o_ref[...] += half_gate * jnp.dot(a, wo_ref[...], preferred_element_type=F32)


def _ffn(x, mod3, row_of_tile, norm_w, wi, wo, k_shift, tm):
    n_tok, d = x.shape
    tf = FFN_TF
    nf = D_FF // tf
    return pl.pallas_call(
        functools.partial(_ffn_kernel, k_shift=k_shift),
        grid=(n_tok // tm, nf),
        in_specs=[
            pl.BlockSpec((tm, d), lambda i, j: (i, 0)),
            pl.BlockSpec((1, N_MOD, d), lambda i, j: (row_of_tile(i, tm), 0, 0)),
            pl.BlockSpec((1, d), lambda i, j: (0, 0)),
            pl.BlockSpec((d, tf), lambda i, j: (0, j)),
            pl.BlockSpec((d, tf), lambda i, j: (0, j + nf)),
            pl.BlockSpec((tf, d), lambda i, j: (j, 0)),
        ],
        out_specs=pl.BlockSpec((tm, d), lambda i, j: (i, 0)),
        out_shape=jax.ShapeDtypeStruct((n_tok, d), F32),
        scratch_shapes=[pltpu.VMEM((tm, d), BF16)],
        compiler_params=_cparams("parallel", "arbitrary"),
        name="ffn",
    )(x, mod3, norm_w, wi, wi, wo)


W_IN_SEGS = (("mq", 512), ("mk", 512), ("mv", 1024), ("mo", 1024), ("mg", M_GATES), ("aq", 1024),
             ("ak", 256), ("av", 256), ("gm", 2048), ("ga", 2048))
W_IN_OFF = {}
for _name, _size in W_IN_SEGS:
    W_IN_OFF[_name] = (sum(s for _, s in W_IN_SEGS[:len(W_IN_OFF)]), _size)
IN_COLS = sum(s for _, s in W_IN_SEGS)
MAIN_ORDER = ("gm", "ga", "mv", "mo", "aq", "mq", "mk")
REGROUP_ROWS = 128


def _regroup_kernel(w_ref, main_ref, aux_ref):
    w = w_ref[...]
    t0 = W_IN_OFF["aq"][0]
    tail = w[:, t0:]

    def seg(name):
        start, size = W_IN_OFF[name]
        return (tail[:, start - t0:start - t0 + size] if start >= t0 else w[:, start:start + size]).astype(BF16)

    col = 0
    for name in MAIN_ORDER:
        size = W_IN_OFF[name][1]
        main_ref[:, col:col + size] = seg(name)
        col += size
    g0 = W_IN_OFF["mg"][0]
    lane = lax.broadcasted_iota(jnp.int32, (1, GATE_LANES), 1)
    aux_ref[:, :GATE_LANES] = jnp.where(lane < M_GATES, w[:, g0:g0 + GATE_LANES], 0.0).astype(BF16)
    aux_ref[:, GATE_LANES:GATE_LANES + A_KV_W] = seg("ak")
    aux_ref[:, GATE_LANES + A_KV_W:] = seg("av")


def _regroup_w_in(w):
    d, n = w.shape
    assert n == IN_COLS and W_IN_OFF["mg"][0] % GATE_LANES == 0
    return pl.pallas_call(
        _regroup_kernel,
        grid=(d // REGROUP_ROWS,),
        in_specs=[pl.BlockSpec((REGROUP_ROWS, n), lambda i: (i, 0))],
        out_specs=[
            pl.BlockSpec((REGROUP_ROWS, Z_COLS), lambda i: (i, 0)),
            pl.BlockSpec((REGROUP_ROWS, AUX_COLS), lambda i: (i, 0)),
        ],
        out_shape=[jax.ShapeDtypeStruct((d, Z_COLS), BF16), jax.ShapeDtypeStruct((d, AUX_COLS), BF16)],
        compiler_params=_cparams("parallel"),
        name="regroup_w_in",
    )(w)


def _proj_kernel(x_ref, mod_ref, nw_ref, w_ref, waux_ref, z_ref, g_ref, kv_ref, h_ref):
    j = pl.program_id(1)

    @pl.when(j == 0)
    def _():
        h = _norm_modulate(x_ref[...], nw_ref[...], mod_ref[0, 3:4, :], mod_ref[0, 4:5, :]).astype(BF16)
        h_ref[...] = h
        aux = jnp.dot(h, waux_ref[...], preferred_element_type=F32)
        g_ref[...] = aux[:, :GATE_LANES]
        kv_ref[...] = aux[:, GATE_LANES:].astype(BF16)

    half = z_ref.shape[1] // 2
    h = h_ref[...]
    z_ref[:, :half] = jnp.dot(h, w_ref[:, :half], preferred_element_type=F32).astype(BF16)
    z_ref[:, half:] = jnp.dot(h, w_ref[:, half:], preferred_element_type=F32).astype(BF16)


def _mixer_proj(x, mod3, row_of_tile, norm_w, w_main, w_aux, tm):
    n_tok, d = x.shape
    tn = PROJ_TN
    once = functools.partial(pl.BlockSpec, pipeline_mode=pl.Buffered(1))
    return pl.pallas_call(
        _proj_kernel,
        grid=(n_tok // tm, Z_COLS // tn),
        in_specs=[
            pl.BlockSpec((tm, d), lambda i, j: (i, 0)),
            pl.BlockSpec((1, N_MOD, d), lambda i, j: (row_of_tile(i, tm), 0, 0)),
            pl.BlockSpec((1, d), lambda i, j: (0, 0)),
            pl.BlockSpec((d, tn), lambda i, j: (0, j)),
            once((d, AUX_COLS), lambda i, j: (0, 0)),
        ],
        out_specs=[
            pl.BlockSpec((tm, tn), lambda i, j: (i, j)),
            pl.BlockSpec((tm, GATE_LANES), lambda i, j: (i, 0)),
            pl.BlockSpec((tm, 2 * A_KV_W), lambda i, j: (i, 0)),
        ],
        out_shape=[
            jax.ShapeDtypeStruct((n_tok, Z_COLS), BF16),
            jax.ShapeDtypeStruct((n_tok, GATE_LANES), F32),
            jax.ShapeDtypeStruct((n_tok, 2 * A_KV_W), BF16),
        ],
        scratch_shapes=[pltpu.VMEM((tm, d), BF16)],
        compiler_params=_cparams("parallel", "arbitrary"),
        name="mixer_proj",
    )(x, mod3, norm_w, w_main, w_aux)


def _scan_lanes(x, op, fill, reverse):
    n = x.shape[-1]
    lane = lax.broadcasted_iota(jnp.int32, x.shape, 1)
    s = 1
    while s < n:
        if reverse:
            x = op(x, jnp.where(lane < n - s, pltpu.roll(x, n - s, 1), fill))
        else:
            x = op(x, jnp.where(lane >= s, pltpu.roll(x, s, 1), fill))
        s *= 2
    return x


def _dir_scan(x, op, fill, fwd_rows):
    return jnp.where(fwd_rows, _scan_lanes(x, op, fill, False), _scan_lanes(x, op, fill, True))


def _log_sigmoid(x):
    return jnp.minimum(x, 0.0) - jnp.log1p(jnp.exp(-jnp.abs(x)))


def _mlstm_kernel(*refs, n_tok, has_state):
    if has_state:
        (q_ref, k_ref, v_ref, og_ref, gi_ref, gf_ref, bi_ref, bf_ref, nw_ref, c0_ref, n0_ref, m0_ref,
         o_ref) = refs
    else:
        (q_ref, k_ref, v_ref, og_ref, gi_ref, gf_ref, bi_ref, bf_ref, nw_ref,
         o_ref, c_out, n_out, m_out) = refs
    blk = min(MLSTM_BLOCK, n_tok)
    nq = n_tok // blk

    fwd_rows = lax.broadcasted_iota(jnp.int32, (GATE_ROWS, 1), 0) < M_HEADS
    ig = gi_ref[0] + bi_ref[...]
    cum_f = _dir_scan(_log_sigmoid(gf_ref[0] + bf_ref[...]), jnp.add, 0.0, fwd_rows)
    a = ig - cum_f
    m0 = m0_ref[0] if has_state else jnp.zeros((GATE_ROWS, 1), F32)
    mx = jnp.maximum(_dir_scan(a, jnp.maximum, -jnp.inf, fwd_rows), m0)
    stats = [-mx * LOG2E, -(cum_f + mx) * LOG2E]
    a2 = a * LOG2E
    m0_2 = m0 * LOG2E
    if not has_state:
        mx_end = jnp.where(fwd_rows, mx[:, n_tok - 1:n_tok], mx[:, 0:1])
        cum_end = jnp.where(fwd_rows, cum_f[:, n_tok - 1:n_tok], cum_f[:, 0:1])
        stats.append(jnp.exp(a - mx_end))
        m_out[0] = cum_end + mx_end
    pad = jnp.zeros((XPOSE_ROWS - GATE_ROWS * len(stats), n_tok), F32)
    cols = jnp.concatenate(stats + [pad], axis=0).T

    row_i = lax.broadcasted_iota(jnp.int32, (blk, blk), 0)
    col_i = lax.broadcasted_iota(jnp.int32, (blk, blk), 1)
    visible = (col_i <= row_i, col_i >= row_i)

    for h in range(M_HEADS):
        qk_cols = slice(h * M_QK, (h + 1) * M_QK)
        v_cols = slice(h * M_V, (h + 1) * M_V)
        qf = q_ref[0, :, qk_cols].astype(F32) * (M_QK ** -0.5)
        q = qf.astype(BF16)
        k = k_ref[0, :, qk_cols]
        v = v_ref[0, :, v_cols]
        for qi in range(nq):
            rows = slice(qi * blk, (qi + 1) * blk)
            h_sum = None
            for d in range(2):
                r = M_HEADS * d + h
                lo, hi = (0, (qi + 1) * blk) if d == 0 else (qi * blk, n_tok)
                dg = hi - lo - blk if d == 0 else 0
                u_col = cols[rows, r:r + 1]
                neg_mt_col = cols[rows, GATE_ROWS + r:GATE_ROWS + r + 1]
                dm = u_col + a2[r:r + 1, lo:hi]
                parts = [dm[:, :dg]] if dg > 0 else []
                parts.append(jnp.where(visible[d], dm[:, dg:dg + blk], -jnp.inf))
                if dg + blk < hi - lo:
                    parts.append(dm[:, dg + blk:])
                dm = jnp.concatenate(parts, axis=1) if len(parts) > 1 else parts[0]
                s = lax.dot_general(q[rows], k[lo:hi], NT_DIMS, preferred_element_type=F32) * jnp.exp2(dm)
                num = jnp.dot(s.astype(BF16), v[lo:hi], preferred_element_type=F32)
                den = jnp.sum(s, axis=1, keepdims=True)
                if has_state:
                    decay = jnp.exp2(u_col + m0_2[r:r + 1, :])
                    num = num + decay * lax.dot_general(q[rows], c0_ref[0, r].astype(BF16), NT_DIMS,
                                                        preferred_element_type=F32)
                    den = den + decay * jnp.sum(qf[rows] * n0_ref[0, r:r + 1, :], axis=1, keepdims=True)
                h_dir = num / jnp.maximum(jnp.abs(den), jnp.exp2(neg_mt_col))
                h_sum = h_dir if h_sum is None else h_sum + h_dir
            hn = h_sum * lax.rsqrt(jnp.mean(h_sum * h_sum, axis=-1, keepdims=True) + NORM_EPS) * nw_ref[:, v_cols]
            o_ref[0, rows, v_cols] = (hn * _sigmoid(og_ref[0, rows, v_cols].astype(F32))).astype(BF16)
        if not has_state:
            kf = k.astype(F32)
            vf = v.astype(F32)
            for d in range(2):
                r = M_HEADS * d + h
                wk_col = cols[:, 2 * GATE_ROWS + r:2 * GATE_ROWS + r + 1]
                c_out[0, r] = lax.dot_general((vf * wk_col).astype(BF16), k, TN_DIMS, preferred_element_type=F32)
                n_out[0, r:r + 1, :] = jnp.sum(kf * wk_col, axis=0, keepdims=True)


def _mlstm(z3, gate_i, gate_f, bias_i, bias_f, norm_w, states):
    bsz, n_tok, _ = z3.shape
    has_state = states is not None
    qk_w, v_w = M_HEADS * M_QK, M_HEADS * M_V
    in_specs = [
        pl.BlockSpec((1, n_tok, qk_w), lambda b: (b, 0, Z_MQ // qk_w)),
        pl.BlockSpec((1, n_tok, qk_w), lambda b: (b, 0, Z_MK // qk_w)),
        pl.BlockSpec((1, n_tok, v_w), lambda b: (b, 0, Z_MV // v_w)),
        pl.BlockSpec((1, n_tok, v_w), lambda b: (b, 0, Z_MO // v_w)),
        pl.BlockSpec((1, GATE_ROWS, n_tok), lambda b: (b, 0, 0)),
        pl.BlockSpec((1, GATE_ROWS, n_tok), lambda b: (b, 0, 0)),
        pl.BlockSpec((GATE_ROWS, 1), lambda b: (0, 0)),
        pl.BlockSpec((GATE_ROWS, 1), lambda b: (0, 0)),
        pl.BlockSpec((1, v_w), lambda b: (0, 0)),
    ]
    state_specs = [
        pl.BlockSpec((1, GATE_ROWS, M_V, M_QK), lambda b: (b, 0, 0, 0)),
        pl.BlockSpec((1, GATE_ROWS, M_QK), lambda b: (b, 0, 0)),
        pl.BlockSpec((1, GATE_ROWS, 1), lambda b: (b, 0, 0)),
    ]
    out_specs = [pl.BlockSpec((1, n_tok, v_w), lambda b: (b, 0, 0))]
    out_shape = [jax.ShapeDtypeStruct((bsz, n_tok, v_w), BF16)]
    args = [z3, z3, z3, z3, gate_i, gate_f, bias_i, bias_f, norm_w]
    if has_state:
        in_specs += state_specs
        args += list(states)
    else:
        out_specs += state_specs
        out_shape += [
            jax.ShapeDtypeStruct((bsz, GATE_ROWS, M_V, M_QK), F32),
            jax.ShapeDtypeStruct((bsz, GATE_ROWS, M_QK), F32),
            jax.ShapeDtypeStruct((bsz, GATE_ROWS, 1), F32),
        ]
    return pl.pallas_call(
        functools.partial(_mlstm_kernel, n_tok=n_tok, has_state=has_state),
        grid=(bsz,),
        in_specs=in_specs,
        out_specs=out_specs,
        out_shape=out_shape,
        compiler_params=_cparams("parallel"),
        name="mlstm",
    )(*args)


def _group_rms(x, w_row, ones_bd):
    x2 = x * x
    hi = x2.astype(BF16)
    lo = (x2 - hi.astype(F32)).astype(BF16)
    ss = jnp.dot(hi, ones_bd, preferred_element_type=F32) + jnp.dot(lo, ones_bd, preferred_element_type=F32)
    return x * lax.rsqrt(ss * (1.0 / A_HD) + NORM_EPS) * w_row


def _rope(x, cos, sin_signed, lane_lo):
    n = x.shape[-1]
    partner = jnp.where(lane_lo, pltpu.roll(x, n - 16, 1), pltpu.roll(x, 16, 1))
    return x * cos + partner * sin_signed


def _attend_block(q, sink_col, k_loc, v_loc, kinds, masks, k_ctx=None, v_ctx=None):
    s_loc = lax.dot_general(q, k_loc, NT_DIMS, preferred_element_type=F32)
    tiles = []
    for j, kind in enumerate(kinds):
        t = s_loc[:, j * WINDOW:(j + 1) * WINDOW]
        tiles.append(t if kind == "cur" else jnp.where(masks[kind], t, -jnp.inf))
    n_loc = len(tiles)
    if k_ctx is not None:
        s_ctx = lax.dot_general(q, k_ctx, NT_DIMS, preferred_element_type=F32)
        tiles += [s_ctx[:, j * WINDOW:(j + 1) * WINDOW] for j in range(k_ctx.shape[0] // WINDOW)]
    tile_max = tiles[0]
    for t in tiles[1:]:
        tile_max = jnp.maximum(tile_max, t)
    m = jnp.maximum(jnp.max(tile_max, axis=1, keepdims=True), sink_col)
    p = [jnp.exp2(t - m).astype(BF16) for t in tiles]
    acc = jnp.dot(jnp.concatenate(p[:n_loc], axis=1), v_loc, preferred_element_type=F32)
    if k_ctx is not None:
        acc = acc + jnp.dot(jnp.concatenate(p[n_loc:], axis=1), v_ctx, preferred_element_type=F32)
    den = acc[:, A_HD:A_HD + 1] + jnp.exp2(sink_col - m)
    return acc[:, :A_HD] / den


def _attn_kernel(*refs, n_tok, latent):
    if latent:
        (sink_ref, q_ref, k_ref, v_ref, qw_ref, kw_ref, cos_ref, sin_ref, ck_ref, cv_ref,
         o_ref, qh_ref, kg_ref, va_ref) = refs
    else:
        (sink_ref, q_ref, k_ref, v_ref, qw_ref, kw_ref,
         o_ref, ko_ref, vo_ref) = refs

    lane = lax.broadcasted_iota(jnp.int32, (1, A_KV_W), 1)
    lane_lo = (lane % 32) < 16
    bd_r = lax.broadcasted_iota(jnp.int32, (A_KV_W, A_KV_W), 0) // A_HD
    bd_c = lax.broadcasted_iota(jnp.int32, (A_KV_W, A_KV_W), 1) // A_HD
    ones_bd = jnp.where(bd_r == bd_c, 1.0, 0.0).astype(BF16)

    k_all = _group_rms(k_ref[0].astype(F32), kw_ref[...], ones_bd)
    v_all = v_ref[0]
    if latent:
        k_all = _rope(k_all, cos_ref[...], sin_ref[...], lane_lo)
    else:
        ko_ref[0] = k_all
        vo_ref[0] = v_all.astype(F32)

    q_blk = WINDOW if latent else n_tok
    m_rows = A_GROUP * q_blk
    head_of_row = lax.broadcasted_iota(jnp.int32, (m_rows, 1), 0) // q_blk
    r_in = lax.broadcasted_iota(jnp.int32, (m_rows, WINDOW), 0) % WINDOW
    c_in = lax.broadcasted_iota(jnp.int32, (m_rows, WINDOW), 1)
    masks = {"prev": c_in >= r_in, "next": c_in <= r_in}
    ones_v = jnp.ones((n_tok, A_HD), BF16)

    def sink_column(g):
        col = jnp.zeros((m_rows, 1), F32)
        for a in range(A_GROUP):
            col = jnp.where(head_of_row == a, sink_ref[A_GROUP * g + a] * LOG2E, col)
        return col

    def unstack(o):
        return jnp.concatenate([o[a * q_blk:(a + 1) * q_blk, :] for a in range(A_GROUP)], axis=1).astype(BF16)

    for g in range(A_KV):
        g_cols = slice(A_KV_W * g, A_KV_W * (g + 1))
        h_cols = slice(A_HD * g, A_HD * (g + 1))
        qg = _group_rms(q_ref[0, :, g_cols].astype(F32), qw_ref[...], ones_bd)
        if latent:
            qg = _rope(qg, cos_ref[...], sin_ref[...], lane_lo)
        qg = qg * (A_HD ** -0.5 * LOG2E)
        kg = k_all[:, h_cols].astype(BF16)
        va = jnp.concatenate([v_all[:, h_cols], ones_v], axis=1)
        sink_col = sink_column(g)
        if not latent:
            q_stack = jnp.concatenate([qg[:, A_HD * a:A_HD * (a + 1)] for a in range(A_GROUP)], axis=0)
            o = _attend_block(q_stack.astype(BF16), sink_col, kg, va, ["cur"] * (n_tok // WINDOW), masks)
            o_ref[0, :, g_cols] = unstack(o)
            continue

        for a in range(A_GROUP):
            qh_ref[a] = qg[:, A_HD * a:A_HD * (a + 1)].astype(BF16)
        kg_ref[...] = kg
        va_ref[...] = va
        n_ctx = ck_ref.shape[1]
        ckg = ck_ref[0][:, h_cols].astype(BF16)
        cva = jnp.concatenate([cv_ref[0][:, h_cols].astype(BF16), jnp.ones((n_ctx, A_HD), BF16)], axis=1)
        nb = n_tok // WINDOW

        def q_block(i, kinds, k_start, g_cols=g_cols, sink_col=sink_col, ckg=ckg, cva=cva, qg=qg, kg=kg, va=va):
            n_keys = len(kinds) * WINDOW
            if isinstance(i, int):
                r0 = i * WINDOW
                q_stack = jnp.concatenate([qg[r0:r0 + WINDOW, A_HD * a:A_HD * (a + 1)] for a in range(A_GROUP)],
                                          axis=0).astype(BF16)
                k_loc = kg[k_start:k_start + n_keys]
                v_loc = va[k_start:k_start + n_keys]
            else:
                r0 = pl.multiple_of(i * WINDOW, WINDOW)
                k_start = pl.multiple_of(k_start, WINDOW)
                q_stack = jnp.concatenate([qh_ref[a, pl.ds(r0, WINDOW), :] for a in range(A_GROUP)], axis=0)
                k_loc = kg_ref[pl.ds(k_start, n_keys), :]
                v_loc = va_ref[pl.ds(k_start, n_keys), :]
            o = _attend_block(q_stack, sink_col, k_loc, v_loc, kinds, masks, ckg, cva)
            o_ref[0, pl.ds(r0, WINDOW), g_cols] = unstack(o)

        def interior(i, carry, q_block=q_block):
            q_block(i, ["prev", "cur", "next"], (i - 1) * WINDOW)
            return carry

        lax.fori_loop(1, nb - 1, interior, 0, unroll=3)
        q_block(0, ["cur", "next"], 0)
        q_block(nb - 1, ["prev", "cur"], (nb - 2) * WINDOW)


def _attention(z3, kv3, sink, q_norm_w, k_norm_w, rope_tabs, cache):
    bsz, n_tok, _ = z3.shape
    latent = cache is not None
    in_specs = [
        pl.BlockSpec(memory_space=pltpu.SMEM),
        pl.BlockSpec((1, n_tok, A_Q_W), lambda b: (b, 0, Z_AQ // A_Q_W)),
        pl.BlockSpec((1, n_tok, A_KV_W), lambda b: (b, 0, 0)),
        pl.BlockSpec((1, n_tok, A_KV_W), lambda b: (b, 0, 1)),
        pl.BlockSpec((1, A_KV_W), lambda b: (0, 0)),
        pl.BlockSpec((1, A_KV_W), lambda b: (0, 0)),
    ]
    args = [sink, z3, kv3, kv3, q_norm_w, k_norm_w]
    out_specs = [pl.BlockSpec((1, n_tok, A_Q_W), lambda b: (b, 0, 0))]
    out_shape = [jax.ShapeDtypeStruct((bsz, n_tok, A_Q_W), BF16)]
    scratch = []
    if latent:
        assert n_tok // WINDOW >= 3
        n_ctx = cache[0].shape[1]
        in_specs += [
            pl.BlockSpec((n_tok, A_KV_W), lambda b: (0, 0)),
            pl.BlockSpec((n_tok, A_KV_W), lambda b: (0, 0)),
            pl.BlockSpec((1, n_ctx, A_KV_W), lambda b: (b, 0, 0)),
            pl.BlockSpec((1, n_ctx, A_KV_W), lambda b: (b, 0, 0)),
        ]
        args += [rope_tabs[0], rope_tabs[1], cache[0], cache[1]]
        scratch = [
            pltpu.VMEM((A_GROUP, n_tok, A_HD), BF16),
            pltpu.VMEM((n_tok, A_HD), BF16),
            pltpu.VMEM((n_tok, 2 * A_HD), BF16),
        ]
    else:
        out_specs += [pl.BlockSpec((1, n_tok, A_KV_W), lambda b: (b, 0, 0))] * 2
        out_shape += [jax.ShapeDtypeStruct((bsz, n_tok, A_KV_W), F32)] * 2
    return pl.pallas_call(
        functools.partial(_attn_kernel, n_tok=n_tok, latent=latent),
        grid=(bsz,),
        in_specs=in_specs,
        out_specs=out_specs,
        out_shape=out_shape,
        scratch_shapes=scratch,
        compiler_params=_cparams("parallel"),
        name="attention",
    )(*args)


def _merge_kernel(x_ref, mod_ref, hm_ref, ha_ref, gm_ref, ga_ref, wpm_ref, wpa_ref, wo_ref, o_ref):
    pm = jnp.dot(hm_ref[...], wpm_ref[...], preferred_element_type=F32)
    pa = jnp.dot(ha_ref[...], wpa_ref[...], preferred_element_type=F32)
    u = _sigmoid(gm_ref[...].astype(F32)) * pm + _sigmoid(ga_ref[...].astype(F32)) * pa
    mix = jnp.dot(u.astype(BF16), wo_ref[...], preferred_element_type=F32)
    o_ref[...] = x_ref[...] + mod_ref[0, 5:6, :] * mix


def _merge(x, mod3, row_of_tile, hm, ha, z, w_proj_m, w_proj_a, w_out, tm):
    n_tok, d = x.shape
    resident = functools.partial(pl.BlockSpec, pipeline_mode=pl.Buffered(1))
    return pl.pallas_call(
        _merge_kernel,
        grid=(n_tok // tm,),
        in_specs=[
            pl.BlockSpec((tm, d), lambda i: (i, 0)),
            pl.BlockSpec((1, N_MOD, d), lambda i: (row_of_tile(i, tm), 0, 0)),
            pl.BlockSpec((tm, M_HEADS * M_V), lambda i: (i, 0)),
            pl.BlockSpec((tm, A_Q_W), lambda i: (i, 0)),
            pl.BlockSpec((tm, d), lambda i: (i, Z_GM // D_MODEL)),
            pl.BlockSpec((tm, d), lambda i: (i, Z_GA // D_MODEL)),
            resident((M_HEADS * M_V, d), lambda i: (0, 0)),
            resident((A_Q_W, d), lambda i: (0, 0)),
            resident((d, d), lambda i: (0, 0)),
        ],
        out_specs=pl.BlockSpec((tm, d), lambda i: (i, 0)),
        out_shape=jax.ShapeDtypeStruct((n_tok, d), F32),
        compiler_params=_cparams("parallel"),
        name="merge",
    )(x, mod3, hm, ha, z, z, w_proj_m, w_proj_a, w_out)


def _rope_tables(n_tok):
    nf = A_HD // 4
    inv = ROPE_THETA ** (-jnp.arange(nf, dtype=F32) / nf)
    tok = jnp.arange(n_tok)
    pos = jnp.stack([tok // GRID_W, tok % GRID_W], axis=1).astype(F32)
    ang = pos[:, :, None] * inv
    cos = jnp.cos(ang)
    sin = jnp.sin(ang)
    cos_h = jnp.concatenate([cos, cos], axis=-1).reshape(n_tok, A_HD)
    sin_h = jnp.concatenate([-sin, sin], axis=-1).reshape(n_tok, A_HD)
    return jnp.tile(cos_h, (1, A_KV)), jnp.tile(sin_h, (1, A_KV))


def _trunk(x, mod3, row_of_tile, wts, states, rope_tabs, cache, tm_ffn, tm_proj, tm_merge):
    bsz, n_tok, d = x.shape
    x2 = x.reshape(bsz * n_tok, d)
    x2 = _ffn(x2, mod3, row_of_tile, wts["norm1"], wts["wi1"], wts["wo1"], 0, tm_ffn)
    z, gates, kv = _mixer_proj(x2, mod3, row_of_tile, wts["norm_mix"], wts["w_main"], wts["w_aux"], tm_proj)
    z3 = z.reshape(bsz, n_tok, Z_COLS)
    g = jnp.transpose(gates[:, :M_GATES].reshape(bsz, n_tok, 2, 2, M_HEADS), (0, 2, 3, 4, 1))
    gate_i = g[:, :, 0].reshape(bsz, GATE_ROWS, n_tok)
    gate_f = g[:, :, 1].reshape(bsz, GATE_ROWS, n_tok)
    m_out = _mlstm(z3, gate_i, gate_f, wts["bias_i"], wts["bias_f"], wts["mlstm_norm"], states)
    a_out = _attention(z3, kv.reshape(bsz, n_tok, 2 * A_KV_W), wts["sink"], wts["q_norm"], wts["k_norm"],
                       rope_tabs, cache)
    hm = m_out[0].reshape(bsz * n_tok, M_HEADS * M_V)
    ha = a_out[0].reshape(bsz * n_tok, A_Q_W)
    x2 = _merge(x2, mod3, row_of_tile, hm, ha, z, wts["w_proj_m"], wts["w_proj_a"], wts["w_out"], tm_merge)
    x2 = _ffn(x2, mod3, row_of_tile, wts["norm2"], wts["wi2"], wts["wo2"], 6, tm_ffn)
    return x2.reshape(bsz, n_tok, d), m_out[1:], a_out[1:]


def kernel(x_prompt, x_sample, cache_attn_k, cache_attn_v, state_mlstm_C, state_mlstm_n, state_mlstm_m, c, c_ctx, ada_w, ada_b, norm_ffn1_w, ffn1_wi, ffn1_wo, norm_mix_w, w_in, mlstm_gate_b, mlstm_norm_w, attn_q_norm_w, attn_k_norm_w, attn_sink, w_proj_m, w_proj_a, w_out, norm_ffn2_w, ffn2_wi, ffn2_wo):
    bp, tp, d = x_prompt.shape
    bs, ts, _ = x_sample.shape
    n_ctx = cache_attn_k.shape[2]
    l = 0

    n_rows = 16
    cond = jnp.concatenate([c_ctx[None, :], c, jnp.zeros((n_rows - 1 - bs, d), F32)], axis=0)
    mod3 = _modulation(cond, ada_w[l], ada_b[l][None, :]).reshape(n_rows, N_MOD, d)

    w_main, w_aux = _regroup_w_in(w_in[l])
    gate_b = mlstm_gate_b[l].reshape(2, 2, M_HEADS)
    wts = dict(
        norm1=norm_ffn1_w[l][None, :], wi1=ffn1_wi[l].astype(BF16), wo1=ffn1_wo[l].astype(BF16),
        norm_mix=norm_mix_w[l][None, :], w_main=w_main, w_aux=w_aux,
        bias_i=gate_b[:, 0].reshape(GATE_ROWS, 1), bias_f=gate_b[:, 1].reshape(GATE_ROWS, 1),
        mlstm_norm=mlstm_norm_w[l][None, :],
        sink=attn_sink[l], q_norm=jnp.tile(attn_q_norm_w[l], A_KV)[None, :],
        k_norm=jnp.tile(attn_k_norm_w[l], A_KV)[None, :],
        w_proj_m=w_proj_m[l].astype(BF16), w_proj_a=w_proj_a[l].astype(BF16), w_out=w_out[l].astype(BF16),
        norm2=norm_ffn2_w[l][None, :], wi2=ffn2_wi[l].astype(BF16), wo2=ffn2_wo[l].astype(BF16),
    )

    y_prompt, (c_new, n_new, m_new), (k_new, v_new) = _trunk(
        x_prompt, mod3, lambda i, tm: 0, wts, None, None, None, tm_ffn=512, tm_proj=1024, tm_merge=256)
    new_attn_k = k_new.reshape(bp, 1, tp, A_KV, A_HD)
    new_attn_v = v_new.reshape(bp, 1, tp, A_KV, A_HD)
    new_c = c_new.reshape(bp, 1, 2, M_HEADS, M_V, M_QK)
    new_n = n_new.reshape(bp, 1, 2, M_HEADS, M_QK)
    new_m = m_new.reshape(bp, 1, 2, M_HEADS)

    states = (state_mlstm_C[:, l].reshape(bs, GATE_ROWS, M_V, M_QK),
              state_mlstm_n[:, l].reshape(bs, GATE_ROWS, M_QK),
              state_mlstm_m[:, l].reshape(bs, GATE_ROWS, 1))
    cache = (cache_attn_k[:, l].reshape(bs, n_ctx, A_KV_W), cache_attn_v[:, l].reshape(bs, n_ctx, A_KV_W))
    y_sample, _, _ = _trunk(
        x_sample, mod3, lambda i, tm: 1 + (i * tm) // ts, wts, states, _rope_tables(ts), cache,
        tm_ffn=512, tm_proj=1024, tm_merge=256)

    return (y_prompt, y_sample, new_attn_k, new_attn_v, new_c, new_n, new_m)
```

```python
import functools

import jax
import jax.numpy as jnp
from jax import lax
from jax.experimental import pallas as pl
from jax.experimental.pallas import tpu as pltpu

F32 = jnp.float32
BF16 = jnp.bfloat16
LOG2E = 1.4426950408889634

D_MODEL = 2048
D_FF = 5632
N_MOD = 9
NORM_EPS = 1e-6
GRID_W = 64
ROPE_THETA = 10000.0
M_HEADS = 4
M_QK = 128
M_V = 256
M_GATES = 4 * M_HEADS
A_HEADS = 16
A_KV = 4
A_GROUP = A_HEADS // A_KV
A_HD = 64
A_Q_W = A_HEADS * A_HD
A_KV_W = A_KV * A_HD
WINDOW = 128

Z_GM, Z_GA, Z_MV, Z_MO, Z_AQ, Z_MQ, Z_MK = 0, 2048, 4096, 5120, 6144, 7168, 7680
Z_COLS = 8192
GATE_LANES = 128
AUX_COLS = GATE_LANES + 2 * A_KV_W
GATE_ROWS = 2 * M_HEADS

FFN_TF = 512
PROJ_TN = 1024
MLSTM_BLOCK = 256
XPOSE_ROWS = 128

VMEM_LIMIT = 56 * 1024 * 1024
NT_DIMS = (((1,), (1,)), ((), ()))
TN_DIMS = (((0,), (0,)), ((), ()))


def _cparams(*sem):
    return pltpu.CompilerParams(dimension_semantics=sem, vmem_limit_bytes=VMEM_LIMIT)


def _sigmoid(x):
    return 1.0 / (1.0 + jnp.exp(-x))


def _norm_modulate(x, norm_w, shift, scale):
    ms = jnp.mean(x * x, axis=-1, keepdims=True)
    y = x * lax.rsqrt(ms + NORM_EPS) * norm_w
    return y * (1.0 + scale) + shift


def _mod_kernel(c_ref, w_ref, b_ref, o_ref):
    c = c_ref[...]
    s = (c * _sigmoid(c)).astype(BF16)
    o_ref[...] = jnp.dot(s, w_ref[...].astype(BF16), preferred_element_type=F32) + b_ref[...]


def _modulation(cond, ada_w, ada_b, tn=1024):
    rows, d = cond.shape
    n = ada_w.shape[1]
    return pl.pallas_call(
        _mod_kernel,
        grid=(n // tn,),
        in_specs=[
            pl.BlockSpec((rows, d), lambda j: (0, 0)),
            pl.BlockSpec((d, tn), lambda j: (0, j)),
            pl.BlockSpec((1, tn), lambda j: (0, j)),
        ],
        out_specs=pl.BlockSpec((rows, tn), lambda j: (0, j)),
        out_shape=jax.ShapeDtypeStruct((rows, n), F32),
        compiler_params=_cparams("arbitrary"),
        name="modulation",
    )(cond, ada_w, ada_b)


def _ffn_kernel(x_ref, mod_ref, nw_ref, wg_ref, wu_ref, wo_ref, o_ref, h_ref, *, k_shift):
    j = pl.program_id(1)

    @pl.when(j == 0)
    def _():
        h = _norm_modulate(x_ref[...], nw_ref[...], mod_ref[0, k_shift:k_shift + 1, :],
                           mod_ref[0, k_shift + 1:k_shift + 2, :])
        h_ref[...] = h.astype(BF16)
        o_ref[...] = x_ref[...]

    h = h_ref[...]
    g = jnp.dot(h, wg_ref[...], preferred_element_type=F32)
    u = jnp.dot(h, wu_ref[...], preferred_element_type=F32)
    a = (g * _sigmoid(g) * u).astype(BF16)
    half_gate = 0.5 * mod_ref[0, k_shift + 2:k_shift + 3, :]
    o_ref[...] += half_gate * jnp.dot(a, wo_ref[...], preferred_element_type=F32)


def _ffn(x, mod3, row_of_tile, norm_w, wi, wo, k_shift, tm):
    n_tok, d = x.shape
    tf = FFN_TF
    nf = D_FF // tf
    return pl.pallas_call(
        functools.partial(_ffn_kernel, k_shift=k_shift),
        grid=(n_tok // tm, nf),
        in_specs=[
            pl.BlockSpec((tm, d), lambda i, j: (i, 0)),
            pl.BlockSpec((1, N_MOD, d), lambda i, j: (row_of_tile(i, tm), 0, 0)),
            pl.BlockSpec((1, d), lambda i, j: (0, 0)),
            pl.BlockSpec((d, tf), lambda i, j: (0, j)),
            pl.BlockSpec((d, tf), lambda i, j: (0, j + nf)),
            pl.BlockSpec((tf, d), lambda i, j: (j, 0)),
        ],
        out_specs=pl.BlockSpec((tm, d), lambda i, j: (i, 0)),
        out_shape=jax.ShapeDtypeStruct((n_tok, d), F32),
        scratch_shapes=[pltpu.VMEM((tm, d), BF16)],
        compiler_params=_cparams("parallel", "arbitrary"),
        name="ffn",
    )(x, mod3, norm_w, wi, wi, wo)


W_IN_SEGS = (("mq", 512), ("mk", 512), ("mv", 1024), ("mo", 1024), ("mg", M_GATES), ("aq", 1024),
             ("ak", 256), ("av", 256), ("gm", 2048), ("ga", 2048))
W_IN_OFF = {}
for _name, _size in W_IN_SEGS:
    W_IN_OFF[_name] = (sum(s for _, s in W_IN_SEGS[:len(W_IN_OFF)]), _size)
IN_COLS = sum(s for _, s in W_IN_SEGS)
MAIN_ORDER = ("gm", "ga", "mv", "mo", "aq", "mq", "mk")
REGROUP_ROWS = 128


def _regroup_kernel(w_ref, main_ref, aux_ref):
    w = w_ref[0]
    t0 = W_IN_OFF["aq"][0]
    tail = w[:, t0:]

    def seg(name):
        start, size = W_IN_OFF[name]
        return (tail[:, start - t0:start - t0 + size] if start >= t0 else w[:, start:start + size]).astype(BF16)

    col = 0
    for name in MAIN_ORDER:
        size = W_IN_OFF[name][1]
        main_ref[:, col:col + size] = seg(name)
        col += size
    g0 = W_IN_OFF["mg"][0]
    lane = lax.broadcasted_iota(jnp.int32, (1, GATE_LANES), 1)
    aux_ref[:, :GATE_LANES] = jnp.where(lane < M_GATES, w[:, g0:g0 + GATE_LANES], 0.0).astype(BF16)
    aux_ref[:, GATE_LANES:GATE_LANES + A_KV_W] = seg("ak")
    aux_ref[:, GATE_LANES + A_KV_W:] = seg("av")


def _regroup_w_in(w, l):
    _, d, n = w.shape
    assert n == IN_COLS and W_IN_OFF["mg"][0] % GATE_LANES == 0
    return pl.pallas_call(
        _regroup_kernel,
        grid=(d // REGROUP_ROWS,),
        in_specs=[pl.BlockSpec((1, REGROUP_ROWS, n), lambda i: (l, i, 0))],
        out_specs=[
            pl.BlockSpec((REGROUP_ROWS, Z_COLS), lambda i: (i, 0)),
            pl.BlockSpec((REGROUP_ROWS, AUX_COLS), lambda i: (i, 0)),
        ],
        out_shape=[jax.ShapeDtypeStruct((d, Z_COLS), BF16), jax.ShapeDtypeStruct((d, AUX_COLS), BF16)],
        compiler_params=_cparams("parallel"),
        name="regroup_w_in",
    )(w)


def _proj_kernel(x_ref, mod_ref, nw_ref, w_ref, waux_ref, z_ref, g_ref, kv_ref, h_ref):
    j = pl.program_id(1)

    @pl.when(j == 0)
    def _():
        h = _norm_modulate(x_ref[...], nw_ref[...], mod_ref[0, 3:4, :], mod_ref[0, 4:5, :]).astype(BF16)
        h_ref[...] = h
        aux = jnp.dot(h, waux_ref[...], preferred_element_type=F32)
        g_ref[...] = aux[:, :GATE_LANES]
        kv_ref[...] = aux[:, GATE_LANES:].astype(BF16)

    half = z_ref.shape[1] // 2
    h = h_ref[...]
    z_ref[:, :half] = jnp.dot(h, w_ref[:, :half], preferred_element_type=F32).astype(BF16)
    z_ref[:, half:] = jnp.dot(h, w_ref[:, half:], preferred_element_type=F32).astype(BF16)


def _mixer_proj(x, mod3, row_of_tile, norm_w, w_main, w_aux, tm):
    n_tok, d = x.shape
    tn = PROJ_TN
    once = functools.partial(pl.BlockSpec, pipeline_mode=pl.Buffered(1))
    return pl.pallas_call(
        _proj_kernel,
        grid=(n_tok // tm, Z_COLS // tn),
        in_specs=[
            pl.BlockSpec((tm, d), lambda i, j: (i, 0)),
            pl.BlockSpec((1, N_MOD, d), lambda i, j: (row_of_tile(i, tm), 0, 0)),
            pl.BlockSpec((1, d), lambda i, j: (0, 0)),
            pl.BlockSpec((d, tn), lambda i, j: (0, j)),
            once((d, AUX_COLS), lambda i, j: (0, 0)),
        ],
        out_specs=[
            pl.BlockSpec((tm, tn), lambda i, j: (i, j)),
            pl.BlockSpec((tm, GATE_LANES), lambda i, j: (i, 0)),
            pl.BlockSpec((tm, 2 * A_KV_W), lambda i, j: (i, 0)),
        ],
        out_shape=[
            jax.ShapeDtypeStruct((n_tok, Z_COLS), BF16),
            jax.ShapeDtypeStruct((n_tok, GATE_LANES), F32),
            jax.ShapeDtypeStruct((n_tok, 2 * A_KV_W), BF16),
        ],
        scratch_shapes=[pltpu.VMEM((tm, d), BF16)],
        compiler_params=_cparams("parallel", "arbitrary"),
        name="mixer_proj",
    )(x, mod3, norm_w, w_main, w_aux)


def _scan_lanes(x, op, fill, reverse):
    n = x.shape[-1]
    lane = lax.broadcasted_iota(jnp.int32, x.shape, 1)
    s = 1
    while s < n:
        if reverse:
            x = op(x, jnp.where(lane < n - s, pltpu.roll(x, n - s, 1), fill))
        else:
            x = op(x, jnp.where(lane >= s, pltpu.roll(x, s, 1), fill))
        s *= 2
    return x


def _dir_scan(x, op, fill, fwd_rows):
    return jnp.where(fwd_rows, _scan_lanes(x, op, fill, False), _scan_lanes(x, op, fill, True))


def _log_sigmoid(x):
    return jnp.minimum(x, 0.0) - jnp.log1p(jnp.exp(-jnp.abs(x)))


def _per_batch_kernel(*refs, body, batched, per_step):
    for bb in range(per_step):
        body(*[r.at[pl.ds(bb, 1)] if is_b else r for r, is_b in zip(refs, batched)])


def _mlstm_kernel(*refs, n_tok, has_state):
    if has_state:
        (q_ref, k_ref, v_ref, og_ref, gi_ref, gf_ref, bi_ref, bf_ref, nw_ref, c0_ref, n0_ref, m0_ref,
         o_ref) = refs
    else:
        (q_ref, k_ref, v_ref, og_ref, gi_ref, gf_ref, bi_ref, bf_ref, nw_ref,
         o_ref, c_out, n_out, m_out) = refs
    blk = min(MLSTM_BLOCK, n_tok)
    nq = n_tok // blk

    fwd_rows = lax.broadcasted_iota(jnp.int32, (GATE_ROWS, 1), 0) < M_HEADS
    ig = gi_ref[0] + bi_ref[...]
    cum_f = _dir_scan(_log_sigmoid(gf_ref[0] + bf_ref[...]), jnp.add, 0.0, fwd_rows)
    a = ig - cum_f
    m0 = m0_ref[0] if has_state else jnp.zeros((GATE_ROWS, 1), F32)
    mx = jnp.maximum(_dir_scan(a, jnp.maximum, -jnp.inf, fwd_rows), m0)
    stats = [-mx * LOG2E, -(cum_f + mx) * LOG2E]
    a2 = a * LOG2E
    m0_2 = m0 * LOG2E
    if not has_state:
        mx_end = jnp.where(fwd_rows, mx[:, n_tok - 1:n_tok], mx[:, 0:1])
        cum_end = jnp.where(fwd_rows, cum_f[:, n_tok - 1:n_tok], cum_f[:, 0:1])
        stats.append(jnp.exp(a - mx_end))
        m_out[0] = cum_end + mx_end
    pad = jnp.zeros((XPOSE_ROWS - GATE_ROWS * len(stats), n_tok), F32)
    cols = jnp.concatenate(stats + [pad], axis=0).T

    row_i = lax.broadcasted_iota(jnp.int32, (blk, blk), 0)
    col_i = lax.broadcasted_iota(jnp.int32, (blk, blk), 1)
    visible = (col_i <= row_i, col_i >= row_i)

    for h in range(M_HEADS):
        qk_cols = slice(h * M_QK, (h + 1) * M_QK)
        v_cols = slice(h * M_V, (h + 1) * M_V)
        qf = q_ref[0, :, qk_cols].astype(F32) * (M_QK ** -0.5)
        q = qf.astype(BF16)
        k = k_ref[0, :, qk_cols]
        v = v_ref[0, :, v_cols]
        for qi in range(nq):
            rows = slice(qi * blk, (qi + 1) * blk)
            h_sum = None
            for d in range(2):
                r = M_HEADS * d + h
                lo, hi = (0, (qi + 1) * blk) if d == 0 else (qi * blk, n_tok)
                dg = hi - lo - blk if d == 0 else 0
                u_col = cols[rows, r:r + 1]
                neg_mt_col = cols[rows, GATE_ROWS + r:GATE_ROWS + r + 1]
                dm = u_col + a2[r:r + 1, lo:hi]
                parts = [dm[:, :dg]] if dg > 0 else []
                parts.append(jnp.where(visible[d], dm[:, dg:dg + blk], -jnp.inf))
                if dg + blk < hi - lo:
                    parts.append(dm[:, dg + blk:])
                dm = jnp.concatenate(parts, axis=1) if len(parts) > 1 else parts[0]
                s = lax.dot_general(q[rows], k[lo:hi], NT_DIMS, preferred_element_type=F32) * jnp.exp2(dm)
                num = jnp.dot(s.astype(BF16), v[lo:hi], preferred_element_type=F32)
                den = jnp.sum(s, axis=1, keepdims=True)
                if has_state:
                    decay = jnp.exp2(u_col + m0_2[r:r + 1, :])
                    num = num + decay * lax.dot_general(q[rows], c0_ref[0, r].astype(BF16), NT_DIMS,
                                                        preferred_element_type=F32)
                    den = den + decay * jnp.sum(qf[rows] * n0_ref[0, r:r + 1, :], axis=1, keepdims=True)
                h_dir = num / jnp.maximum(jnp.abs(den), jnp.exp2(neg_mt_col))
                h_sum = h_dir if h_sum is None else h_sum + h_dir
            hn = h_sum * lax.rsqrt(jnp.mean(h_sum * h_sum, axis=-1, keepdims=True) + NORM_EPS) * nw_ref[:, v_cols]
            o_ref[0, rows, v_cols] = (hn * _sigmoid(og_ref[0, rows, v_cols].astype(F32))).astype(BF16)
        if not has_state:
            kf = k.astype(F32)
            vf = v.astype(F32)
            for d in range(2):
                r = M_HEADS * d + h
                wk_col = cols[:, 2 * GATE_ROWS + r:2 * GATE_ROWS + r + 1]
                c_out[0, r] = lax.dot_general((vf * wk_col).astype(BF16), k, TN_DIMS, preferred_element_type=F32)
                n_out[0, r:r + 1, :] = jnp.sum(kf * wk_col, axis=0, keepdims=True)


def _mlstm(z3, gate_i, gate_f, bias_i, bias_f, norm_w, states, per_step):
    bsz, n_tok, _ = z3.shape
    has_state = states is not None
    qk_w, v_w = M_HEADS * M_QK, M_HEADS * M_V
    nb = per_step
    in_specs = [
        pl.BlockSpec((nb, n_tok, qk_w), lambda b: (b, 0, Z_MQ // qk_w)),
        pl.BlockSpec((nb, n_tok, qk_w), lambda b: (b, 0, Z_MK // qk_w)),
        pl.BlockSpec((nb, n_tok, v_w), lambda b: (b, 0, Z_MV // v_w)),
        pl.BlockSpec((nb, n_tok, v_w), lambda b: (b, 0, Z_MO // v_w)),
        pl.BlockSpec((nb, GATE_ROWS, n_tok), lambda b: (b, 0, 0)),
        pl.BlockSpec((nb, GATE_ROWS, n_tok), lambda b: (b, 0, 0)),
        pl.BlockSpec((GATE_ROWS, 1), lambda b: (0, 0)),
        pl.BlockSpec((GATE_ROWS, 1), lambda b: (0, 0)),
        pl.BlockSpec((1, v_w), lambda b: (0, 0)),
    ]
    batched = [True] * 6 + [False] * 3 + [True] * 4
    state_specs = [
        pl.BlockSpec((nb, GATE_ROWS, M_V, M_QK), lambda b: (b, 0, 0, 0)),
        pl.BlockSpec((nb, GATE_ROWS, M_QK), lambda b: (b, 0, 0)),
        pl.BlockSpec((nb, GATE_ROWS, 1), lambda b: (b, 0, 0)),
    ]
    out_specs = [pl.BlockSpec((nb, n_tok, v_w), lambda b: (b, 0, 0))]
    out_shape = [jax.ShapeDtypeStruct((bsz, n_tok, v_w), BF16)]
    args = [z3, z3, z3, z3, gate_i, gate_f, bias_i, bias_f, norm_w]
    if has_state:
        in_specs += state_specs
        args += list(states)
    else:
        out_specs += state_specs
        out_shape += [
            jax.ShapeDtypeStruct((bsz, GATE_ROWS, M_V, M_QK), F32),
            jax.ShapeDtypeStruct((bsz, GATE_ROWS, M_QK), F32),
            jax.ShapeDtypeStruct((bsz, GATE_ROWS, 1), F32),
        ]
    return pl.pallas_call(
        functools.partial(_per_batch_kernel, per_step=nb, batched=batched,
                          body=functools.partial(_mlstm_kernel, n_tok=n_tok, has_state=has_state)),
        grid=(bsz // nb,),
        in_specs=in_specs,
        out_specs=out_specs,
        out_shape=out_shape,
        compiler_params=_cparams("parallel"),
        name="mlstm",
    )(*args)


def _group_rms(x, w_row, ones_bd):
    x2 = x * x
    hi = x2.astype(BF16)
    lo = (x2 - hi.astype(F32)).astype(BF16)
    ss = jnp.dot(hi, ones_bd, preferred_element_type=F32) + jnp.dot(lo, ones_bd, preferred_element_type=F32)
    return x * lax.rsqrt(ss * (1.0 / A_HD) + NORM_EPS) * w_row


def _rope(x, cos, sin_signed, lane_lo):
    n = x.shape[-1]
    partner = jnp.where(lane_lo, pltpu.roll(x, n - 16, 1), pltpu.roll(x, 16, 1))
    return x * cos + partner * sin_signed


def _attend_block(q, sink_col, k_loc, v_loc, kinds, masks, k_ctx=None, v_ctx=None):
    s_loc = lax.dot_general(q, k_loc, NT_DIMS, preferred_element_type=F32)
    tiles = []
    for j, kind in enumerate(kinds):
        t = s_loc[:, j * WINDOW:(j + 1) * WINDOW]
        tiles.append(t if kind == "cur" else jnp.where(masks[kind], t, -jnp.inf))
    n_loc = len(tiles)
    if k_ctx is not None:
        s_ctx = lax.dot_general(q, k_ctx, NT_DIMS, preferred_element_type=F32)
        tiles += [s_ctx[:, j * WINDOW:(j + 1) * WINDOW] for j in range(k_ctx.shape[0] // WINDOW)]
    tile_max = tiles[0]
    for t in tiles[1:]:
        tile_max = jnp.maximum(tile_max, t)
    m = jnp.maximum(jnp.max(tile_max, axis=1, keepdims=True), sink_col)
    p = [jnp.exp2(t - m).astype(BF16) for t in tiles]
    acc = jnp.dot(jnp.concatenate(p[:n_loc], axis=1), v_loc, preferred_element_type=F32)
    if k_ctx is not None:
        acc = acc + jnp.dot(jnp.concatenate(p[n_loc:], axis=1), v_ctx, preferred_element_type=F32)
    den = acc[:, A_HD:A_HD + 1] + jnp.exp2(sink_col - m)
    return acc[:, :A_HD] / den


def _attn_kernel(*refs, n_tok, latent):
    if latent:
        (sink_ref, q_ref, k_ref, v_ref, qw_ref, kw_ref, cos_ref, sin_ref, ck_ref, cv_ref,
         o_ref, qh_ref, kg_ref, va_ref) = refs
    else:
        (sink_ref, q_ref, k_ref, v_ref, qw_ref, kw_ref,
         o_ref, ko_ref, vo_ref) = refs

    lane = lax.broadcasted_iota(jnp.int32, (1, A_KV_W), 1)
    lane_lo = (lane % 32) < 16
    bd_r = lax.broadcasted_iota(jnp.int32, (A_KV_W, A_KV_W), 0) // A_HD
    bd_c = lax.broadcasted_iota(jnp.int32, (A_KV_W, A_KV_W), 1) // A_HD
    ones_bd = jnp.where(bd_r == bd_c, 1.0, 0.0).astype(BF16)

    k_all = _group_rms(k_ref[0].astype(F32), kw_ref[...], ones_bd)
    v_all = v_ref[0]
    if latent:
        k_all = _rope(k_all, cos_ref[...], sin_ref[...], lane_lo)
    else:
        ko_ref[0] = k_all
        vo_ref[0] = v_all.astype(F32)

    q_blk = WINDOW if latent else n_tok
    m_rows = A_GROUP * q_blk
    head_of_row = lax.broadcasted_iota(jnp.int32, (m_rows, 1), 0) // q_blk
    r_in = lax.broadcasted_iota(jnp.int32, (m_rows, WINDOW), 0) % WINDOW
    c_in = lax.broadcasted_iota(jnp.int32, (m_rows, WINDOW), 1)
    masks = {"prev": c_in >= r_in, "next": c_in <= r_in}
    ones_v = jnp.ones((n_tok, A_HD), BF16)

    def sink_column(g):
        col = jnp.zeros((m_rows, 1), F32)
        for a in range(A_GROUP):
            col = jnp.where(head_of_row == a, sink_ref[A_GROUP * g + a] * LOG2E, col)
        return col

    def unstack(o):
        return jnp.concatenate([o[a * q_blk:(a + 1) * q_blk, :] for a in range(A_GROUP)], axis=1).astype(BF16)

    for g in range(A_KV):
        g_cols = slice(A_KV_W * g, A_KV_W * (g + 1))
        h_cols = slice(A_HD * g, A_HD * (g + 1))
        qg = _group_rms(q_ref[0, :, g_cols].astype(F32), qw_ref[...], ones_bd)
        if latent:
            qg = _rope(qg, cos_ref[...], sin_ref[...], lane_lo)
        qg = qg * (A_HD ** -0.5 * LOG2E)
        kg = k_all[:, h_cols].astype(BF16)
        va = jnp.concatenate([v_all[:, h_cols], ones_v], axis=1)
        sink_col = sink_column(g)
        if not latent:
            q_stack = jnp.concatenate([qg[:, A_HD * a:A_HD * (a + 1)] for a in range(A_GROUP)], axis=0)
            o = _attend_block(q_stack.astype(BF16), sink_col, kg, va, ["cur"] * (n_tok // WINDOW), masks)
            o_ref[0, :, g_cols] = unstack(o)
            continue

        for a in range(A_GROUP):
            qh_ref[a] = qg[:, A_HD * a:A_HD * (a + 1)].astype(BF16)
        kg_ref[...] = kg
        va_ref[...] = va
        n_ctx = ck_ref.shape[1]
        ckg = ck_ref[0][:, h_cols].astype(BF16)
        cva = jnp.concatenate([cv_ref[0][:, h_cols].astype(BF16), jnp.ones((n_ctx, A_HD), BF16)], axis=1)
        nb = n_tok // WINDOW

        def q_block(i, kinds, k_start, g_cols=g_cols, sink_col=sink_col, ckg=ckg, cva=cva, qg=qg, kg=kg, va=va):
            n_keys = len(kinds) * WINDOW
            if isinstance(i, int):
                r0 = i * WINDOW
                q_stack = jnp.concatenate([qg[r0:r0 + WINDOW, A_HD * a:A_HD * (a + 1)] for a in range(A_GROUP)],
                                          axis=0).astype(BF16)
                k_loc = kg[k_start:k_start + n_keys]
                v_loc = va[k_start:k_start + n_keys]
            else:
                r0 = pl.multiple_of(i * WINDOW, WINDOW)
                k_start = pl.multiple_of(k_start, WINDOW)
                q_stack = jnp.concatenate([qh_ref[a, pl.ds(r0, WINDOW), :] for a in range(A_GROUP)], axis=0)
                k_loc = kg_ref[pl.ds(k_start, n_keys), :]
                v_loc = va_ref[pl.ds(k_start, n_keys), :]
            o = _attend_block(q_stack, sink_col, k_loc, v_loc, kinds, masks, ckg, cva)
            o_ref[0, pl.ds(r0, WINDOW), g_cols] = unstack(o)

        def interior(i, carry, q_block=q_block):
            q_block(i, ["prev", "cur", "next"], (i - 1) * WINDOW)
            return carry

        lax.fori_loop(1, nb - 1, interior, 0, unroll=3)
        q_block(0, ["cur", "next"], 0)
        q_block(nb - 1, ["prev", "cur"], (nb - 2) * WINDOW)


def _attention(z3, kv3, sink, q_norm_w, k_norm_w, rope_tabs, cache, per_step):
    bsz, n_tok, _ = z3.shape
    latent = cache is not None
    nb = per_step
    in_specs = [
        pl.BlockSpec(memory_space=pltpu.SMEM),
        pl.BlockSpec((nb, n_tok, A_Q_W), lambda b: (b, 0, Z_AQ // A_Q_W)),
        pl.BlockSpec((nb, n_tok, A_KV_W), lambda b: (b, 0, 0)),
        pl.BlockSpec((nb, n_tok, A_KV_W), lambda b: (b, 0, 1)),
        pl.BlockSpec((1, A_KV_W), lambda b: (0, 0)),
        pl.BlockSpec((1, A_KV_W), lambda b: (0, 0)),
    ]
    batched = [False, True, True, True, False, False]
    args = [sink, z3, kv3, kv3, q_norm_w, k_norm_w]
    out_specs = [pl.BlockSpec((nb, n_tok, A_Q_W), lambda b: (b, 0, 0))]
    out_shape = [jax.ShapeDtypeStruct((bsz, n_tok, A_Q_W), BF16)]
    scratch = []
    if latent:
        assert n_tok // WINDOW >= 3 and nb == 1
        n_ctx = cache[0].shape[1]
        in_specs += [
            pl.BlockSpec((n_tok, A_KV_W), lambda b: (0, 0)),
            pl.BlockSpec((n_tok, A_KV_W), lambda b: (0, 0)),
            pl.BlockSpec((nb, n_ctx, A_KV_W), lambda b: (b, 0, 0)),
            pl.BlockSpec((nb, n_ctx, A_KV_W), lambda b: (b, 0, 0)),
        ]
        batched += [False, False, True, True, True, False, False, False]
        args += [rope_tabs[0], rope_tabs[1], cache[0], cache[1]]
        scratch = [
            pltpu.VMEM((A_GROUP, n_tok, A_HD), BF16),
            pltpu.VMEM((n_tok, A_HD), BF16),
            pltpu.VMEM((n_tok, 2 * A_HD), BF16),
        ]
    else:
        batched += [True, True, True]
        out_specs += [pl.BlockSpec((nb, n_tok, A_KV_W), lambda b: (b, 0, 0))] * 2
        out_shape += [jax.ShapeDtypeStruct((bsz, n_tok, A_KV_W), F32)] * 2
    return pl.pallas_call(
        functools.partial(_per_batch_kernel, per_step=nb, batched=batched,
                          body=functools.partial(_attn_kernel, n_tok=n_tok, latent=latent)),
        grid=(bsz // nb,),
        in_specs=in_specs,
        out_specs=out_specs,
        out_shape=out_shape,
        scratch_shapes=scratch,
        compiler_params=_cparams("parallel"),
        name="attention",
    )(*args)


def _merge_kernel(x_ref, mod_ref, hm_ref, ha_ref, gm_ref, ga_ref, wpm_ref, wpa_ref, wo_ref, o_ref):
    pm = jnp.dot(hm_ref[...], wpm_ref[...], preferred_element_type=F32)
    pa = jnp.dot(ha_ref[...], wpa_ref[...], preferred_element_type=F32)
    u = _sigmoid(gm_ref[...].astype(F32)) * pm + _sigmoid(ga_ref[...].astype(F32)) * pa
    mix = jnp.dot(u.astype(BF16), wo_ref[...], preferred_element_type=F32)
    o_ref[...] = x_ref[...] + mod_ref[0, 5:6, :] * mix


def _merge(x, mod3, row_of_tile, hm, ha, z, w_proj_m, w_proj_a, w_out, tm):
    n_tok, d = x.shape
    resident = functools.partial(pl.BlockSpec, pipeline_mode=pl.Buffered(1))
    return pl.pallas_call(
        _merge_kernel,
        grid=(n_tok // tm,),
        in_specs=[
            pl.BlockSpec((tm, d), lambda i: (i, 0)),
            pl.BlockSpec((1, N_MOD, d), lambda i: (row_of_tile(i, tm), 0, 0)),
            pl.BlockSpec((tm, M_HEADS * M_V), lambda i: (i, 0)),
            pl.BlockSpec((tm, A_Q_W), lambda i: (i, 0)),
            pl.BlockSpec((tm, d), lambda i: (i, Z_GM // D_MODEL)),
            pl.BlockSpec((tm, d), lambda i: (i, Z_GA // D_MODEL)),
            resident((M_HEADS * M_V, d), lambda i: (0, 0)),
            resident((A_Q_W, d), lambda i: (0, 0)),
            resident((d, d), lambda i: (0, 0)),
        ],
        out_specs=pl.BlockSpec((tm, d), lambda i: (i, 0)),
        out_shape=jax.ShapeDtypeStruct((n_tok, d), F32),
        compiler_params=_cparams("parallel"),
        name="merge",
    )(x, mod3, hm, ha, z, z, w_proj_m, w_proj_a, w_out)


def _rope_tables(n_tok):
    nf = A_HD // 4
    inv = ROPE_THETA ** (-jnp.arange(nf, dtype=F32) / nf)
    tok = jnp.arange(n_tok)
    pos = jnp.stack([tok // GRID_W, tok % GRID_W], axis=1).astype(F32)
    ang = pos[:, :, None] * inv
    cos = jnp.cos(ang)
    sin = jnp.sin(ang)
    cos_h = jnp.concatenate([cos, cos], axis=-1).reshape(n_tok, A_HD)
    sin_h = jnp.concatenate([-sin, sin], axis=-1).reshape(n_tok, A_HD)
    return jnp.tile(cos_h, (1, A_KV)), jnp.tile(sin_h, (1, A_KV))


TM_FFN = 512
TM_PROJ = 1024
TM_MERGE = 256


def _trunk(x, mod3, row_of_tile, wts, states, rope_tabs, cache):
    bsz, n_tok, d = x.shape
    per_step = 2 if (cache is None and n_tok <= 256 and bsz % 2 == 0) else 1
    x2 = x.reshape(bsz * n_tok, d)
    x2 = _ffn(x2, mod3, row_of_tile, wts["norm1"], wts["wi1"], wts["wo1"], 0, TM_FFN)
    z, gates, kv = _mixer_proj(x2, mod3, row_of_tile, wts["norm_mix"], wts["w_main"], wts["w_aux"], TM_PROJ)
    z3 = z.reshape(bsz, n_tok, Z_COLS)
    g = jnp.transpose(gates[:, :M_GATES].reshape(bsz, n_tok, 2, 2, M_HEADS), (0, 2, 3, 4, 1))
    gate_i = g[:, :, 0].reshape(bsz, GATE_ROWS, n_tok)
    gate_f = g[:, :, 1].reshape(bsz, GATE_ROWS, n_tok)
    m_out = _mlstm(z3, gate_i, gate_f, wts["bias_i"], wts["bias_f"], wts["mlstm_norm"], states, per_step)
    a_out = _attention(z3, kv.reshape(bsz, n_tok, 2 * A_KV_W), wts["sink"], wts["q_norm"], wts["k_norm"],
                       rope_tabs, cache, per_step)
    hm = m_out[0].reshape(bsz * n_tok, M_HEADS * M_V)
    ha = a_out[0].reshape(bsz * n_tok, A_Q_W)
    x2 = _merge(x2, mod3, row_of_tile, hm, ha, z, wts["w_proj_m"], wts["w_proj_a"], wts["w_out"], TM_MERGE)
    x2 = _ffn(x2, mod3, row_of_tile, wts["norm2"], wts["wi2"], wts["wo2"], 6, TM_FFN)
    return x2.reshape(bsz, n_tok, d), m_out[1:], a_out[1:]


def kernel(x_prompt, x_sample, cache_attn_k, cache_attn_v, state_mlstm_C, state_mlstm_n, state_mlstm_m, c, c_ctx, ada_w, ada_b, norm_ffn1_w, ffn1_wi, ffn1_wo, norm_mix_w, w_in, mlstm_gate_b, mlstm_norm_w, attn_q_norm_w, attn_k_norm_w, attn_sink, w_proj_m, w_proj_a, w_out, norm_ffn2_w, ffn2_wi, ffn2_wo):
    bp, tp, d = x_prompt.shape
    bs, ts, _ = x_sample.shape
    n_ctx = cache_attn_k.shape[2]
    l = 0

    n_rows = 16
    cond = jnp.concatenate([c_ctx[None, :], c, jnp.zeros((n_rows - 1 - bs, d), F32)], axis=0)
    mod3 = _modulation(cond, ada_w[l], ada_b[l][None, :]).reshape(n_rows, N_MOD, d)

    w_main, w_aux = _regroup_w_in(w_in, l)
    gate_b = mlstm_gate_b[l].reshape(2, 2, M_HEADS)
    wts = dict(
        norm1=norm_ffn1_w[l][None, :], wi1=ffn1_wi[l].astype(BF16), wo1=ffn1_wo[l].astype(BF16),
        norm_mix=norm_mix_w[l][None, :], w_main=w_main, w_aux=w_aux,
        bias_i=gate_b[:, 0].reshape(GATE_ROWS, 1), bias_f=gate_b[:, 1].reshape(GATE_ROWS, 1),
        mlstm_norm=mlstm_norm_w[l][None, :],
        sink=attn_sink[l], q_norm=jnp.tile(attn_q_norm_w[l], A_KV)[None, :],
        k_norm=jnp.tile(attn_k_norm_w[l], A_KV)[None, :],
        w_proj_m=w_proj_m[l].astype(BF16), w_proj_a=w_proj_a[l].astype(BF16), w_out=w_out[l].astype(BF16),
        norm2=norm_ffn2_w[l][None, :], wi2=ffn2_wi[l].astype(BF16), wo2=ffn2_wo[l].astype(BF16),
    )

    y_prompt, (c_new, n_new, m_new), (k_new, v_new) = _trunk(
        x_prompt, mod3, lambda i, tm: 0, wts, None, None, None)
    new_attn_k = k_new.reshape(bp, 1, tp, A_KV, A_HD)
    new_attn_v = v_new.reshape(bp, 1, tp, A_KV, A_HD)
    new_c = c_new.reshape(bp, 1, 2, M_HEADS, M_V, M_QK)
    new_n = n_new.reshape(bp, 1, 2, M_HEADS, M_QK)
    new_m = m_new.reshape(bp, 1, 2, M_HEADS)

    states = (state_mlstm_C[:, l].reshape(bs, GATE_ROWS, M_V, M_QK),
              state_mlstm_n[:, l].reshape(bs, GATE_ROWS, M_QK),
              state_mlstm_m[:, l].reshape(bs, GATE_ROWS, 1))
    cache = (cache_attn_k[:, l].reshape(bs, n_ctx, A_KV_W), cache_attn_v[:, l].reshape(bs, n_ctx, A_KV_W))
    y_sample, _, _ = _trunk(
        x_sample, mod3, lambda i, tm: 1 + (i * tm) // ts, wts, states, _rope_tables(ts), cache)

    return (y_prompt, y_sample, new_attn_k, new_attn_v, new_c, new_n, new_m)
```

```python
import functools

import jax
import jax.numpy as jnp
from jax import lax
from jax.experimental import pallas as pl
from jax.experimental.pallas import tpu as pltpu

F32 = jnp.float32
BF16 = jnp.bfloat16
LOG2E = 1.4426950408889634

D_MODEL = 2048
D_FF = 5632
N_MOD = 9
NORM_EPS = 1e-6
GRID_W = 64
ROPE_THETA = 10000.0
M_HEADS = 4
M_QK = 128
M_V = 256
M_GATES = 4 * M_HEADS
A_HEADS = 16
A_KV = 4
A_GROUP = A_HEADS // A_KV
A_HD = 64
A_Q_W = A_HEADS * A_HD
A_KV_W = A_KV * A_HD
WINDOW = 128

Z_GM, Z_GA, Z_MV, Z_MO, Z_AQ, Z_MQ, Z_MK = 0, 2048, 4096, 5120, 6144, 7168, 7680
Z_COLS = 8192
GATE_LANES = 128
AUX_COLS = GATE_LANES + 2 * A_KV_W
GATE_ROWS = 2 * M_HEADS

FFN_TF = 512
PROJ_TN = 1024
MLSTM_BLOCK = 256
XPOSE_ROWS = 128

VMEM_LIMIT = 56 * 1024 * 1024
NT_DIMS = (((1,), (1,)), ((), ()))
TN_DIMS = (((0,), (0,)), ((), ()))


def _cparams(*sem):
    return pltpu.CompilerParams(dimension_semantics=sem, vmem_limit_bytes=VMEM_LIMIT)


def _sigmoid(x):
    return 1.0 / (1.0 + jnp.exp(-x))


def _norm_modulate(x, norm_w, shift, scale):
    ms = jnp.mean(x * x, axis=-1, keepdims=True)
    y = x * lax.rsqrt(ms + NORM_EPS) * norm_w
    return y * (1.0 + scale) + shift


def _mod_kernel(c_ref, w_ref, b_ref, o_ref):
    c = c_ref[...]
    s = (c * _sigmoid(c)).astype(BF16)
    o_ref[...] = jnp.dot(s, w_ref[...].astype(BF16), preferred_element_type=F32) + b_ref[...]


def _modulation(cond, ada_w, ada_b, tn=1024):
    rows, d = cond.shape
    n = ada_w.shape[1]
    return pl.pallas_call(
        _mod_kernel,
        grid=(n // tn,),
        in_specs=[
            pl.BlockSpec((rows, d), lambda j: (0, 0)),
            pl.BlockSpec((d, tn), lambda j: (0, j)),
            pl.BlockSpec((1, tn), lambda j: (0, j)),
        ],
        out_specs=pl.BlockSpec((rows, tn), lambda j: (0, j)),
        out_shape=jax.ShapeDtypeStruct((rows, n), F32),
        compiler_params=_cparams("arbitrary"),
        name="modulation",
    )(cond, ada_w, ada_b)


SIDE_ROWS = 16


def _ffn_kernel(x_ref, mod_ref, nw_ref, wg_ref, wu_ref, wo_ref, *rest, k_shift, n_side):
    side_in, o_ref, side_out, h_ref = rest[:n_side], rest[n_side], rest[n_side + 1:-1], rest[-1]
    j = pl.program_id(1)
    for src, dst in zip(side_in, side_out):
        dst[...] = src[...].astype(BF16)

    @pl.when(j == 0)
    def _():
        h = _norm_modulate(x_ref[...], nw_ref[...], mod_ref[0, k_shift:k_shift + 1, :],
                           mod_ref[0, k_shift + 1:k_shift + 2, :])
        h_ref[...] = h.astype(BF16)
        o_ref[...] = x_ref[...]

    h = h_ref[...]
    g = jnp.dot(h, wg_ref[...], preferred_element_type=F32)
    u = jnp.dot(h, wu_ref[...], preferred_element_type=F32)
    a = (g * _sigmoid(g) * u).astype(BF16)
    half_gate = 0.5 * mod_ref[0, k_shift + 2:k_shift + 3, :]
    o_ref[...] += half_gate * jnp.dot(a, wo_ref[...], preferred_element_type=F32)


def _ffn(x, mod3, row_of_tile, norm_w, wi, wo, k_shift, tm, side=()):
    n_tok, d = x.shape
    tf = FFN_TF
    nf = D_FF // tf
    n_steps = (n_tok // tm) * nf
    side_specs = []
    for arr in side:
        r, c = arr.shape
        rows = SIDE_ROWS * pl.cdiv(r, SIDE_ROWS * n_steps)
        assert r % rows == 0
        last = r // rows - 1
        side_specs.append(pl.BlockSpec((rows, c), lambda i, j, last=last: (jnp.minimum(i * nf + j, last), 0)))
    out = pl.pallas_call(
        functools.partial(_ffn_kernel, k_shift=k_shift, n_side=len(side)),
        grid=(n_tok // tm, nf),
        in_specs=[
            pl.BlockSpec((tm, d), lambda i, j: (i, 0)),
            pl.BlockSpec((1, N_MOD, d), lambda i, j: (row_of_tile(i, tm), 0, 0)),
            pl.BlockSpec((1, d), lambda i, j: (0, 0)),
            pl.BlockSpec((d, tf), lambda i, j: (0, j)),
            pl.BlockSpec((d, tf), lambda i, j: (0, j + nf)),
            pl.BlockSpec((tf, d), lambda i, j: (j, 0)),
        ] + side_specs,
        out_specs=[pl.BlockSpec((tm, d), lambda i, j: (i, 0))] + side_specs,
        out_shape=[jax.ShapeDtypeStruct((n_tok, d), F32)] + [jax.ShapeDtypeStruct(a.shape, BF16) for a in side],
        scratch_shapes=[pltpu.VMEM((tm, d), BF16)],
        compiler_params=_cparams("arbitrary", "arbitrary"),
        name="ffn",
    )(x, mod3, norm_w, wi, wi, wo, *side)
    return out if side else out[0]


W_IN_SEGS = (("mq", 512), ("mk", 512), ("mv", 1024), ("mo", 1024), ("mg", M_GATES), ("aq", 1024),
             ("ak", 256), ("av", 256), ("gm", 2048), ("ga", 2048))
W_IN_OFF = {}
for _name, _size in W_IN_SEGS:
    W_IN_OFF[_name] = (sum(s for _, s in W_IN_SEGS[:len(W_IN_OFF)]), _size)
IN_COLS = sum(s for _, s in W_IN_SEGS)
MAIN_ORDER = ("gm", "ga", "mv", "mo", "aq", "mq", "mk")
REGROUP_ROWS = 128


def _regroup_kernel(w_ref, main_ref, aux_ref):
    w = w_ref[0]
    t0 = W_IN_OFF["aq"][0]
    tail = w[:, t0:]

    def seg(name):
        start, size = W_IN_OFF[name]
        return (tail[:, start - t0:start - t0 + size] if start >= t0 else w[:, start:start + size]).astype(BF16)

    col = 0
    for name in MAIN_ORDER:
        size = W_IN_OFF[name][1]
        main_ref[:, col:col + size] = seg(name)
        col += size
    g0 = W_IN_OFF["mg"][0]
    lane = lax.broadcasted_iota(jnp.int32, (1, GATE_LANES), 1)
    aux_ref[:, :GATE_LANES] = jnp.where(lane < M_GATES, w[:, g0:g0 + GATE_LANES], 0.0).astype(BF16)
    aux_ref[:, GATE_LANES:GATE_LANES + A_KV_W] = seg("ak")
    aux_ref[:, GATE_LANES + A_KV_W:] = seg("av")


def _regroup_w_in(w, l):
    _, d, n = w.shape
    assert n == IN_COLS and W_IN_OFF["mg"][0] % GATE_LANES == 0
    return pl.pallas_call(
        _regroup_kernel,
        grid=(d // REGROUP_ROWS,),
        in_specs=[pl.BlockSpec((1, REGROUP_ROWS, n), lambda i: (l, i, 0))],
        out_specs=[
            pl.BlockSpec((REGROUP_ROWS, Z_COLS), lambda i: (i, 0)),
            pl.BlockSpec((REGROUP_ROWS, AUX_COLS), lambda i: (i, 0)),
        ],
        out_shape=[jax.ShapeDtypeStruct((d, Z_COLS), BF16), jax.ShapeDtypeStruct((d, AUX_COLS), BF16)],
        compiler_params=_cparams("parallel"),
        name="regroup_w_in",
    )(w)


def _proj_kernel(x_ref, mod_ref, nw_ref, w_ref, waux_ref, z_ref, g_ref, kv_ref, h_ref):
    j = pl.program_id(1)

    @pl.when(j == 0)
    def _():
        h = _norm_modulate(x_ref[...], nw_ref[...], mod_ref[0, 3:4, :], mod_ref[0, 4:5, :]).astype(BF16)
        h_ref[...] = h
        aux = jnp.dot(h, waux_ref[...], preferred_element_type=F32)
        g_ref[...] = aux[:, :GATE_LANES]
        kv_ref[...] = aux[:, GATE_LANES:].astype(BF16)

    half = z_ref.shape[1] // 2
    h = h_ref[...]
    z_ref[:, :half] = jnp.dot(h, w_ref[:, :half], preferred_element_type=F32).astype(BF16)
    z_ref[:, half:] = jnp.dot(h, w_ref[:, half:], preferred_element_type=F32).astype(BF16)


def _mixer_proj(x, mod3, row_of_tile, norm_w, w_main, w_aux, tm):
    n_tok, d = x.shape
    tn = PROJ_TN
    once = functools.partial(pl.BlockSpec, pipeline_mode=pl.Buffered(1))
    return pl.pallas_call(
        _proj_kernel,
        grid=(n_tok // tm, Z_COLS // tn),
        in_specs=[
            pl.BlockSpec((tm, d), lambda i, j: (i, 0)),
            pl.BlockSpec((1, N_MOD, d), lambda i, j: (row_of_tile(i, tm), 0, 0)),
            pl.BlockSpec((1, d), lambda i, j: (0, 0)),
            pl.BlockSpec((d, tn), lambda i, j: (0, j)),
            once((d, AUX_COLS), lambda i, j: (0, 0)),
        ],
        out_specs=[
            pl.BlockSpec((tm, tn), lambda i, j: (i, j)),
            pl.BlockSpec((tm, GATE_LANES), lambda i, j: (i, 0)),
            pl.BlockSpec((tm, 2 * A_KV_W), lambda i, j: (i, 0)),
        ],
        out_shape=[
            jax.ShapeDtypeStruct((n_tok, Z_COLS), BF16),
            jax.ShapeDtypeStruct((n_tok, GATE_LANES), F32),
            jax.ShapeDtypeStruct((n_tok, 2 * A_KV_W), BF16),
        ],
        scratch_shapes=[pltpu.VMEM((tm, d), BF16)],
        compiler_params=_cparams("parallel", "arbitrary"),
        name="mixer_proj",
    )(x, mod3, norm_w, w_main, w_aux)


def _scan_lanes(x, op, fill, reverse):
    n = x.shape[-1]
    lane = lax.broadcasted_iota(jnp.int32, x.shape, 1)
    s = 1
    while s < n:
        if reverse:
            x = op(x, jnp.where(lane < n - s, pltpu.roll(x, n - s, 1), fill))
        else:
            x = op(x, jnp.where(lane >= s, pltpu.roll(x, s, 1), fill))
        s *= 2
    return x


def _dir_scan(x, op, fill, fwd_rows):
    return jnp.where(fwd_rows, _scan_lanes(x, op, fill, False), _scan_lanes(x, op, fill, True))


def _log_sigmoid(x):
    return jnp.minimum(x, 0.0) - jnp.log1p(jnp.exp(-jnp.abs(x)))


def _mlstm_kernel(*refs, n_tok, has_state):
    if has_state:
        (q_ref, k_ref, v_ref, og_ref, gi_ref, gf_ref, bi_ref, bf_ref, nw_ref, c0_ref, n0_ref, m0_ref,
         o_ref) = refs
    else:
        (q_ref, k_ref, v_ref, og_ref, gi_ref, gf_ref, bi_ref, bf_ref, nw_ref,
         o_ref, c_out, n_out, m_out) = refs
    blk = min(MLSTM_BLOCK, n_tok)
    nq = n_tok // blk

    fwd_rows = lax.broadcasted_iota(jnp.int32, (GATE_ROWS, 1), 0) < M_HEADS
    ig = gi_ref[0] + bi_ref[...]
    cum_f = _dir_scan(_log_sigmoid(gf_ref[0] + bf_ref[...]), jnp.add, 0.0, fwd_rows)
    a = ig - cum_f
    m0 = m0_ref[0] if has_state else jnp.zeros((GATE_ROWS, 1), F32)
    mx = jnp.maximum(_dir_scan(a, jnp.maximum, -jnp.inf, fwd_rows), m0)
    stats = [-mx * LOG2E, -(cum_f + mx) * LOG2E]
    a2 = a * LOG2E
    m0_2 = m0 * LOG2E
    if not has_state:
        mx_end = jnp.where(fwd_rows, mx[:, n_tok - 1:n_tok], mx[:, 0:1])
        cum_end = jnp.where(fwd_rows, cum_f[:, n_tok - 1:n_tok], cum_f[:, 0:1])
        stats.append(jnp.exp(a - mx_end))
        m_out[0] = cum_end + mx_end
    pad = jnp.zeros((XPOSE_ROWS - GATE_ROWS * len(stats), n_tok), F32)
    cols = jnp.concatenate(stats + [pad], axis=0).T

    row_i = lax.broadcasted_iota(jnp.int32, (blk, blk), 0)
    col_i = lax.broadcasted_iota(jnp.int32, (blk, blk), 1)
    visible = (col_i <= row_i, col_i >= row_i)

    for h in range(M_HEADS):
        qk_cols = slice(h * M_QK, (h + 1) * M_QK)
        v_cols = slice(h * M_V, (h + 1) * M_V)
        qf = q_ref[0, :, qk_cols].astype(F32) * (M_QK ** -0.5)
        q = qf.astype(BF16)
        k = k_ref[0, :, qk_cols]
        v = v_ref[0, :, v_cols]
        for qi in range(nq):
            rows = slice(qi * blk, (qi + 1) * blk)
            h_sum = None
            for d in range(2):
                r = M_HEADS * d + h
                lo, hi = (0, (qi + 1) * blk) if d == 0 else (qi * blk, n_tok)
                dg = hi - lo - blk if d == 0 else 0
                u_col = cols[rows, r:r + 1]
                neg_mt_col = cols[rows, GATE_ROWS + r:GATE_ROWS + r + 1]
                dm = u_col + a2[r:r + 1, lo:hi]
                parts = [dm[:, :dg]] if dg > 0 else []
                parts.append(jnp.where(visible[d], dm[:, dg:dg + blk], -jnp.inf))
                if dg + blk < hi - lo:
                    parts.append(dm[:, dg + blk:])
                dm = jnp.concatenate(parts, axis=1) if len(parts) > 1 else parts[0]
                s = lax.dot_general(q[rows], k[lo:hi], NT_DIMS, preferred_element_type=F32) * jnp.exp2(dm)
                num = jnp.dot(s.astype(BF16), v[lo:hi], preferred_element_type=F32)
                den = jnp.sum(s, axis=1, keepdims=True)
                if has_state:
                    decay = jnp.exp2(u_col + m0_2[r:r + 1, :])
                    num = num + decay * lax.dot_general(q[rows], c0_ref[0, r].astype(BF16), NT_DIMS,
                                                        preferred_element_type=F32)
                    den = den + decay * jnp.sum(qf[rows] * n0_ref[0, r:r + 1, :], axis=1, keepdims=True)
                h_dir = num / jnp.maximum(jnp.abs(den), jnp.exp2(neg_mt_col))
                h_sum = h_dir if h_sum is None else h_sum + h_dir
            hn = h_sum * lax.rsqrt(jnp.mean(h_sum * h_sum, axis=-1, keepdims=True) + NORM_EPS) * nw_ref[:, v_cols]
            o_ref[0, rows, v_cols] = (hn * _sigmoid(og_ref[0, rows, v_cols].astype(F32))).astype(BF16)
        if not has_state:
            kf = k.astype(F32)
            vf = v.astype(F32)
            for d in range(2):
                r = M_HEADS * d + h
                wk_col = cols[:, 2 * GATE_ROWS + r:2 * GATE_ROWS + r + 1]
                c_out[0, r] = lax.dot_general((vf * wk_col).astype(BF16), k, TN_DIMS, preferred_element_type=F32)
                n_out[0, r:r + 1, :] = jnp.sum(kf * wk_col, axis=0, keepdims=True)


def _mlstm(z3, gate_i, gate_f, bias_i, bias_f, norm_w, states):
    bsz, n_tok, _ = z3.shape
    has_state = states is not None
    qk_w, v_w = M_HEADS * M_QK, M_HEADS * M_V
    in_specs = [
        pl.BlockSpec((1, n_tok, qk_w), lambda b: (b, 0, Z_MQ // qk_w)),
        pl.BlockSpec((1, n_tok, qk_w), lambda b: (b, 0, Z_MK // qk_w)),
        pl.BlockSpec((1, n_tok, v_w), lambda b: (b, 0, Z_MV // v_w)),
        pl.BlockSpec((1, n_tok, v_w), lambda b: (b, 0, Z_MO // v_w)),
        pl.BlockSpec((1, GATE_ROWS, n_tok), lambda b: (b, 0, 0)),
        pl.BlockSpec((1, GATE_ROWS, n_tok), lambda b: (b, 0, 0)),
        pl.BlockSpec((GATE_ROWS, 1), lambda b: (0, 0)),
        pl.BlockSpec((GATE_ROWS, 1), lambda b: (0, 0)),
        pl.BlockSpec((1, v_w), lambda b: (0, 0)),
    ]
    state_specs = [
        pl.BlockSpec((1, GATE_ROWS, M_V, M_QK), lambda b: (b, 0, 0, 0)),
        pl.BlockSpec((1, GATE_ROWS, M_QK), lambda b: (b, 0, 0)),
        pl.BlockSpec((1, GATE_ROWS, 1), lambda b: (b, 0, 0)),
    ]
    out_specs = [pl.BlockSpec((1, n_tok, v_w), lambda b: (b, 0, 0))]
    out_shape = [jax.ShapeDtypeStruct((bsz, n_tok, v_w), BF16)]
    args = [z3, z3, z3, z3, gate_i, gate_f, bias_i, bias_f, norm_w]
    if has_state:
        in_specs += state_specs
        args += list(states)
    else:
        out_specs += state_specs
        out_shape += [
            jax.ShapeDtypeStruct((bsz, GATE_ROWS, M_V, M_QK), F32),
            jax.ShapeDtypeStruct((bsz, GATE_ROWS, M_QK), F32),
            jax.ShapeDtypeStruct((bsz, GATE_ROWS, 1), F32),
        ]
    return pl.pallas_call(
        functools.partial(_mlstm_kernel, n_tok=n_tok, has_state=has_state),
        grid=(bsz,),
        in_specs=in_specs,
        out_specs=out_specs,
        out_shape=out_shape,
        compiler_params=_cparams("parallel"),
        name="mlstm",
    )(*args)


def _group_rms(x, w_row, ones_bd):
    x2 = x * x
    hi = x2.astype(BF16)
    lo = (x2 - hi.astype(F32)).astype(BF16)
    ss = jnp.dot(hi, ones_bd, preferred_element_type=F32) + jnp.dot(lo, ones_bd, preferred_element_type=F32)
    return x * lax.rsqrt(ss * (1.0 / A_HD) + NORM_EPS) * w_row


def _rope(x, cos, sin_signed, lane_lo):
    n = x.shape[-1]
    partner = jnp.where(lane_lo, pltpu.roll(x, n - 16, 1), pltpu.roll(x, 16, 1))
    return x * cos + partner * sin_signed


def _attend_block(q, sink_col, k_loc, v_loc, kinds, masks, k_ctx=None, v_ctx=None):
    s_loc = lax.dot_general(q, k_loc, NT_DIMS, preferred_element_type=F32)
    tiles = []
    for j, kind in enumerate(kinds):
        t = s_loc[:, j * WINDOW:(j + 1) * WINDOW]
        tiles.append(t if kind == "cur" else jnp.where(masks[kind], t, -jnp.inf))
    n_loc = len(tiles)
    if k_ctx is not None:
        s_ctx = lax.dot_general(q, k_ctx, NT_DIMS, preferred_element_type=F32)
        tiles += [s_ctx[:, j * WINDOW:(j + 1) * WINDOW] for j in range(k_ctx.shape[0] // WINDOW)]
    tile_max = tiles[0]
    for t in tiles[1:]:
        tile_max = jnp.maximum(tile_max, t)
    m = jnp.maximum(jnp.max(tile_max, axis=1, keepdims=True), sink_col)
    p = [jnp.exp2(t - m).astype(BF16) for t in tiles]
    acc = jnp.dot(jnp.concatenate(p[:n_loc], axis=1), v_loc, preferred_element_type=F32)
    if k_ctx is not None:
        acc = acc + jnp.dot(jnp.concatenate(p[n_loc:], axis=1), v_ctx, preferred_element_type=F32)
    den = acc[:, A_HD:A_HD + 1] + jnp.exp2(sink_col - m)
    return acc[:, :A_HD] / den


def _attn_kernel(*refs, n_tok, latent):
    if latent:
        (sink_ref, q_ref, k_ref, v_ref, qw_ref, kw_ref, cos_ref, sin_ref, ck_ref, cv_ref,
         o_ref, qh_ref, kg_ref, va_ref) = refs
    else:
        (sink_ref, q_ref, k_ref, v_ref, qw_ref, kw_ref,
         o_ref, ko_ref, vo_ref) = refs

    lane = lax.broadcasted_iota(jnp.int32, (1, A_KV_W), 1)
    lane_lo = (lane % 32) < 16
    bd_r = lax.broadcasted_iota(jnp.int32, (A_KV_W, A_KV_W), 0) // A_HD
    bd_c = lax.broadcasted_iota(jnp.int32, (A_KV_W, A_KV_W), 1) // A_HD
    ones_bd = jnp.where(bd_r == bd_c, 1.0, 0.0).astype(BF16)

    k_all = _group_rms(k_ref[0].astype(F32), kw_ref[...], ones_bd)
    v_all = v_ref[0]
    if latent:
        k_all = _rope(k_all, cos_ref[...], sin_ref[...], lane_lo)
    else:
        ko_ref[0] = k_all
        vo_ref[0] = v_all.astype(F32)

    q_blk = WINDOW if latent else n_tok
    m_rows = A_GROUP * q_blk
    head_of_row = lax.broadcasted_iota(jnp.int32, (m_rows, 1), 0) // q_blk
    r_in = lax.broadcasted_iota(jnp.int32, (m_rows, WINDOW), 0) % WINDOW
    c_in = lax.broadcasted_iota(jnp.int32, (m_rows, WINDOW), 1)
    masks = {"prev": c_in >= r_in, "next": c_in <= r_in}
    ones_v = jnp.ones((n_tok, A_HD), BF16)

    def sink_column(g):
        col = jnp.zeros((m_rows, 1), F32)
        for a in range(A_GROUP):
            col = jnp.where(head_of_row == a, sink_ref[A_GROUP * g + a] * LOG2E, col)
        return col

    def unstack(o):
        return jnp.concatenate([o[a * q_blk:(a + 1) * q_blk, :] for a in range(A_GROUP)], axis=1).astype(BF16)

    for g in range(A_KV):
        g_cols = slice(A_KV_W * g, A_KV_W * (g + 1))
        h_cols = slice(A_HD * g, A_HD * (g + 1))
        qg = _group_rms(q_ref[0, :, g_cols].astype(F32), qw_ref[...], ones_bd)
        if latent:
            qg = _rope(qg, cos_ref[...], sin_ref[...], lane_lo)
        qg = qg * (A_HD ** -0.5 * LOG2E)
        kg = k_all[:, h_cols].astype(BF16)
        va = jnp.concatenate([v_all[:, h_cols], ones_v], axis=1)
        sink_col = sink_column(g)
        if not latent:
            q_stack = jnp.concatenate([qg[:, A_HD * a:A_HD * (a + 1)] for a in range(A_GROUP)], axis=0)
            o = _attend_block(q_stack.astype(BF16), sink_col, kg, va, ["cur"] * (n_tok // WINDOW), masks)
            o_ref[0, :, g_cols] = unstack(o)
            continue

        for a in range(A_GROUP):
            qh_ref[a] = qg[:, A_HD * a:A_HD * (a + 1)].astype(BF16)
        kg_ref[...] = kg
        va_ref[...] = va
        n_ctx = ck_ref.shape[1]
        ckg = ck_ref[0][:, h_cols].astype(BF16)
        cva = jnp.concatenate([cv_ref[0][:, h_cols].astype(BF16), jnp.ones((n_ctx, A_HD), BF16)], axis=1)
        nb = n_tok // WINDOW

        def q_block(i, kinds, k_start, g_cols=g_cols, sink_col=sink_col, ckg=ckg, cva=cva, qg=qg, kg=kg, va=va):
            n_keys = len(kinds) * WINDOW
            if isinstance(i, int):
                r0 = i * WINDOW
                q_stack = jnp.concatenate([qg[r0:r0 + WINDOW, A_HD * a:A_HD * (a + 1)] for a in range(A_GROUP)],
                                          axis=0).astype(BF16)
                k_loc = kg[k_start:k_start + n_keys]
                v_loc = va[k_start:k_start + n_keys]
            else:
                r0 = pl.multiple_of(i * WINDOW, WINDOW)
                k_start = pl.multiple_of(k_start, WINDOW)
                q_stack = jnp.concatenate([qh_ref[a, pl.ds(r0, WINDOW), :] for a in range(A_GROUP)], axis=0)
                k_loc = kg_ref[pl.ds(k_start, n_keys), :]
                v_loc = va_ref[pl.ds(k_start, n_keys), :]
            o = _attend_block(q_stack, sink_col, k_loc, v_loc, kinds, masks, ckg, cva)
            o_ref[0, pl.ds(r0, WINDOW), g_cols] = unstack(o)

        def interior(i, carry, q_block=q_block):
            q_block(i, ["prev", "cur", "next"], (i - 1) * WINDOW)
            return carry

        lax.fori_loop(1, nb - 1, interior, 0, unroll=3)
        q_block(0, ["cur", "next"], 0)
        q_block(nb - 1, ["prev", "cur"], (nb - 2) * WINDOW)


def _attention(z3, kv3, sink, q_norm_w, k_norm_w, rope_tabs, cache):
    bsz, n_tok, _ = z3.shape
    latent = cache is not None
    in_specs = [
        pl.BlockSpec(memory_space=pltpu.SMEM),
        pl.BlockSpec((1, n_tok, A_Q_W), lambda b: (b, 0, Z_AQ // A_Q_W)),
        pl.BlockSpec((1, n_tok, A_KV_W), lambda b: (b, 0, 0)),
        pl.BlockSpec((1, n_tok, A_KV_W), lambda b: (b, 0, 1)),
        pl.BlockSpec((1, A_KV_W), lambda b: (0, 0)),
        pl.BlockSpec((1, A_KV_W), lambda b: (0, 0)),
    ]
    args = [sink, z3, kv3, kv3, q_norm_w, k_norm_w]
    out_specs = [pl.BlockSpec((1, n_tok, A_Q_W), lambda b: (b, 0, 0))]
    out_shape = [jax.ShapeDtypeStruct((bsz, n_tok, A_Q_W), BF16)]
    scratch = []
    if latent:
        assert n_tok // WINDOW >= 3
        n_ctx = cache[0].shape[1]
        in_specs += [
            pl.BlockSpec((n_tok, A_KV_W), lambda b: (0, 0)),
            pl.BlockSpec((n_tok, A_KV_W), lambda b: (0, 0)),
            pl.BlockSpec((1, n_ctx, A_KV_W), lambda b: (b, 0, 0)),
            pl.BlockSpec((1, n_ctx, A_KV_W), lambda b: (b, 0, 0)),
        ]
        args += [rope_tabs[0], rope_tabs[1], cache[0], cache[1]]
        scratch = [
            pltpu.VMEM((A_GROUP, n_tok, A_HD), BF16),
            pltpu.VMEM((n_tok, A_HD), BF16),
            pltpu.VMEM((n_tok, 2 * A_HD), BF16),
        ]
    else:
        out_specs += [pl.BlockSpec((1, n_tok, A_KV_W), lambda b: (b, 0, 0))] * 2
        out_shape += [jax.ShapeDtypeStruct((bsz, n_tok, A_KV_W), F32)] * 2
    return pl.pallas_call(
        functools.partial(_attn_kernel, n_tok=n_tok, latent=latent),
        grid=(bsz,),
        in_specs=in_specs,
        out_specs=out_specs,
        out_shape=out_shape,
        scratch_shapes=scratch,
        compiler_params=_cparams("parallel"),
        name="attention",
    )(*args)


def _merge_kernel(x_ref, mod_ref, hm_ref, ha_ref, gm_ref, ga_ref, wpm_ref, wpa_ref, wo_ref, o_ref):
    pm = jnp.dot(hm_ref[...], wpm_ref[...], preferred_element_type=F32)
    pa = jnp.dot(ha_ref[...], wpa_ref[...], preferred_element_type=F32)
    u = _sigmoid(gm_ref[...].astype(F32)) * pm + _sigmoid(ga_ref[...].astype(F32)) * pa
    mix = jnp.dot(u.astype(BF16), wo_ref[...], preferred_element_type=F32)
    o_ref[...] = x_ref[...] + mod_ref[0, 5:6, :] * mix


def _merge(x, mod3, row_of_tile, hm, ha, z, w_proj_m, w_proj_a, w_out, tm):
    n_tok, d = x.shape
    resident = functools.partial(pl.BlockSpec, pipeline_mode=pl.Buffered(1))
    return pl.pallas_call(
        _merge_kernel,
        grid=(n_tok // tm,),
        in_specs=[
            pl.BlockSpec((tm, d), lambda i: (i, 0)),
            pl.BlockSpec((1, N_MOD, d), lambda i: (row_of_tile(i, tm), 0, 0)),
            pl.BlockSpec((tm, M_HEADS * M_V), lambda i: (i, 0)),
            pl.BlockSpec((tm, A_Q_W), lambda i: (i, 0)),
            pl.BlockSpec((tm, d), lambda i: (i, Z_GM // D_MODEL)),
            pl.BlockSpec((tm, d), lambda i: (i, Z_GA // D_MODEL)),
            resident((M_HEADS * M_V, d), lambda i: (0, 0)),
            resident((A_Q_W, d), lambda i: (0, 0)),
            resident((d, d), lambda i: (0, 0)),
        ],
        out_specs=pl.BlockSpec((tm, d), lambda i: (i, 0)),
        out_shape=jax.ShapeDtypeStruct((n_tok, d), F32),
        compiler_params=_cparams("parallel"),
        name="merge",
    )(x, mod3, hm, ha, z, z, w_proj_m, w_proj_a, w_out)


def _rope_tables(n_tok):
    nf = A_HD // 4
    inv = ROPE_THETA ** (-jnp.arange(nf, dtype=F32) / nf)
    tok = jnp.arange(n_tok)
    pos = jnp.stack([tok // GRID_W, tok % GRID_W], axis=1).astype(F32)
    ang = pos[:, :, None] * inv
    cos = jnp.cos(ang)
    sin = jnp.sin(ang)
    cos_h = jnp.concatenate([cos, cos], axis=-1).reshape(n_tok, A_HD)
    sin_h = jnp.concatenate([-sin, sin], axis=-1).reshape(n_tok, A_HD)
    return jnp.tile(cos_h, (1, A_KV)), jnp.tile(sin_h, (1, A_KV))


TM_FFN = 512
TM_PROJ = 1024
TM_MERGE = 256


def _trunk(x, mod3, row_of_tile, wts, states, rope_tabs, cache):
    bsz, n_tok, d = x.shape
    x2 = x.reshape(bsz * n_tok, d)
    z, gates, kv = _mixer_proj(x2, mod3, row_of_tile, wts["norm_mix"], wts["w_main"], wts["w_aux"], TM_PROJ)
    z3 = z.reshape(bsz, n_tok, Z_COLS)
    g = jnp.transpose(gates[:, :M_GATES].reshape(bsz, n_tok, 2, 2, M_HEADS), (0, 2, 3, 4, 1))
    gate_i = g[:, :, 0].reshape(bsz, GATE_ROWS, n_tok)
    gate_f = g[:, :, 1].reshape(bsz, GATE_ROWS, n_tok)
    m_out = _mlstm(z3, gate_i, gate_f, wts["bias_i"], wts["bias_f"], wts["mlstm_norm"], states)
    a_out = _attention(z3, kv.reshape(bsz, n_tok, 2 * A_KV_W), wts["sink"], wts["q_norm"], wts["k_norm"],
                       rope_tabs, cache)
    hm = m_out[0].reshape(bsz * n_tok, M_HEADS * M_V)
    ha = a_out[0].reshape(bsz * n_tok, A_Q_W)
    x2 = _merge(x2, mod3, row_of_tile, hm, ha, z, wts["w_proj_m"], wts["w_proj_a"], wts["w_out"], TM_MERGE)
    x2 = _ffn(x2, mod3, row_of_tile, wts["norm2"], wts["wi2"], wts["wo2"], 6, TM_FFN)
    return x2.reshape(bsz, n_tok, d), m_out[1:], a_out[1:]


def kernel(x_prompt, x_sample, cache_attn_k, cache_attn_v, state_mlstm_C, state_mlstm_n, state_mlstm_m, c, c_ctx, ada_w, ada_b, norm_ffn1_w, ffn1_wi, ffn1_wo, norm_mix_w, w_in, mlstm_gate_b, mlstm_norm_w, attn_q_norm_w, attn_k_norm_w, attn_sink, w_proj_m, w_proj_a, w_out, norm_ffn2_w, ffn2_wi, ffn2_wo):
    bp, tp, d = x_prompt.shape
    bs, ts, _ = x_sample.shape
    n_ctx = cache_attn_k.shape[2]
    l = 0

    n_rows = 16
    cond = jnp.concatenate([c_ctx[None, :], c, jnp.zeros((n_rows - 1 - bs, d), F32)], axis=0)
    mod3 = _modulation(cond, ada_w[l], ada_b[l][None, :]).reshape(n_rows, N_MOD, d)

    w_main, w_aux = _regroup_w_in(w_in, l)
    gate_b = mlstm_gate_b[l].reshape(2, 2, M_HEADS)
    norm1, wi1, wo1 = norm_ffn1_w[l][None, :], ffn1_wi[l].astype(BF16), ffn1_wo[l].astype(BF16)
    ctx_row = lambda i, tm: 0
    lat_row = lambda i, tm: 1 + (i * tm) // ts

    x1_prompt = _ffn(x_prompt.reshape(bp * tp, d), mod3, ctx_row, norm1, wi1, wo1, 0, TM_FFN).reshape(bp, tp, d)
    x1_sample, wi2, wo2, w_out_b, w_pm_b, w_pa_b = _ffn(
        x_sample.reshape(bs * ts, d), mod3, lat_row, norm1, wi1, wo1, 0, TM_FFN,
        side=(ffn2_wi[l], ffn2_wo[l], w_out[l], w_proj_m[l], w_proj_a[l]))
    x1_sample = x1_sample.reshape(bs, ts, d)
    wts = dict(
        norm_mix=norm_mix_w[l][None, :], w_main=w_main, w_aux=w_aux,
        bias_i=gate_b[:, 0].reshape(GATE_ROWS, 1), bias_f=gate_b[:, 1].reshape(GATE_ROWS, 1),
        mlstm_norm=mlstm_norm_w[l][None, :],
        sink=attn_sink[l], q_norm=jnp.tile(attn_q_norm_w[l], A_KV)[None, :],
        k_norm=jnp.tile(attn_k_norm_w[l], A_KV)[None, :],
        w_proj_m=w_pm_b, w_proj_a=w_pa_b, w_out=w_out_b,
        norm2=norm_ffn2_w[l][None, :], wi2=wi2, wo2=wo2,
    )

    y_prompt, (c_new, n_new, m_new), (k_new, v_new) = _trunk(x1_prompt, mod3, ctx_row, wts, None, None, None)
    new_attn_k = k_new.reshape(bp, 1, tp, A_KV, A_HD)
    new_attn_v = v_new.reshape(bp, 1, tp, A_KV, A_HD)
    new_c = c_new.reshape(bp, 1, 2, M_HEADS, M_V, M_QK)
    new_n = n_new.reshape(bp, 1, 2, M_HEADS, M_QK)
    new_m = m_new.reshape(bp, 1, 2, M_HEADS)

    states = (state_mlstm_C[:, l].reshape(bs, GATE_ROWS, M_V, M_QK),
              state_mlstm_n[:, l].reshape(bs, GATE_ROWS, M_QK),
              state_mlstm_m[:, l].reshape(bs, GATE_ROWS, 1))
    cache = (cache_attn_k[:, l].reshape(bs, n_ctx, A_KV_W), cache_attn_v[:, l].reshape(bs, n_ctx, A_KV_W))
    y_sample, _, _ = _trunk(x1_sample, mod3, lat_row, wts, states, _rope_tables(ts), cache)

    return (y_prompt, y_sample, new_attn_k, new_attn_v, new_c, new_n, new_m)
```

```python
import functools

import jax
import jax.numpy as jnp
from jax import lax
from jax.experimental import pallas as pl
from jax.experimental.pallas import tpu as pltpu

F32 = jnp.float32
BF16 = jnp.bfloat16
LOG2E = 1.4426950408889634

D_MODEL = 2048
D_FF = 5632
N_MOD = 9
NORM_EPS = 1e-6
GRID_W = 64
ROPE_THETA = 10000.0
M_HEADS = 4
M_QK = 128
M_V = 256
M_GATES = 4 * M_HEADS
A_HEADS = 16
A_KV = 4
A_GROUP = A_HEADS // A_KV
A_HD = 64
A_Q_W = A_HEADS * A_HD
A_KV_W = A_KV * A_HD
WINDOW = 128

Z_GM, Z_GA, Z_MV, Z_MO, Z_AQ, Z_MQ, Z_MK = 0, 2048, 4096, 5120, 6144, 7168, 7680
Z_COLS = 8192
GATE_LANES = 128
AUX_COLS = GATE_LANES + 2 * A_KV_W
GATE_ROWS = 2 * M_HEADS

FFN_TF = 512
PROJ_TN = 1024
MLSTM_BLOCK = 256
XPOSE_ROWS = 128

VMEM_LIMIT = 56 * 1024 * 1024
NT_DIMS = (((1,), (1,)), ((), ()))
TN_DIMS = (((0,), (0,)), ((), ()))


def _cparams(*sem):
    return pltpu.CompilerParams(dimension_semantics=sem, vmem_limit_bytes=VMEM_LIMIT)


def _sigmoid(x):
    return 1.0 / (1.0 + jnp.exp(-x))


def _norm_modulate(x, norm_w, shift, scale):
    ms = jnp.mean(x * x, axis=-1, keepdims=True)
    y = x * lax.rsqrt(ms + NORM_EPS) * norm_w
    return y * (1.0 + scale) + shift


def _mod_kernel(c_ref, w_ref, b_ref, o_ref):
    c = c_ref[...]
    s = (c * _sigmoid(c)).astype(BF16)
    o_ref[...] = jnp.dot(s, w_ref[...].astype(BF16), preferred_element_type=F32) + b_ref[...]


def _modulation(cond, ada_w, ada_b, tn=1024):
    rows, d = cond.shape
    n = ada_w.shape[1]
    return pl.pallas_call(
        _mod_kernel,
        grid=(n // tn,),
        in_specs=[
            pl.BlockSpec((rows, d), lambda j: (0, 0)),
            pl.BlockSpec((d, tn), lambda j: (0, j)),
            pl.BlockSpec((1, tn), lambda j: (0, j)),
        ],
        out_specs=pl.BlockSpec((rows, tn), lambda j: (0, j)),
        out_shape=jax.ShapeDtypeStruct((rows, n), F32),
        compiler_params=_cparams("arbitrary"),
        name="modulation",
    )(cond, ada_w, ada_b)


SIDE_ROWS = 16


FFN_SLOTS = 3


def _ffn_kernel(x_ref, mod_ref, nw_ref, wi_hbm, wo_hbm, *rest, k_shift, n_side, nf, tf):
    side_in, o_ref, side_out = rest[:n_side], rest[n_side], rest[n_side + 1:2 * n_side + 1]
    h_ref, wg_buf, wu_buf, wo_buf, sem = rest[2 * n_side + 1:]
    i = pl.program_id(0)
    j = pl.program_id(1)
    n_steps = pl.num_programs(0) * nf
    s = i * nf + j

    def chunk_copies(step, slot):
        c = step % nf
        g_cols = pl.ds(pl.multiple_of(c * tf, tf), tf)
        u_cols = pl.ds(pl.multiple_of((c + nf) * tf, tf), tf)
        return (pltpu.make_async_copy(wi_hbm.at[:, g_cols], wg_buf.at[slot], sem.at[0, slot]),
                pltpu.make_async_copy(wi_hbm.at[:, u_cols], wu_buf.at[slot], sem.at[1, slot]),
                pltpu.make_async_copy(wo_hbm.at[g_cols, :], wo_buf.at[slot], sem.at[2, slot]))

    @pl.when(s == 0)
    def _():
        for first in range(FFN_SLOTS - 1):
            for cp in chunk_copies(first, first):
                cp.start()

    ahead = s + FFN_SLOTS - 1

    @pl.when(ahead < n_steps)
    def _():
        for cp in chunk_copies(ahead, ahead % FFN_SLOTS):
            cp.start()

    for src, dst in zip(side_in, side_out):
        dst[...] = src[...].astype(BF16)

    @pl.when(j == 0)
    def _():
        h = _norm_modulate(x_ref[...], nw_ref[...], mod_ref[0, k_shift:k_shift + 1, :],
                           mod_ref[0, k_shift + 1:k_shift + 2, :])
        h_ref[...] = h.astype(BF16)
        o_ref[...] = x_ref[...]

    slot = s % FFN_SLOTS
    for cp in chunk_copies(s, slot):
        cp.wait()

    h = h_ref[...]
    g = jnp.dot(h, wg_buf[slot], preferred_element_type=F32)
    u = jnp.dot(h, wu_buf[slot], preferred_element_type=F32)
    a = (g * _sigmoid(g) * u).astype(BF16)
    half_gate = 0.5 * mod_ref[0, k_shift + 2:k_shift + 3, :]
    o_ref[...] += half_gate * jnp.dot(a, wo_buf[slot], preferred_element_type=F32)


def _ffn(x, mod3, row_of_tile, norm_w, wi, wo, k_shift, tm, side=()):
    n_tok, d = x.shape
    tf = FFN_TF
    nf = D_FF // tf
    n_steps = (n_tok // tm) * nf
    side_specs = []
    for arr in side:
        r, c = arr.shape
        rows = SIDE_ROWS * pl.cdiv(r, SIDE_ROWS * n_steps)
        assert r % rows == 0
        last = r // rows - 1
        side_specs.append(pl.BlockSpec((rows, c), lambda i, j, last=last: (jnp.minimum(i * nf + j, last), 0)))
    assert n_steps >= FFN_SLOTS - 1
    out = pl.pallas_call(
        functools.partial(_ffn_kernel, k_shift=k_shift, n_side=len(side), nf=nf, tf=tf),
        grid=(n_tok // tm, nf),
        in_specs=[
            pl.BlockSpec((tm, d), lambda i, j: (i, 0)),
            pl.BlockSpec((1, N_MOD, d), lambda i, j: (row_of_tile(i, tm), 0, 0)),
            pl.BlockSpec((1, d), lambda i, j: (0, 0)),
            pl.BlockSpec(memory_space=pl.ANY),
            pl.BlockSpec(memory_space=pl.ANY),
        ] + side_specs,
        out_specs=[pl.BlockSpec((tm, d), lambda i, j: (i, 0))] + side_specs,
        out_shape=[jax.ShapeDtypeStruct((n_tok, d), F32)] + [jax.ShapeDtypeStruct(a.shape, BF16) for a in side],
        scratch_shapes=[
            pltpu.VMEM((tm, d), BF16),
            pltpu.VMEM((FFN_SLOTS, d, tf), BF16),
            pltpu.VMEM((FFN_SLOTS, d, tf), BF16),
            pltpu.VMEM((FFN_SLOTS, tf, d), BF16),
            pltpu.SemaphoreType.DMA((3, FFN_SLOTS)),
        ],
        compiler_params=_cparams("arbitrary", "arbitrary"),
        name="ffn",
    )(x, mod3, norm_w, wi, wo, *side)
    return out if side else out[0]


W_IN_SEGS = (("mq", 512), ("mk", 512), ("mv", 1024), ("mo", 1024), ("mg", M_GATES), ("aq", 1024),
             ("ak", 256), ("av", 256), ("gm", 2048), ("ga", 2048))
W_IN_OFF = {}
for _name, _size in W_IN_SEGS:
    W_IN_OFF[_name] = (sum(s for _, s in W_IN_SEGS[:len(W_IN_OFF)]), _size)
IN_COLS = sum(s for _, s in W_IN_SEGS)
MAIN_ORDER = ("gm", "ga", "mv", "mo", "aq", "mq", "mk")
REGROUP_ROWS = 128


def _regroup_kernel(w_ref, main_ref, aux_ref):
    w = w_ref[0]
    t0 = W_IN_OFF["aq"][0]
    tail = w[:, t0:]

    def seg(name):
        start, size = W_IN_OFF[name]
        return (tail[:, start - t0:start - t0 + size] if start >= t0 else w[:, start:start + size]).astype(BF16)

    col = 0
    for name in MAIN_ORDER:
        size = W_IN_OFF[name][1]
        main_ref[:, col:col + size] = seg(name)
        col += size
    g0 = W_IN_OFF["mg"][0]
    lane = lax.broadcasted_iota(jnp.int32, (1, GATE_LANES), 1)
    aux_ref[:, :GATE_LANES] = jnp.where(lane < M_GATES, w[:, g0:g0 + GATE_LANES], 0.0).astype(BF16)
    aux_ref[:, GATE_LANES:GATE_LANES + A_KV_W] = seg("ak")
    aux_ref[:, GATE_LANES + A_KV_W:] = seg("av")


def _regroup_w_in(w, l):
    _, d, n = w.shape
    assert n == IN_COLS and W_IN_OFF["mg"][0] % GATE_LANES == 0
    return pl.pallas_call(
        _regroup_kernel,
        grid=(d // REGROUP_ROWS,),
        in_specs=[pl.BlockSpec((1, REGROUP_ROWS, n), lambda i: (l, i, 0))],
        out_specs=[
            pl.BlockSpec((REGROUP_ROWS, Z_COLS), lambda i: (i, 0)),
            pl.BlockSpec((REGROUP_ROWS, AUX_COLS), lambda i: (i, 0)),
        ],
        out_shape=[jax.ShapeDtypeStruct((d, Z_COLS), BF16), jax.ShapeDtypeStruct((d, AUX_COLS), BF16)],
        compiler_params=_cparams("parallel"),
        name="regroup_w_in",
    )(w)


def _proj_kernel(x_ref, mod_ref, nw_ref, w_ref, waux_ref, z_ref, g_ref, kv_ref, h_ref):
    j = pl.program_id(1)

    @pl.when(j == 0)
    def _():
        h = _norm_modulate(x_ref[...], nw_ref[...], mod_ref[0, 3:4, :], mod_ref[0, 4:5, :]).astype(BF16)
        h_ref[...] = h
        aux = jnp.dot(h, waux_ref[...], preferred_element_type=F32)
        g_ref[...] = aux[:, :GATE_LANES]
        kv_ref[...] = aux[:, GATE_LANES:].astype(BF16)

    half = z_ref.shape[1] // 2
    h = h_ref[...]
    z_ref[:, :half] = jnp.dot(h, w_ref[:, :half], preferred_element_type=F32).astype(BF16)
    z_ref[:, half:] = jnp.dot(h, w_ref[:, half:], preferred_element_type=F32).astype(BF16)


def _mixer_proj(x, mod3, row_of_tile, norm_w, w_main, w_aux, tm):
    n_tok, d = x.shape
    tn = PROJ_TN
    once = functools.partial(pl.BlockSpec, pipeline_mode=pl.Buffered(1))
    return pl.pallas_call(
        _proj_kernel,
        grid=(n_tok // tm, Z_COLS // tn),
        in_specs=[
            pl.BlockSpec((tm, d), lambda i, j: (i, 0)),
            pl.BlockSpec((1, N_MOD, d), lambda i, j: (row_of_tile(i, tm), 0, 0)),
            pl.BlockSpec((1, d), lambda i, j: (0, 0)),
            pl.BlockSpec((d, tn), lambda i, j: (0, j)),
            once((d, AUX_COLS), lambda i, j: (0, 0)),
        ],
        out_specs=[
            pl.BlockSpec((tm, tn), lambda i, j: (i, j)),
            pl.BlockSpec((tm, GATE_LANES), lambda i, j: (i, 0)),
            pl.BlockSpec((tm, 2 * A_KV_W), lambda i, j: (i, 0)),
        ],
        out_shape=[
            jax.ShapeDtypeStruct((n_tok, Z_COLS), BF16),
            jax.ShapeDtypeStruct((n_tok, GATE_LANES), F32),
            jax.ShapeDtypeStruct((n_tok, 2 * A_KV_W), BF16),
        ],
        scratch_shapes=[pltpu.VMEM((tm, d), BF16)],
        compiler_params=_cparams("parallel", "arbitrary"),
        name="mixer_proj",
    )(x, mod3, norm_w, w_main, w_aux)


def _scan_lanes(x, op, fill, reverse):
    n = x.shape[-1]
    lane = lax.broadcasted_iota(jnp.int32, x.shape, 1)
    s = 1
    while s < n:
        if reverse:
            x = op(x, jnp.where(lane < n - s, pltpu.roll(x, n - s, 1), fill))
        else:
            x = op(x, jnp.where(lane >= s, pltpu.roll(x, s, 1), fill))
        s *= 2
    return x


def _dir_scan(x, op, fill, fwd_rows):
    return jnp.where(fwd_rows, _scan_lanes(x, op, fill, False), _scan_lanes(x, op, fill, True))


def _log_sigmoid(x):
    return jnp.minimum(x, 0.0) - jnp.log1p(jnp.exp(-jnp.abs(x)))


def _mlstm_kernel(*refs, n_tok, has_state):
    if has_state:
        (q_ref, k_ref, v_ref, og_ref, gi_ref, gf_ref, bi_ref, bf_ref, nw_ref, c0_ref, n0_ref, m0_ref,
         o_ref) = refs
    else:
        (q_ref, k_ref, v_ref, og_ref, gi_ref, gf_ref, bi_ref, bf_ref, nw_ref,
         o_ref, c_out, n_out, m_out) = refs
    blk = min(MLSTM_BLOCK, n_tok)
    nq = n_tok // blk

    fwd_rows = lax.broadcasted_iota(jnp.int32, (GATE_ROWS, 1), 0) < M_HEADS
    ig = gi_ref[0] + bi_ref[...]
    cum_f = _dir_scan(_log_sigmoid(gf_ref[0] + bf_ref[...]), jnp.add, 0.0, fwd_rows)
    a = ig - cum_f
    m0 = m0_ref[0] if has_state else jnp.zeros((GATE_ROWS, 1), F32)
    mx = jnp.maximum(_dir_scan(a, jnp.maximum, -jnp.inf, fwd_rows), m0)
    stats = [-mx * LOG2E, -(cum_f + mx) * LOG2E]
    a2 = a * LOG2E
    m0_2 = m0 * LOG2E
    if not has_state:
        mx_end = jnp.where(fwd_rows, mx[:, n_tok - 1:n_tok], mx[:, 0:1])
        cum_end = jnp.where(fwd_rows, cum_f[:, n_tok - 1:n_tok], cum_f[:, 0:1])
        stats.append(jnp.exp(a - mx_end))
        m_out[0] = cum_end + mx_end
    pad = jnp.zeros((XPOSE_ROWS - GATE_ROWS * len(stats), n_tok), F32)
    cols = jnp.concatenate(stats + [pad], axis=0).T

    row_i = lax.broadcasted_iota(jnp.int32, (blk, blk), 0)
    col_i = lax.broadcasted_iota(jnp.int32, (blk, blk), 1)
    visible = (col_i <= row_i, col_i >= row_i)

    for h in range(M_HEADS):
        qk_cols = slice(h * M_QK, (h + 1) * M_QK)
        v_cols = slice(h * M_V, (h + 1) * M_V)
        qf = q_ref[0, :, qk_cols].astype(F32) * (M_QK ** -0.5)
        q = qf.astype(BF16)
        k = k_ref[0, :, qk_cols]
        v = v_ref[0, :, v_cols]
        for qi in range(nq):
            rows = slice(qi * blk, (qi + 1) * blk)
            h_sum = None
            for d in range(2):
                r = M_HEADS * d + h
                lo, hi = (0, (qi + 1) * blk) if d == 0 else (qi * blk, n_tok)
                dg = hi - lo - blk if d == 0 else 0
                u_col = cols[rows, r:r + 1]
                neg_mt_col = cols[rows, GATE_ROWS + r:GATE_ROWS + r + 1]
                dm = u_col + a2[r:r + 1, lo:hi]
                parts = [dm[:, :dg]] if dg > 0 else []
                parts.append(jnp.where(visible[d], dm[:, dg:dg + blk], -jnp.inf))
                if dg + blk < hi - lo:
                    parts.append(dm[:, dg + blk:])
                dm = jnp.concatenate(parts, axis=1) if len(parts) > 1 else parts[0]
                s = lax.dot_general(q[rows], k[lo:hi], NT_DIMS, preferred_element_type=F32) * jnp.exp2(dm)
                num = jnp.dot(s.astype(BF16), v[lo:hi], preferred_element_type=F32)
                den = jnp.sum(s, axis=1, keepdims=True)
                if has_state:
                    decay = jnp.exp2(u_col + m0_2[r:r + 1, :])
                    num = num + decay * lax.dot_general(q[rows], c0_ref[0, r].astype(BF16), NT_DIMS,
                                                        preferred_element_type=F32)
                    den = den + decay * jnp.sum(qf[rows] * n0_ref[0, r:r + 1, :], axis=1, keepdims=True)
                h_dir = num / jnp.maximum(jnp.abs(den), jnp.exp2(neg_mt_col))
                h_sum = h_dir if h_sum is None else h_sum + h_dir
            hn = h_sum * lax.rsqrt(jnp.mean(h_sum * h_sum, axis=-1, keepdims=True) + NORM_EPS) * nw_ref[:, v_cols]
            o_ref[0, rows, v_cols] = (hn * _sigmoid(og_ref[0, rows, v_cols].astype(F32))).astype(BF16)
        if not has_state:
            kf = k.astype(F32)
            vf = v.astype(F32)
            for d in range(2):
                r = M_HEADS * d + h
                wk_col = cols[:, 2 * GATE_ROWS + r:2 * GATE_ROWS + r + 1]
                c_out[0, r] = lax.dot_general((vf * wk_col).astype(BF16), k, TN_DIMS, preferred_element_type=F32)
                n_out[0, r:r + 1, :] = jnp.sum(kf * wk_col, axis=0, keepdims=True)


def _mlstm(z3, gate_i, gate_f, bias_i, bias_f, norm_w, states):
    bsz, n_tok, _ = z3.shape
    has_state = states is not None
    qk_w, v_w = M_HEADS * M_QK, M_HEADS * M_V
    in_specs = [
        pl.BlockSpec((1, n_tok, qk_w), lambda b: (b, 0, Z_MQ // qk_w)),
        pl.BlockSpec((1, n_tok, qk_w), lambda b: (b, 0, Z_MK // qk_w)),
        pl.BlockSpec((1, n_tok, v_w), lambda b: (b, 0, Z_MV // v_w)),
        pl.BlockSpec((1, n_tok, v_w), lambda b: (b, 0, Z_MO // v_w)),
        pl.BlockSpec((1, GATE_ROWS, n_tok), lambda b: (b, 0, 0)),
        pl.BlockSpec((1, GATE_ROWS, n_tok), lambda b: (b, 0, 0)),
        pl.BlockSpec((GATE_ROWS, 1), lambda b: (0, 0)),
        pl.BlockSpec((GATE_ROWS, 1), lambda b: (0, 0)),
        pl.BlockSpec((1, v_w), lambda b: (0, 0)),
    ]
    state_specs = [
        pl.BlockSpec((1, GATE_ROWS, M_V, M_QK), lambda b: (b, 0, 0, 0)),
        pl.BlockSpec((1, GATE_ROWS, M_QK), lambda b: (b, 0, 0)),
        pl.BlockSpec((1, GATE_ROWS, 1), lambda b: (b, 0, 0)),
    ]
    out_specs = [pl.BlockSpec((1, n_tok, v_w), lambda b: (b, 0, 0))]
    out_shape = [jax.ShapeDtypeStruct((bsz, n_tok, v_w), BF16)]
    args = [z3, z3, z3, z3, gate_i, gate_f, bias_i, bias_f, norm_w]
    if has_state:
        in_specs += state_specs
        args += list(states)
    else:
        out_specs += state_specs
        out_shape += [
            jax.ShapeDtypeStruct((bsz, GATE_ROWS, M_V, M_QK), F32),
            jax.ShapeDtypeStruct((bsz, GATE_ROWS, M_QK), F32),
            jax.ShapeDtypeStruct((bsz, GATE_ROWS, 1), F32),
        ]
    return pl.pallas_call(
        functools.partial(_mlstm_kernel, n_tok=n_tok, has_state=has_state),
        grid=(bsz,),
        in_specs=in_specs,
        out_specs=out_specs,
        out_shape=out_shape,
        compiler_params=_cparams("parallel"),
        name="mlstm",
    )(*args)


def _group_rms(x, w_row, ones_bd):
    x2 = x * x
    hi = x2.astype(BF16)
    lo = (x2 - hi.astype(F32)).astype(BF16)
    ss = jnp.dot(hi, ones_bd, preferred_element_type=F32) + jnp.dot(lo, ones_bd, preferred_element_type=F32)
    return x * lax.rsqrt(ss * (1.0 / A_HD) + NORM_EPS) * w_row


def _rope(x, cos, sin_signed, lane_lo):
    n = x.shape[-1]
    partner = jnp.where(lane_lo, pltpu.roll(x, n - 16, 1), pltpu.roll(x, 16, 1))
    return x * cos + partner * sin_signed


def _attend_block(q, sink_col, k_loc, v_loc, kinds, masks, k_ctx=None, v_ctx=None):
    s_loc = lax.dot_general(q, k_loc, NT_DIMS, preferred_element_type=F32)
    tiles = []
    for j, kind in enumerate(kinds):
        t = s_loc[:, j * WINDOW:(j + 1) * WINDOW]
        tiles.append(t if kind == "cur" else jnp.where(masks[kind], t, -jnp.inf))
    n_loc = len(tiles)
    if k_ctx is not None:
        s_ctx = lax.dot_general(q, k_ctx, NT_DIMS, preferred_element_type=F32)
        tiles += [s_ctx[:, j * WINDOW:(j + 1) * WINDOW] for j in range(k_ctx.shape[0] // WINDOW)]
    tile_max = tiles[0]
    for t in tiles[1:]:
        tile_max = jnp.maximum(tile_max, t)
    m = jnp.maximum(jnp.max(tile_max, axis=1, keepdims=True), sink_col)
    p = [jnp.exp2(t - m).astype(BF16) for t in tiles]
    acc = jnp.dot(jnp.concatenate(p[:n_loc], axis=1), v_loc, preferred_element_type=F32)
    if k_ctx is not None:
        acc = acc + jnp.dot(jnp.concatenate(p[n_loc:], axis=1), v_ctx, preferred_element_type=F32)
    den = acc[:, A_HD:A_HD + 1] + jnp.exp2(sink_col - m)
    return acc[:, :A_HD] / den


def _attn_kernel(*refs, n_tok, latent):
    if latent:
        (sink_ref, q_ref, k_ref, v_ref, qw_ref, kw_ref, cos_ref, sin_ref, ck_ref, cv_ref,
         o_ref, qh_ref, kg_ref, va_ref) = refs
    else:
        (sink_ref, q_ref, k_ref, v_ref, qw_ref, kw_ref,
         o_ref, ko_ref, vo_ref) = refs

    lane = lax.broadcasted_iota(jnp.int32, (1, A_KV_W), 1)
    lane_lo = (lane % 32) < 16
    bd_r = lax.broadcasted_iota(jnp.int32, (A_KV_W, A_KV_W), 0) // A_HD
    bd_c = lax.broadcasted_iota(jnp.int32, (A_KV_W, A_KV_W), 1) // A_HD
    ones_bd = jnp.where(bd_r == bd_c, 1.0, 0.0).astype(BF16)

    k_all = _group_rms(k_ref[0].astype(F32), kw_ref[...], ones_bd)
    v_all = v_ref[0]
    if latent:
        k_all = _rope(k_all, cos_ref[...], sin_ref[...], lane_lo)
    else:
        ko_ref[0] = k_all
        vo_ref[0] = v_all.astype(F32)

    q_blk = WINDOW if latent else n_tok
    m_rows = A_GROUP * q_blk
    head_of_row = lax.broadcasted_iota(jnp.int32, (m_rows, 1), 0) // q_blk
    r_in = lax.broadcasted_iota(jnp.int32, (m_rows, WINDOW), 0) % WINDOW
    c_in = lax.broadcasted_iota(jnp.int32, (m_rows, WINDOW), 1)
    masks = {"prev": c_in >= r_in, "next": c_in <= r_in}
    ones_v = jnp.ones((n_tok, A_HD), BF16)

    def sink_column(g):
        col = jnp.zeros((m_rows, 1), F32)
        for a in range(A_GROUP):
            col = jnp.where(head_of_row == a, sink_ref[A_GROUP * g + a] * LOG2E, col)
        return col

    def unstack(o):
        return jnp.concatenate([o[a * q_blk:(a + 1) * q_blk, :] for a in range(A_GROUP)], axis=1).astype(BF16)

    for g in range(A_KV):
        g_cols = slice(A_KV_W * g, A_KV_W * (g + 1))
        h_cols = slice(A_HD * g, A_HD * (g + 1))
        qg = _group_rms(q_ref[0, :, g_cols].astype(F32), qw_ref[...], ones_bd)
        if latent:
            qg = _rope(qg, cos_ref[...], sin_ref[...], lane_lo)
        qg = qg * (A_HD ** -0.5 * LOG2E)
        kg = k_all[:, h_cols].astype(BF16)
        va = jnp.concatenate([v_all[:, h_cols], ones_v], axis=1)
        sink_col = sink_column(g)
        if not latent:
            q_stack = jnp.concatenate([qg[:, A_HD * a:A_HD * (a + 1)] for a in range(A_GROUP)], axis=0)
            o = _attend_block(q_stack.astype(BF16), sink_col, kg, va, ["cur"] * (n_tok // WINDOW), masks)
            o_ref[0, :, g_cols] = unstack(o)
            continue

        for a in range(A_GROUP):
            qh_ref[a] = qg[:, A_HD * a:A_HD * (a + 1)].astype(BF16)
        kg_ref[...] = kg
        va_ref[...] = va
        n_ctx = ck_ref.shape[1]
        ckg = ck_ref[0][:, h_cols].astype(BF16)
        cva = jnp.concatenate([cv_ref[0][:, h_cols].astype(BF16), jnp.ones((n_ctx, A_HD), BF16)], axis=1)
        nb = n_tok // WINDOW

        def q_block(i, kinds, k_start, g_cols=g_cols, sink_col=sink_col, ckg=ckg, cva=cva, qg=qg, kg=kg, va=va):
            n_keys = len(kinds) * WINDOW
            if isinstance(i, int):
                r0 = i * WINDOW
                q_stack = jnp.concatenate([qg[r0:r0 + WINDOW, A_HD * a:A_HD * (a + 1)] for a in range(A_GROUP)],
                                          axis=0).astype(BF16)
                k_loc = kg[k_start:k_start + n_keys]
                v_loc = va[k_start:k_start + n_keys]
            else:
                r0 = pl.multiple_of(i * WINDOW, WINDOW)
                k_start = pl.multiple_of(k_start, WINDOW)
                q_stack = jnp.concatenate([qh_ref[a, pl.ds(r0, WINDOW), :] for a in range(A_GROUP)], axis=0)
                k_loc = kg_ref[pl.ds(k_start, n_keys), :]
                v_loc = va_ref[pl.ds(k_start, n_keys), :]
            o = _attend_block(q_stack, sink_col, k_loc, v_loc, kinds, masks, ckg, cva)
            o_ref[0, pl.ds(r0, WINDOW), g_cols] = unstack(o)

        def interior(i, carry, q_block=q_block):
            q_block(i, ["prev", "cur", "next"], (i - 1) * WINDOW)
            return carry

        lax.fori_loop(1, nb - 1, interior, 0, unroll=3)
        q_block(0, ["cur", "next"], 0)
        q_block(nb - 1, ["prev", "cur"], (nb - 2) * WINDOW)


def _attention(z3, kv3, sink, q_norm_w, k_norm_w, rope_tabs, cache):
    bsz, n_tok, _ = z3.shape
    latent = cache is not None
    in_specs = [
        pl.BlockSpec(memory_space=pltpu.SMEM),
        pl.BlockSpec((1, n_tok, A_Q_W), lambda b: (b, 0, Z_AQ // A_Q_W)),
        pl.BlockSpec((1, n_tok, A_KV_W), lambda b: (b, 0, 0)),
        pl.BlockSpec((1, n_tok, A_KV_W), lambda b: (b, 0, 1)),
        pl.BlockSpec((1, A_KV_W), lambda b: (0, 0)),
        pl.BlockSpec((1, A_KV_W), lambda b: (0, 0)),
    ]
    args = [sink, z3, kv3, kv3, q_norm_w, k_norm_w]
    out_specs = [pl.BlockSpec((1, n_tok, A_Q_W), lambda b: (b, 0, 0))]
    out_shape = [jax.ShapeDtypeStruct((bsz, n_tok, A_Q_W), BF16)]
    scratch = []
    if latent:
        assert n_tok // WINDOW >= 3
        n_ctx = cache[0].shape[1]
        in_specs += [
            pl.BlockSpec((n_tok, A_KV_W), lambda b: (0, 0)),
            pl.BlockSpec((n_tok, A_KV_W), lambda b: (0, 0)),
            pl.BlockSpec((1, n_ctx, A_KV_W), lambda b: (b, 0, 0)),
            pl.BlockSpec((1, n_ctx, A_KV_W), lambda b: (b, 0, 0)),
        ]
        args += [rope_tabs[0], rope_tabs[1], cache[0], cache[1]]
        scratch = [
            pltpu.VMEM((A_GROUP, n_tok, A_HD), BF16),
            pltpu.VMEM((n_tok, A_HD), BF16),
            pltpu.VMEM((n_tok, 2 * A_HD), BF16),
        ]
    else:
        out_specs += [pl.BlockSpec((1, n_tok, A_KV_W), lambda b: (b, 0, 0))] * 2
        out_shape += [jax.ShapeDtypeStruct((bsz, n_tok, A_KV_W), F32)] * 2
    return pl.pallas_call(
        functools.partial(_attn_kernel, n_tok=n_tok, latent=latent),
        grid=(bsz,),
        in_specs=in_specs,
        out_specs=out_specs,
        out_shape=out_shape,
        scratch_shapes=scratch,
        compiler_params=_cparams("parallel"),
        name="attention",
    )(*args)


def _merge_kernel(x_ref, mod_ref, hm_ref, ha_ref, gm_ref, ga_ref, wpm_ref, wpa_ref, wo_ref, o_ref):
    pm = jnp.dot(hm_ref[...], wpm_ref[...], preferred_element_type=F32)
    pa = jnp.dot(ha_ref[...], wpa_ref[...], preferred_element_type=F32)
    u = _sigmoid(gm_ref[...].astype(F32)) * pm + _sigmoid(ga_ref[...].astype(F32)) * pa
    mix = jnp.dot(u.astype(BF16), wo_ref[...], preferred_element_type=F32)
    o_ref[...] = x_ref[...] + mod_ref[0, 5:6, :] * mix


def _merge(x, mod3, row_of_tile, hm, ha, z, w_proj_m, w_proj_a, w_out, tm):
    n_tok, d = x.shape
    resident = functools.partial(pl.BlockSpec, pipeline_mode=pl.Buffered(1))
    return pl.pallas_call(
        _merge_kernel,
        grid=(n_tok // tm,),
        in_specs=[
            pl.BlockSpec((tm, d), lambda i: (i, 0)),
            pl.BlockSpec((1, N_MOD, d), lambda i: (row_of_tile(i, tm), 0, 0)),
            pl.BlockSpec((tm, M_HEADS * M_V), lambda i: (i, 0)),
            pl.BlockSpec((tm, A_Q_W), lambda i: (i, 0)),
            pl.BlockSpec((tm, d), lambda i: (i, Z_GM // D_MODEL)),
            pl.BlockSpec((tm, d), lambda i: (i, Z_GA // D_MODEL)),
            resident((M_HEADS * M_V, d), lambda i: (0, 0)),
            resident((A_Q_W, d), lambda i: (0, 0)),
            resident((d, d), lambda i: (0, 0)),
        ],
        out_specs=pl.BlockSpec((tm, d), lambda i: (i, 0)),
        out_shape=jax.ShapeDtypeStruct((n_tok, d), F32),
        compiler_params=_cparams("parallel"),
        name="merge",
    )(x, mod3, hm, ha, z, z, w_proj_m, w_proj_a, w_out)


def _rope_tables(n_tok):
    nf = A_HD // 4
    inv = ROPE_THETA ** (-jnp.arange(nf, dtype=F32) / nf)
    tok = jnp.arange(n_tok)
    pos = jnp.stack([tok // GRID_W, tok % GRID_W], axis=1).astype(F32)
    ang = pos[:, :, None] * inv
    cos = jnp.cos(ang)
    sin = jnp.sin(ang)
    cos_h = jnp.concatenate([cos, cos], axis=-1).reshape(n_tok, A_HD)
    sin_h = jnp.concatenate([-sin, sin], axis=-1).reshape(n_tok, A_HD)
    return jnp.tile(cos_h, (1, A_KV)), jnp.tile(sin_h, (1, A_KV))


TM_FFN = 512
TM_PROJ = 1024
TM_MERGE = 256


def _trunk(x, mod3, row_of_tile, wts, states, rope_tabs, cache):
    bsz, n_tok, d = x.shape
    x2 = x.reshape(bsz * n_tok, d)
    z, gates, kv = _mixer_proj(x2, mod3, row_of_tile, wts["norm_mix"], wts["w_main"], wts["w_aux"], TM_PROJ)
    z3 = z.reshape(bsz, n_tok, Z_COLS)
    g = jnp.transpose(gates[:, :M_GATES].reshape(bsz, n_tok, 2, 2, M_HEADS), (0, 2, 3, 4, 1))
    gate_i = g[:, :, 0].reshape(bsz, GATE_ROWS, n_tok)
    gate_f = g[:, :, 1].reshape(bsz, GATE_ROWS, n_tok)
    m_out = _mlstm(z3, gate_i, gate_f, wts["bias_i"], wts["bias_f"], wts["mlstm_norm"], states)
    a_out = _attention(z3, kv.reshape(bsz, n_tok, 2 * A_KV_W), wts["sink"], wts["q_norm"], wts["k_norm"],
                       rope_tabs, cache)
    hm = m_out[0].reshape(bsz * n_tok, M_HEADS * M_V)
    ha = a_out[0].reshape(bsz * n_tok, A_Q_W)
    x2 = _merge(x2, mod3, row_of_tile, hm, ha, z, wts["w_proj_m"], wts["w_proj_a"], wts["w_out"], TM_MERGE)
    x2 = _ffn(x2, mod3, row_of_tile, wts["norm2"], wts["wi2"], wts["wo2"], 6, TM_FFN)
    return x2.reshape(bsz, n_tok, d), m_out[1:], a_out[1:]


def kernel(x_prompt, x_sample, cache_attn_k, cache_attn_v, state_mlstm_C, state_mlstm_n, state_mlstm_m, c, c_ctx, ada_w, ada_b, norm_ffn1_w, ffn1_wi, ffn1_wo, norm_mix_w, w_in, mlstm_gate_b, mlstm_norm_w, attn_q_norm_w, attn_k_norm_w, attn_sink, w_proj_m, w_proj_a, w_out, norm_ffn2_w, ffn2_wi, ffn2_wo):
    bp, tp, d = x_prompt.shape
    bs, ts, _ = x_sample.shape
    n_ctx = cache_attn_k.shape[2]
    l = 0

    n_rows = 16
    cond = jnp.concatenate([c_ctx[None, :], c, jnp.zeros((n_rows - 1 - bs, d), F32)], axis=0)
    mod3 = _modulation(cond, ada_w[l], ada_b[l][None, :]).reshape(n_rows, N_MOD, d)

    w_main, w_aux = _regroup_w_in(w_in, l)
    gate_b = mlstm_gate_b[l].reshape(2, 2, M_HEADS)
    norm1, wi1, wo1 = norm_ffn1_w[l][None, :], ffn1_wi[l].astype(BF16), ffn1_wo[l].astype(BF16)
    ctx_row = lambda i, tm: 0
    lat_row = lambda i, tm: 1 + (i * tm) // ts

    x1_prompt = _ffn(x_prompt.reshape(bp * tp, d), mod3, ctx_row, norm1, wi1, wo1, 0, TM_FFN).reshape(bp, tp, d)
    x1_sample, wi2, wo2, w_out_b, w_pm_b, w_pa_b = _ffn(
        x_sample.reshape(bs * ts, d), mod3, lat_row, norm1, wi1, wo1, 0, TM_FFN,
        side=(ffn2_wi[l], ffn2_wo[l], w_out[l], w_proj_m[l], w_proj_a[l]))
    x1_sample = x1_sample.reshape(bs, ts, d)
    wts = dict(
        norm_mix=norm_mix_w[l][None, :], w_main=w_main, w_aux=w_aux,
        bias_i=gate_b[:, 0].reshape(GATE_ROWS, 1), bias_f=gate_b[:, 1].reshape(GATE_ROWS, 1),
        mlstm_norm=mlstm_norm_w[l][None, :],
        sink=attn_sink[l], q_norm=jnp.tile(attn_q_norm_w[l], A_KV)[None, :],
        k_norm=jnp.tile(attn_k_norm_w[l], A_KV)[None, :],
        w_proj_m=w_pm_b, w_proj_a=w_pa_b, w_out=w_out_b,
        norm2=norm_ffn2_w[l][None, :], wi2=wi2, wo2=wo2,
    )

    y_prompt, (c_new, n_new, m_new), (k_new, v_new) = _trunk(x1_prompt, mod3, ctx_row, wts, None, None, None)
    new_attn_k = k_new.reshape(bp, 1, tp, A_KV, A_HD)
    new_attn_v = v_new.reshape(bp, 1, tp, A_KV, A_HD)
    new_c = c_new.reshape(bp, 1, 2, M_HEADS, M_V, M_QK)
    new_n = n_new.reshape(bp, 1, 2, M_HEADS, M_QK)
    new_m = m_new.reshape(bp, 1, 2, M_HEADS)

    states = (state_mlstm_C[:, l].reshape(bs, GATE_ROWS, M_V, M_QK),
              state_mlstm_n[:, l].reshape(bs, GATE_ROWS, M_QK),
              state_mlstm_m[:, l].reshape(bs, GATE_ROWS, 1))
    cache = (cache_attn_k[:, l].reshape(bs, n_ctx, A_KV_W), cache_attn_v[:, l].reshape(bs, n_ctx, A_KV_W))
    y_sample, _, _ = _trunk(x1_sample, mod3, lat_row, wts, states, _rope_tables(ts), cache)

    return (y_prompt, y_sample, new_attn_k, new_attn_v, new_c, new_n, new_m)
```

```python
import functools

import jax
import jax.numpy as jnp
from jax import lax
from jax.experimental import pallas as pl
from jax.experimental.pallas import tpu as pltpu

F32 = jnp.float32
BF16 = jnp.bfloat16
LOG2E = 1.4426950408889634

D_MODEL = 2048
D_FF = 5632
N_MOD = 9
NORM_EPS = 1e-6
GRID_W = 64
ROPE_THETA = 10000.0
M_HEADS = 4
M_QK = 128
M_V = 256
M_GATES = 4 * M_HEADS
A_HEADS = 16
A_KV = 4
A_GROUP = A_HEADS // A_KV
A_HD = 64
A_Q_W = A_HEADS * A_HD
A_KV_W = A_KV * A_HD
WINDOW = 128

Z_GM, Z_GA, Z_MV, Z_MO, Z_AQ, Z_MQ, Z_MK = 0, 2048, 4096, 5120, 6144, 7168, 7680
Z_COLS = 8192
GATE_LANES = 128
AUX_COLS = GATE_LANES + 2 * A_KV_W
GATE_ROWS = 2 * M_HEADS

FFN_TF = 512
PROJ_TN = 1024
MLSTM_BLOCK = 256
XPOSE_ROWS = 128

VMEM_LIMIT = 56 * 1024 * 1024
NT_DIMS = (((1,), (1,)), ((), ()))
TN_DIMS = (((0,), (0,)), ((), ()))


def _cparams(*sem):
    return pltpu.CompilerParams(dimension_semantics=sem, vmem_limit_bytes=VMEM_LIMIT)


def _sigmoid(x):
    return 1.0 / (1.0 + jnp.exp(-x))


def _norm_modulate(x, norm_w, shift, scale):
    ms = jnp.mean(x * x, axis=-1, keepdims=True)
    y = x * lax.rsqrt(ms + NORM_EPS) * norm_w
    return y * (1.0 + scale) + shift


def _mod_kernel(c_ref, w_ref, b_ref, o_ref):
    c = c_ref[...]
    s = (c * _sigmoid(c)).astype(BF16)
    o_ref[...] = jnp.dot(s, w_ref[...].astype(BF16), preferred_element_type=F32) + b_ref[...]


def _modulation(cond, ada_w, ada_b, tn=1024):
    rows, d = cond.shape
    n = ada_w.shape[1]
    return pl.pallas_call(
        _mod_kernel,
        grid=(n // tn,),
        in_specs=[
            pl.BlockSpec((rows, d), lambda j: (0, 0)),
            pl.BlockSpec((d, tn), lambda j: (0, j)),
            pl.BlockSpec((1, tn), lambda j: (0, j)),
        ],
        out_specs=pl.BlockSpec((rows, tn), lambda j: (0, j)),
        out_shape=jax.ShapeDtypeStruct((rows, n), F32),
        compiler_params=_cparams("arbitrary"),
        name="modulation",
    )(cond, ada_w, ada_b)


SIDE_ROWS = 16


FFN_SLOTS = 4


def _ffn_kernel(x_ref, mod_ref, nw_ref, wi_hbm, wo_hbm, *rest, k_shift, n_side, nf, tf):
    side_in, o_ref, side_out = rest[:n_side], rest[n_side], rest[n_side + 1:2 * n_side + 1]
    h_ref, wg_buf, wu_buf, wo_buf, sem = rest[2 * n_side + 1:]
    i = pl.program_id(0)
    j = pl.program_id(1)
    n_steps = pl.num_programs(0) * nf
    s = i * nf + j

    def chunk_copies(step, slot):
        c = step % nf
        g_cols = pl.ds(pl.multiple_of(c * tf, tf), tf)
        u_cols = pl.ds(pl.multiple_of((c + nf) * tf, tf), tf)
        return (pltpu.make_async_copy(wi_hbm.at[:, g_cols], wg_buf.at[slot], sem.at[0, slot]),
                pltpu.make_async_copy(wi_hbm.at[:, u_cols], wu_buf.at[slot], sem.at[1, slot]),
                pltpu.make_async_copy(wo_hbm.at[g_cols, :], wo_buf.at[slot], sem.at[2, slot]))

    @pl.when(s == 0)
    def _():
        for first in range(FFN_SLOTS - 1):
            for cp in chunk_copies(first, first):
                cp.start()

    ahead = s + FFN_SLOTS - 1

    @pl.when(ahead < n_steps)
    def _():
        for cp in chunk_copies(ahead, ahead % FFN_SLOTS):
            cp.start()

    for src, dst in zip(side_in, side_out):
        dst[...] = src[...].astype(BF16)

    @pl.when(j == 0)
    def _():
        h = _norm_modulate(x_ref[...], nw_ref[...], mod_ref[0, k_shift:k_shift + 1, :],
                           mod_ref[0, k_shift + 1:k_shift + 2, :])
        h_ref[...] = h.astype(BF16)
        o_ref[...] = x_ref[...]

    slot = s % FFN_SLOTS
    for cp in chunk_copies(s, slot):
        cp.wait()

    h = h_ref[...]
    g = jnp.dot(h, wg_buf[slot], preferred_element_type=F32)
    u = jnp.dot(h, wu_buf[slot], preferred_element_type=F32)
    a = (g * _sigmoid(g) * u).astype(BF16)
    half_gate = 0.5 * mod_ref[0, k_shift + 2:k_shift + 3, :]
    o_ref[...] += half_gate * jnp.dot(a, wo_buf[slot], preferred_element_type=F32)


def _ffn(x, mod3, row_of_tile, norm_w, wi, wo, k_shift, tm, side=()):
    n_tok, d = x.shape
    tf = FFN_TF
    nf = D_FF // tf
    n_steps = (n_tok // tm) * nf
    side_specs = []
    for arr in side:
        r, c = arr.shape
        rows = SIDE_ROWS * pl.cdiv(r, SIDE_ROWS * n_steps)
        assert r % rows == 0
        last = r // rows - 1
        side_specs.append(pl.BlockSpec((rows, c), lambda i, j, last=last: (jnp.minimum(i * nf + j, last), 0)))
    assert n_steps >= FFN_SLOTS - 1
    out = pl.pallas_call(
        functools.partial(_ffn_kernel, k_shift=k_shift, n_side=len(side), nf=nf, tf=tf),
        grid=(n_tok // tm, nf),
        in_specs=[
            pl.BlockSpec((tm, d), lambda i, j: (i, 0)),
            pl.BlockSpec((1, N_MOD, d), lambda i, j: (row_of_tile(i, tm), 0, 0)),
            pl.BlockSpec((1, d), lambda i, j: (0, 0)),
            pl.BlockSpec(memory_space=pl.ANY),
            pl.BlockSpec(memory_space=pl.ANY),
        ] + side_specs,
        out_specs=[pl.BlockSpec((tm, d), lambda i, j: (i, 0))] + side_specs,
        out_shape=[jax.ShapeDtypeStruct((n_tok, d), F32)] + [jax.ShapeDtypeStruct(a.shape, BF16) for a in side],
        scratch_shapes=[
            pltpu.VMEM((tm, d), BF16),
            pltpu.VMEM((FFN_SLOTS, d, tf), BF16),
            pltpu.VMEM((FFN_SLOTS, d, tf), BF16),
            pltpu.VMEM((FFN_SLOTS, tf, d), BF16),
            pltpu.SemaphoreType.DMA((3, FFN_SLOTS)),
        ],
        compiler_params=_cparams("arbitrary", "arbitrary"),
        name="ffn",
    )(x, mod3, norm_w, wi, wo, *side)
    return out if side else out[0]


W_IN_SEGS = (("mq", 512), ("mk", 512), ("mv", 1024), ("mo", 1024), ("mg", M_GATES), ("aq", 1024),
             ("ak", 256), ("av", 256), ("gm", 2048), ("ga", 2048))
W_IN_OFF = {}
for _name, _size in W_IN_SEGS:
    W_IN_OFF[_name] = (sum(s for _, s in W_IN_SEGS[:len(W_IN_OFF)]), _size)
IN_COLS = sum(s for _, s in W_IN_SEGS)
MAIN_ORDER = ("gm", "ga", "mv", "mo", "aq", "mq", "mk")
REGROUP_ROWS = 128


def _regroup_kernel(w_ref, main_ref, aux_ref):
    w = w_ref[0]
    t0 = W_IN_OFF["aq"][0]
    tail = w[:, t0:]

    def seg(name):
        start, size = W_IN_OFF[name]
        return (tail[:, start - t0:start - t0 + size] if start >= t0 else w[:, start:start + size]).astype(BF16)

    col = 0
    for name in MAIN_ORDER:
        size = W_IN_OFF[name][1]
        main_ref[:, col:col + size] = seg(name)
        col += size
    g0 = W_IN_OFF["mg"][0]
    lane = lax.broadcasted_iota(jnp.int32, (1, GATE_LANES), 1)
    aux_ref[:, :GATE_LANES] = jnp.where(lane < M_GATES, w[:, g0:g0 + GATE_LANES], 0.0).astype(BF16)
    aux_ref[:, GATE_LANES:GATE_LANES + A_KV_W] = seg("ak")
    aux_ref[:, GATE_LANES + A_KV_W:] = seg("av")


def _regroup_w_in(w, l):
    _, d, n = w.shape
    assert n == IN_COLS and W_IN_OFF["mg"][0] % GATE_LANES == 0
    return pl.pallas_call(
        _regroup_kernel,
        grid=(d // REGROUP_ROWS,),
        in_specs=[pl.BlockSpec((1, REGROUP_ROWS, n), lambda i: (l, i, 0))],
        out_specs=[
            pl.BlockSpec((REGROUP_ROWS, Z_COLS), lambda i: (i, 0)),
            pl.BlockSpec((REGROUP_ROWS, AUX_COLS), lambda i: (i, 0)),
        ],
        out_shape=[jax.ShapeDtypeStruct((d, Z_COLS), BF16), jax.ShapeDtypeStruct((d, AUX_COLS), BF16)],
        compiler_params=_cparams("parallel"),
        name="regroup_w_in",
    )(w)


def _proj_kernel(x_ref, mod_ref, nw_ref, w_ref, waux_ref, z_ref, g_ref, kv_ref, h_ref):
    j = pl.program_id(1)

    @pl.when(j == 0)
    def _():
        h = _norm_modulate(x_ref[...], nw_ref[...], mod_ref[0, 3:4, :], mod_ref[0, 4:5, :]).astype(BF16)
        h_ref[...] = h
        aux = jnp.dot(h, waux_ref[...], preferred_element_type=F32)
        g_ref[...] = aux[:, :GATE_LANES]
        kv_ref[...] = aux[:, GATE_LANES:].astype(BF16)

    half = z_ref.shape[1] // 2
    h = h_ref[...]
    z_ref[:, :half] = jnp.dot(h, w_ref[:, :half], preferred_element_type=F32).astype(BF16)
    z_ref[:, half:] = jnp.dot(h, w_ref[:, half:], preferred_element_type=F32).astype(BF16)


def _mixer_proj(x, mod3, row_of_tile, norm_w, w_main, w_aux, tm):
    n_tok, d = x.shape
    tn = PROJ_TN
    once = functools.partial(pl.BlockSpec, pipeline_mode=pl.Buffered(1))
    return pl.pallas_call(
        _proj_kernel,
        grid=(n_tok // tm, Z_COLS // tn),
        in_specs=[
            pl.BlockSpec((tm, d), lambda i, j: (i, 0)),
            pl.BlockSpec((1, N_MOD, d), lambda i, j: (row_of_tile(i, tm), 0, 0)),
            pl.BlockSpec((1, d), lambda i, j: (0, 0)),
            pl.BlockSpec((d, tn), lambda i, j: (0, j)),
            once((d, AUX_COLS), lambda i, j: (0, 0)),
        ],
        out_specs=[
            pl.BlockSpec((tm, tn), lambda i, j: (i, j)),
            pl.BlockSpec((tm, GATE_LANES), lambda i, j: (i, 0)),
            pl.BlockSpec((tm, 2 * A_KV_W), lambda i, j: (i, 0)),
        ],
        out_shape=[
            jax.ShapeDtypeStruct((n_tok, Z_COLS), BF16),
            jax.ShapeDtypeStruct((n_tok, GATE_LANES), F32),
            jax.ShapeDtypeStruct((n_tok, 2 * A_KV_W), BF16),
        ],
        scratch_shapes=[pltpu.VMEM((tm, d), BF16)],
        compiler_params=_cparams("parallel", "arbitrary"),
        name="mixer_proj",
    )(x, mod3, norm_w, w_main, w_aux)


def _scan_lanes(x, op, fill, reverse):
    n = x.shape[-1]
    lane = lax.broadcasted_iota(jnp.int32, x.shape, 1)
    s = 1
    while s < n:
        if reverse:
            x = op(x, jnp.where(lane < n - s, pltpu.roll(x, n - s, 1), fill))
        else:
            x = op(x, jnp.where(lane >= s, pltpu.roll(x, s, 1), fill))
        s *= 2
    return x


def _dir_scan(x, op, fill, fwd_rows):
    return jnp.where(fwd_rows, _scan_lanes(x, op, fill, False), _scan_lanes(x, op, fill, True))


def _log_sigmoid(x):
    return jnp.minimum(x, 0.0) - jnp.log1p(jnp.exp(-jnp.abs(x)))


def _mlstm_kernel(*refs, n_tok, has_state):
    if has_state:
        (q_ref, k_ref, v_ref, og_ref, gi_ref, gf_ref, bi_ref, bf_ref, nw_ref, c0_ref, n0_ref, m0_ref,
         o_ref) = refs
    else:
        (q_ref, k_ref, v_ref, og_ref, gi_ref, gf_ref, bi_ref, bf_ref, nw_ref,
         o_ref, c_out, n_out, m_out) = refs
    blk = min(MLSTM_BLOCK, n_tok)
    nq = n_tok // blk

    fwd_rows = lax.broadcasted_iota(jnp.int32, (GATE_ROWS, 1), 0) < M_HEADS
    ig = gi_ref[0] + bi_ref[...]
    cum_f = _dir_scan(_log_sigmoid(gf_ref[0] + bf_ref[...]), jnp.add, 0.0, fwd_rows)
    a = ig - cum_f
    m0 = m0_ref[0] if has_state else jnp.zeros((GATE_ROWS, 1), F32)
    mx = jnp.maximum(_dir_scan(a, jnp.maximum, -jnp.inf, fwd_rows), m0)
    stats = [-mx * LOG2E, -(cum_f + mx) * LOG2E]
    a2 = a * LOG2E
    m0_2 = m0 * LOG2E
    if not has_state:
        mx_end = jnp.where(fwd_rows, mx[:, n_tok - 1:n_tok], mx[:, 0:1])
        cum_end = jnp.where(fwd_rows, cum_f[:, n_tok - 1:n_tok], cum_f[:, 0:1])
        stats.append(jnp.exp(a - mx_end))
        m_out[0] = cum_end + mx_end
    pad = jnp.zeros((XPOSE_ROWS - GATE_ROWS * len(stats), n_tok), F32)
    cols = jnp.concatenate(stats + [pad], axis=0).T

    row_i = lax.broadcasted_iota(jnp.int32, (blk, blk), 0)
    col_i = lax.broadcasted_iota(jnp.int32, (blk, blk), 1)
    visible = (col_i <= row_i, col_i >= row_i)

    for h in range(M_HEADS):
        qk_cols = slice(h * M_QK, (h + 1) * M_QK)
        v_cols = slice(h * M_V, (h + 1) * M_V)
        qf = q_ref[0, :, qk_cols].astype(F32) * (M_QK ** -0.5)
        q = qf.astype(BF16)
        k = k_ref[0, :, qk_cols]
        v = v_ref[0, :, v_cols]
        for qi in range(nq):
            rows = slice(qi * blk, (qi + 1) * blk)
            h_sum = None
            for d in range(2):
                r = M_HEADS * d + h
                lo, hi = (0, (qi + 1) * blk) if d == 0 else (qi * blk, n_tok)
                dg = hi - lo - blk if d == 0 else 0
                u_col = cols[rows, r:r + 1]
                neg_mt_col = cols[rows, GATE_ROWS + r:GATE_ROWS + r + 1]
                dm = u_col + a2[r:r + 1, lo:hi]
                parts = [dm[:, :dg]] if dg > 0 else []
                parts.append(jnp.where(visible[d], dm[:, dg:dg + blk], -jnp.inf))
                if dg + blk < hi - lo:
                    parts.append(dm[:, dg + blk:])
                dm = jnp.concatenate(parts, axis=1) if len(parts) > 1 else parts[0]
                s = lax.dot_general(q[rows], k[lo:hi], NT_DIMS, preferred_element_type=F32) * jnp.exp2(dm)
                num = jnp.dot(s.astype(BF16), v[lo:hi], preferred_element_type=F32)
                den = jnp.sum(s, axis=1, keepdims=True)
                if has_state:
                    decay = jnp.exp2(u_col + m0_2[r:r + 1, :])
                    num = num + decay * lax.dot_general(q[rows], c0_ref[0, r].astype(BF16), NT_DIMS,
                                                        preferred_element_type=F32)
                    den = den + decay * jnp.sum(qf[rows] * n0_ref[0, r:r + 1, :], axis=1, keepdims=True)
                h_dir = num / jnp.maximum(jnp.abs(den), jnp.exp2(neg_mt_col))
                h_sum = h_dir if h_sum is None else h_sum + h_dir
            hn = h_sum * lax.rsqrt(jnp.mean(h_sum * h_sum, axis=-1, keepdims=True) + NORM_EPS) * nw_ref[:, v_cols]
            o_ref[0, rows, v_cols] = (hn * _sigmoid(og_ref[0, rows, v_cols].astype(F32))).astype(BF16)
        if not has_state:
            kf = k.astype(F32)
            vf = v.astype(F32)
            for d in range(2):
                r = M_HEADS * d + h
                wk_col = cols[:, 2 * GATE_ROWS + r:2 * GATE_ROWS + r + 1]
                c_out[0, r] = lax.dot_general((vf * wk_col).astype(BF16), k, TN_DIMS, preferred_element_type=F32)
                n_out[0, r:r + 1, :] = jnp.sum(kf * wk_col, axis=0, keepdims=True)


def _mlstm(z3, gate_i, gate_f, bias_i, bias_f, norm_w, states):
    bsz, n_tok, _ = z3.shape
    has_state = states is not None
    qk_w, v_w = M_HEADS * M_QK, M_HEADS * M_V
    in_specs = [
        pl.BlockSpec((1, n_tok, qk_w), lambda b: (b, 0, Z_MQ // qk_w)),
        pl.BlockSpec((1, n_tok, qk_w), lambda b: (b, 0, Z_MK // qk_w)),
        pl.BlockSpec((1, n_tok, v_w), lambda b: (b, 0, Z_MV // v_w)),
        pl.BlockSpec((1, n_tok, v_w), lambda b: (b, 0, Z_MO // v_w)),
        pl.BlockSpec((1, GATE_ROWS, n_tok), lambda b: (b, 0, 0)),
        pl.BlockSpec((1, GATE_ROWS, n_tok), lambda b: (b, 0, 0)),
        pl.BlockSpec((GATE_ROWS, 1), lambda b: (0, 0)),
        pl.BlockSpec((GATE_ROWS, 1), lambda b: (0, 0)),
        pl.BlockSpec((1, v_w), lambda b: (0, 0)),
    ]
    state_specs = [
        pl.BlockSpec((1, GATE_ROWS, M_V, M_QK), lambda b: (b, 0, 0, 0)),
        pl.BlockSpec((1, GATE_ROWS, M_QK), lambda b: (b, 0, 0)),
        pl.BlockSpec((1, GATE_ROWS, 1), lambda b: (b, 0, 0)),
    ]
    out_specs = [pl.BlockSpec((1, n_tok, v_w), lambda b: (b, 0, 0))]
    out_shape = [jax.ShapeDtypeStruct((bsz, n_tok, v_w), BF16)]
    args = [z3, z3, z3, z3, gate_i, gate_f, bias_i, bias_f, norm_w]
    if has_state:
        in_specs += state_specs
        args += list(states)
    else:
        out_specs += state_specs
        out_shape += [
            jax.ShapeDtypeStruct((bsz, GATE_ROWS, M_V, M_QK), F32),
            jax.ShapeDtypeStruct((bsz, GATE_ROWS, M_QK), F32),
            jax.ShapeDtypeStruct((bsz, GATE_ROWS, 1), F32),
        ]
    return pl.pallas_call(
        functools.partial(_mlstm_kernel, n_tok=n_tok, has_state=has_state),
        grid=(bsz,),
        in_specs=in_specs,
        out_specs=out_specs,
        out_shape=out_shape,
        compiler_params=_cparams("parallel"),
        name="mlstm",
    )(*args)


def _group_rms(x, w_row, ones_bd):
    x2 = x * x
    hi = x2.astype(BF16)
    lo = (x2 - hi.astype(F32)).astype(BF16)
    ss = jnp.dot(hi, ones_bd, preferred_element_type=F32) + jnp.dot(lo, ones_bd, preferred_element_type=F32)
    return x * lax.rsqrt(ss * (1.0 / A_HD) + NORM_EPS) * w_row


def _rope(x, cos, sin_signed, lane_lo):
    n = x.shape[-1]
    partner = jnp.where(lane_lo, pltpu.roll(x, n - 16, 1), pltpu.roll(x, 16, 1))
    return x * cos + partner * sin_signed


def _attend_block(q, sink_col, k_loc, v_loc, kinds, masks, k_ctx=None, v_ctx=None):
    s_loc = lax.dot_general(q, k_loc, NT_DIMS, preferred_element_type=F32)
    tiles = []
    for j, kind in enumerate(kinds):
        t = s_loc[:, j * WINDOW:(j + 1) * WINDOW]
        tiles.append(t if kind == "cur" else jnp.where(masks[kind], t, -jnp.inf))
    n_loc = len(tiles)
    if k_ctx is not None:
        s_ctx = lax.dot_general(q, k_ctx, NT_DIMS, preferred_element_type=F32)
        tiles += [s_ctx[:, j * WINDOW:(j + 1) * WINDOW] for j in range(k_ctx.shape[0] // WINDOW)]
    tile_max = tiles[0]
    for t in tiles[1:]:
        tile_max = jnp.maximum(tile_max, t)
    m = jnp.maximum(jnp.max(tile_max, axis=1, keepdims=True), sink_col)
    p = [jnp.exp2(t - m).astype(BF16) for t in tiles]
    acc = jnp.dot(jnp.concatenate(p[:n_loc], axis=1), v_loc, preferred_element_type=F32)
    if k_ctx is not None:
        acc = acc + jnp.dot(jnp.concatenate(p[n_loc:], axis=1), v_ctx, preferred_element_type=F32)
    den = acc[:, A_HD:A_HD + 1] + jnp.exp2(sink_col - m)
    return acc[:, :A_HD] / den


def _attn_kernel(*refs, n_tok, latent):
    if latent:
        (sink_ref, q_ref, k_ref, v_ref, qw_ref, kw_ref, cos_ref, sin_ref, ck_ref, cv_ref,
         o_ref, qh_ref, kg_ref, va_ref) = refs
    else:
        (sink_ref, q_ref, k_ref, v_ref, qw_ref, kw_ref,
         o_ref, ko_ref, vo_ref) = refs

    lane = lax.broadcasted_iota(jnp.int32, (1, A_KV_W), 1)
    lane_lo = (lane % 32) < 16
    bd_r = lax.broadcasted_iota(jnp.int32, (A_KV_W, A_KV_W), 0) // A_HD
    bd_c = lax.broadcasted_iota(jnp.int32, (A_KV_W, A_KV_W), 1) // A_HD
    ones_bd = jnp.where(bd_r == bd_c, 1.0, 0.0).astype(BF16)

    k_all = _group_rms(k_ref[0].astype(F32), kw_ref[...], ones_bd)
    v_all = v_ref[0]
    if latent:
        k_all = _rope(k_all, cos_ref[...], sin_ref[...], lane_lo)
    else:
        ko_ref[0] = k_all
        vo_ref[0] = v_all.astype(F32)

    q_blk = WINDOW if latent else n_tok
    m_rows = A_GROUP * q_blk
    head_of_row = lax.broadcasted_iota(jnp.int32, (m_rows, 1), 0) // q_blk
    r_in = lax.broadcasted_iota(jnp.int32, (m_rows, WINDOW), 0) % WINDOW
    c_in = lax.broadcasted_iota(jnp.int32, (m_rows, WINDOW), 1)
    masks = {"prev": c_in >= r_in, "next": c_in <= r_in}
    ones_v = jnp.ones((n_tok, A_HD), BF16)

    def sink_column(g):
        col = jnp.zeros((m_rows, 1), F32)
        for a in range(A_GROUP):
            col = jnp.where(head_of_row == a, sink_ref[A_GROUP * g + a] * LOG2E, col)
        return col

    def unstack(o):
        return jnp.concatenate([o[a * q_blk:(a + 1) * q_blk, :] for a in range(A_GROUP)], axis=1).astype(BF16)

    for g in range(A_KV):
        g_cols = slice(A_KV_W * g, A_KV_W * (g + 1))
        h_cols = slice(A_HD * g, A_HD * (g + 1))
        qg = _group_rms(q_ref[0, :, g_cols].astype(F32), qw_ref[...], ones_bd)
        if latent:
            qg = _rope(qg, cos_ref[...], sin_ref[...], lane_lo)
        qg = qg * (A_HD ** -0.5 * LOG2E)
        kg = k_all[:, h_cols].astype(BF16)
        va = jnp.concatenate([v_all[:, h_cols], ones_v], axis=1)
        sink_col = sink_column(g)
        if not latent:
            q_stack = jnp.concatenate([qg[:, A_HD * a:A_HD * (a + 1)] for a in range(A_GROUP)], axis=0)
            o = _attend_block(q_stack.astype(BF16), sink_col, kg, va, ["cur"] * (n_tok // WINDOW), masks)
            o_ref[0, :, g_cols] = unstack(o)
            continue

        for a in range(A_GROUP):
            qh_ref[a] = qg[:, A_HD * a:A_HD * (a + 1)].astype(BF16)
        kg_ref[...] = kg
        va_ref[...] = va
        n_ctx = ck_ref.shape[1]
        ckg = ck_ref[0][:, h_cols].astype(BF16)
        cva = jnp.concatenate([cv_ref[0][:, h_cols].astype(BF16), jnp.ones((n_ctx, A_HD), BF16)], axis=1)
        nb = n_tok // WINDOW

        def q_block(i, kinds, k_start, g_cols=g_cols, sink_col=sink_col, ckg=ckg, cva=cva, qg=qg, kg=kg, va=va):
            n_keys = len(kinds) * WINDOW
            if isinstance(i, int):
                r0 = i * WINDOW
                q_stack = jnp.concatenate([qg[r0:r0 + WINDOW, A_HD * a:A_HD * (a + 1)] for a in range(A_GROUP)],
                                          axis=0).astype(BF16)
                k_loc = kg[k_start:k_start + n_keys]
                v_loc = va[k_start:k_start + n_keys]
            else:
                r0 = pl.multiple_of(i * WINDOW, WINDOW)
                k_start = pl.multiple_of(k_start, WINDOW)
                q_stack = jnp.concatenate([qh_ref[a, pl.ds(r0, WINDOW), :] for a in range(A_GROUP)], axis=0)
                k_loc = kg_ref[pl.ds(k_start, n_keys), :]
                v_loc = va_ref[pl.ds(k_start, n_keys), :]
            o = _attend_block(q_stack, sink_col, k_loc, v_loc, kinds, masks, ckg, cva)
            o_ref[0, pl.ds(r0, WINDOW), g_cols] = unstack(o)

        def interior(i, carry, q_block=q_block):
            q_block(i, ["prev", "cur", "next"], (i - 1) * WINDOW)
            return carry

        lax.fori_loop(1, nb - 1, interior, 0, unroll=3)
        q_block(0, ["cur", "next"], 0)
        q_block(nb - 1, ["prev", "cur"], (nb - 2) * WINDOW)


def _attention(z3, kv3, sink, q_norm_w, k_norm_w, rope_tabs, cache):
    bsz, n_tok, _ = z3.shape
    latent = cache is not None
    in_specs = [
        pl.BlockSpec(memory_space=pltpu.SMEM),
        pl.BlockSpec((1, n_tok, A_Q_W), lambda b: (b, 0, Z_AQ // A_Q_W)),
        pl.BlockSpec((1, n_tok, A_KV_W), lambda b: (b, 0, 0)),
        pl.BlockSpec((1, n_tok, A_KV_W), lambda b: (b, 0, 1)),
        pl.BlockSpec((1, A_KV_W), lambda b: (0, 0)),
        pl.BlockSpec((1, A_KV_W), lambda b: (0, 0)),
    ]
    args = [sink, z3, kv3, kv3, q_norm_w, k_norm_w]
    out_specs = [pl.BlockSpec((1, n_tok, A_Q_W), lambda b: (b, 0, 0))]
    out_shape = [jax.ShapeDtypeStruct((bsz, n_tok, A_Q_W), BF16)]
    scratch = []
    if latent:
        assert n_tok // WINDOW >= 3
        n_ctx = cache[0].shape[1]
        in_specs += [
            pl.BlockSpec((n_tok, A_KV_W), lambda b: (0, 0)),
            pl.BlockSpec((n_tok, A_KV_W), lambda b: (0, 0)),
            pl.BlockSpec((1, n_ctx, A_KV_W), lambda b: (b, 0, 0)),
            pl.BlockSpec((1, n_ctx, A_KV_W), lambda b: (b, 0, 0)),
        ]
        args += [rope_tabs[0], rope_tabs[1], cache[0], cache[1]]
        scratch = [
            pltpu.VMEM((A_GROUP, n_tok, A_HD), BF16),
            pltpu.VMEM((n_tok, A_HD), BF16),
            pltpu.VMEM((n_tok, 2 * A_HD), BF16),
        ]
    else:
        out_specs += [pl.BlockSpec((1, n_tok, A_KV_W), lambda b: (b, 0, 0))] * 2
        out_shape += [jax.ShapeDtypeStruct((bsz, n_tok, A_KV_W), F32)] * 2
    return pl.pallas_call(
        functools.partial(_attn_kernel, n_tok=n_tok, latent=latent),
        grid=(bsz,),
        in_specs=in_specs,
        out_specs=out_specs,
        out_shape=out_shape,
        scratch_shapes=scratch,
        compiler_params=_cparams("parallel"),
        name="attention",
    )(*args)


def _merge_kernel(x_ref, mod_ref, hm_ref, ha_ref, gm_ref, ga_ref, wpm_ref, wpa_ref, wo_ref, o_ref):
    pm = jnp.dot(hm_ref[...], wpm_ref[...], preferred_element_type=F32)
    pa = jnp.dot(ha_ref[...], wpa_ref[...], preferred_element_type=F32)
    u = _sigmoid(gm_ref[...].astype(F32)) * pm + _sigmoid(ga_ref[...].astype(F32)) * pa
    mix = jnp.dot(u.astype(BF16), wo_ref[...], preferred_element_type=F32)
    o_ref[...] = x_ref[...] + mod_ref[0, 5:6, :] * mix


def _merge(x, mod3, row_of_tile, hm, ha, z, w_proj_m, w_proj_a, w_out, tm):
    n_tok, d = x.shape
    resident = functools.partial(pl.BlockSpec, pipeline_mode=pl.Buffered(1))
    return pl.pallas_call(
        _merge_kernel,
        grid=(n_tok // tm,),
        in_specs=[
            pl.BlockSpec((tm, d), lambda i: (i, 0)),
            pl.BlockSpec((1, N_MOD, d), lambda i: (row_of_tile(i, tm), 0, 0)),
            pl.BlockSpec((tm, M_HEADS * M_V), lambda i: (i, 0)),
            pl.BlockSpec((tm, A_Q_W), lambda i: (i, 0)),
            pl.BlockSpec((tm, d), lambda i: (i, Z_GM // D_MODEL)),
            pl.BlockSpec((tm, d), lambda i: (i, Z_GA // D_MODEL)),
            resident((M_HEADS * M_V, d), lambda i: (0, 0)),
            resident((A_Q_W, d), lambda i: (0, 0)),
            resident((d, d), lambda i: (0, 0)),
        ],
        out_specs=pl.BlockSpec((tm, d), lambda i: (i, 0)),
        out_shape=jax.ShapeDtypeStruct((n_tok, d), F32),
        compiler_params=_cparams("parallel"),
        name="merge",
    )(x, mod3, hm, ha, z, z, w_proj_m, w_proj_a, w_out)


def _rope_tables(n_tok):
    nf = A_HD // 4
    inv = ROPE_THETA ** (-jnp.arange(nf, dtype=F32) / nf)
    tok = jnp.arange(n_tok)
    pos = jnp.stack([tok // GRID_W, tok % GRID_W], axis=1).astype(F32)
    ang = pos[:, :, None] * inv
    cos = jnp.cos(ang)
    sin = jnp.sin(ang)
    cos_h = jnp.concatenate([cos, cos], axis=-1).reshape(n_tok, A_HD)
    sin_h = jnp.concatenate([-sin, sin], axis=-1).reshape(n_tok, A_HD)
    return jnp.tile(cos_h, (1, A_KV)), jnp.tile(sin_h, (1, A_KV))


TM_FFN = 512
TM_PROJ = 1024
TM_MERGE = 256


def _trunk(x, mod3, row_of_tile, wts, states, rope_tabs, cache):
    bsz, n_tok, d = x.shape
    x2 = x.reshape(bsz * n_tok, d)
    z, gates, kv = _mixer_proj(x2, mod3, row_of_tile, wts["norm_mix"], wts["w_main"], wts["w_aux"], TM_PROJ)
    z3 = z.reshape(bsz, n_tok, Z_COLS)
    g = jnp.transpose(gates[:, :M_GATES].reshape(bsz, n_tok, 2, 2, M_HEADS), (0, 2, 3, 4, 1))
    gate_i = g[:, :, 0].reshape(bsz, GATE_ROWS, n_tok)
    gate_f = g[:, :, 1].reshape(bsz, GATE_ROWS, n_tok)
    m_out = _mlstm(z3, gate_i, gate_f, wts["bias_i"], wts["bias_f"], wts["mlstm_norm"], states)
    a_out = _attention(z3, kv.reshape(bsz, n_tok, 2 * A_KV_W), wts["sink"], wts["q_norm"], wts["k_norm"],
                       rope_tabs, cache)
    hm = m_out[0].reshape(bsz * n_tok, M_HEADS * M_V)
    ha = a_out[0].reshape(bsz * n_tok, A_Q_W)
    x2 = _merge(x2, mod3, row_of_tile, hm, ha, z, wts["w_proj_m"], wts["w_proj_a"], wts["w_out"], TM_MERGE)
    x2 = _ffn(x2, mod3, row_of_tile, wts["norm2"], wts["wi2"], wts["wo2"], 6, TM_FFN)
    return x2.reshape(bsz, n_tok, d), m_out[1:], a_out[1:]


def kernel(x_prompt, x_sample, cache_attn_k, cache_attn_v, state_mlstm_C, state_mlstm_n, state_mlstm_m, c, c_ctx, ada_w, ada_b, norm_ffn1_w, ffn1_wi, ffn1_wo, norm_mix_w, w_in, mlstm_gate_b, mlstm_norm_w, attn_q_norm_w, attn_k_norm_w, attn_sink, w_proj_m, w_proj_a, w_out, norm_ffn2_w, ffn2_wi, ffn2_wo):
    bp, tp, d = x_prompt.shape
    bs, ts, _ = x_sample.shape
    n_ctx = cache_attn_k.shape[2]
    l = 0

    n_rows = 16
    cond = jnp.concatenate([c_ctx[None, :], c, jnp.zeros((n_rows - 1 - bs, d), F32)], axis=0)
    mod3 = _modulation(cond, ada_w[l], ada_b[l][None, :]).reshape(n_rows, N_MOD, d)

    w_main, w_aux = _regroup_w_in(w_in.astype(BF16), l)
    gate_b = mlstm_gate_b[l].reshape(2, 2, M_HEADS)
    norm1, wi1, wo1 = norm_ffn1_w[l][None, :], ffn1_wi[l].astype(BF16), ffn1_wo[l].astype(BF16)
    ctx_row = lambda i, tm: 0
    lat_row = lambda i, tm: 1 + (i * tm) // ts

    x1_prompt = _ffn(x_prompt.reshape(bp * tp, d), mod3, ctx_row, norm1, wi1, wo1, 0, TM_FFN).reshape(bp, tp, d)
    x1_sample, wi2, wo2, w_out_b, w_pm_b, w_pa_b = _ffn(
        x_sample.reshape(bs * ts, d), mod3, lat_row, norm1, wi1, wo1, 0, TM_FFN,
        side=(ffn2_wi[l], ffn2_wo[l], w_out[l], w_proj_m[l], w_proj_a[l]))
    x1_sample = x1_sample.reshape(bs, ts, d)
    wts = dict(
        norm_mix=norm_mix_w[l][None, :], w_main=w_main, w_aux=w_aux,
        bias_i=gate_b[:, 0].reshape(GATE_ROWS, 1), bias_f=gate_b[:, 1].reshape(GATE_ROWS, 1),
        mlstm_norm=mlstm_norm_w[l][None, :],
        sink=attn_sink[l], q_norm=jnp.tile(attn_q_norm_w[l], A_KV)[None, :],
        k_norm=jnp.tile(attn_k_norm_w[l], A_KV)[None, :],
        w_proj_m=w_pm_b, w_proj_a=w_pa_b, w_out=w_out_b,
        norm2=norm_ffn2_w[l][None, :], wi2=wi2, wo2=wo2,
    )

    y_prompt, (c_new, n_new, m_new), (k_new, v_new) = _trunk(x1_prompt, mod3, ctx_row, wts, None, None, None)
    new_attn_k = k_new.reshape(bp, 1, tp, A_KV, A_HD)
    new_attn_v = v_new.reshape(bp, 1, tp, A_KV, A_HD)
    new_c = c_new.reshape(bp, 1, 2, M_HEADS, M_V, M_QK)
    new_n = n_new.reshape(bp, 1, 2, M_HEADS, M_QK)
    new_m = m_new.reshape(bp, 1, 2, M_HEADS)

    states = (state_mlstm_C[:, l].reshape(bs, GATE_ROWS, M_V, M_QK),
              state_mlstm_n[:, l].reshape(bs, GATE_ROWS, M_QK),
              state_mlstm_m[:, l].reshape(bs, GATE_ROWS, 1))
    cache = (cache_attn_k[:, l].reshape(bs, n_ctx, A_KV_W), cache_attn_v[:, l].reshape(bs, n_ctx, A_KV_W))
    y_sample, _, _ = _trunk(x1_sample, mod3, lat_row, wts, states, _rope_tables(ts), cache)

    return (y_prompt, y_sample, new_attn_k, new_attn_v, new_c, new_n, new_m)
```

```python
import functools

import jax
import jax.numpy as jnp
from jax import lax
from jax.experimental import pallas as pl
from jax.experimental.pallas import tpu as pltpu

F32 = jnp.float32
BF16 = jnp.bfloat16
LOG2E = 1.4426950408889634

D_MODEL = 2048
D_FF = 5632
N_MOD = 9
NORM_EPS = 1e-6
GRID_W = 64
ROPE_THETA = 10000.0
M_HEADS = 4
M_QK = 128
M_V = 256
M_GATES = 4 * M_HEADS
A_HEADS = 16
A_KV = 4
A_GROUP = A_HEADS // A_KV
A_HD = 64
A_Q_W = A_HEADS * A_HD
A_KV_W = A_KV * A_HD
WINDOW = 128

Z_GM, Z_GA, Z_MV, Z_MO, Z_AQ, Z_MQ, Z_MK = 0, 2048, 4096, 5120, 6144, 7168, 7680
Z_COLS = 8192
GATE_LANES = 128
AUX_COLS = GATE_LANES + 2 * A_KV_W
GATE_ROWS = 2 * M_HEADS

FFN_TF = 512
PROJ_TN = 1024
MLSTM_BLOCK = 256
XPOSE_ROWS = 128

VMEM_LIMIT = 56 * 1024 * 1024
NT_DIMS = (((1,), (1,)), ((), ()))
TN_DIMS = (((0,), (0,)), ((), ()))


def _cparams(*sem):
    return pltpu.CompilerParams(dimension_semantics=sem, vmem_limit_bytes=VMEM_LIMIT)


def _sigmoid(x):
    return 1.0 / (1.0 + jnp.exp(-x))


def _norm_modulate(x, norm_w, shift, scale):
    ms = jnp.mean(x * x, axis=-1, keepdims=True)
    y = x * lax.rsqrt(ms + NORM_EPS) * norm_w
    return y * (1.0 + scale) + shift


def _mod_kernel(c_ref, w_ref, b_ref, o_ref):
    c = c_ref[...]
    s = (c * _sigmoid(c)).astype(BF16)
    o_ref[...] = jnp.dot(s, w_ref[...].astype(BF16), preferred_element_type=F32) + b_ref[...]


def _modulation(cond, ada_w, ada_b, tn=1024):
    rows, d = cond.shape
    n = ada_w.shape[1]
    return pl.pallas_call(
        _mod_kernel,
        grid=(n // tn,),
        in_specs=[
            pl.BlockSpec((rows, d), lambda j: (0, 0)),
            pl.BlockSpec((d, tn), lambda j: (0, j)),
            pl.BlockSpec((1, tn), lambda j: (0, j)),
        ],
        out_specs=pl.BlockSpec((rows, tn), lambda j: (0, j)),
        out_shape=jax.ShapeDtypeStruct((rows, n), F32),
        compiler_params=_cparams("arbitrary"),
        name="modulation",
    )(cond, ada_w, ada_b)


SIDE_ROWS = 16


FFN_SLOTS = 3


def _ffn_kernel(x_ref, mod_ref, nw_ref, wi_hbm, wo_hbm, *rest, k_shift, n_side, nf, tf):
    side_in, o_ref, side_out = rest[:n_side], rest[n_side], rest[n_side + 1:2 * n_side + 1]
    h_ref, wg_buf, wu_buf, wo_buf, sem = rest[2 * n_side + 1:]
    i = pl.program_id(0)
    j = pl.program_id(1)
    n_steps = pl.num_programs(0) * nf
    s = i * nf + j

    def chunk_copies(step, slot):
        c = step % nf
        g_cols = pl.ds(pl.multiple_of(c * tf, tf), tf)
        u_cols = pl.ds(pl.multiple_of((c + nf) * tf, tf), tf)
        return (pltpu.make_async_copy(wi_hbm.at[:, g_cols], wg_buf.at[slot], sem.at[0, slot]),
                pltpu.make_async_copy(wi_hbm.at[:, u_cols], wu_buf.at[slot], sem.at[1, slot]),
                pltpu.make_async_copy(wo_hbm.at[g_cols, :], wo_buf.at[slot], sem.at[2, slot]))

    @pl.when(s == 0)
    def _():
        for first in range(FFN_SLOTS - 1):
            for cp in chunk_copies(first, first):
                cp.start()

    ahead = s + FFN_SLOTS - 1

    @pl.when(ahead < n_steps)
    def _():
        for cp in chunk_copies(ahead, ahead % FFN_SLOTS):
            cp.start()

    for src, dst in zip(side_in, side_out):
        dst[...] = src[...].astype(BF16)

    @pl.when(j == 0)
    def _():
        h = _norm_modulate(x_ref[...], nw_ref[...], mod_ref[0, k_shift:k_shift + 1, :],
                           mod_ref[0, k_shift + 1:k_shift + 2, :])
        h_ref[...] = h.astype(BF16)
        o_ref[...] = x_ref[...]

    slot = s % FFN_SLOTS
    for cp in chunk_copies(s, slot):
        cp.wait()

    h = h_ref[...]
    g = jnp.dot(h, wg_buf[slot], preferred_element_type=F32)
    u = jnp.dot(h, wu_buf[slot], preferred_element_type=F32)
    a = (g * _sigmoid(g) * u).astype(BF16)
    half_gate = 0.5 * mod_ref[0, k_shift + 2:k_shift + 3, :]
    o_ref[...] += half_gate * jnp.dot(a, wo_buf[slot], preferred_element_type=F32)


def _ffn(x, mod3, row_of_tile, norm_w, wi, wo, k_shift, tm, side=()):
    n_tok, d = x.shape
    tf = FFN_TF
    nf = D_FF // tf
    n_steps = (n_tok // tm) * nf
    side_specs = []
    for arr in side:
        r, c = arr.shape
        rows = SIDE_ROWS * pl.cdiv(r, SIDE_ROWS * n_steps)
        assert r % rows == 0
        last = r // rows - 1
        side_specs.append(pl.BlockSpec((rows, c), lambda i, j, last=last: (jnp.minimum(i * nf + j, last), 0)))
    assert n_steps >= FFN_SLOTS - 1
    out = pl.pallas_call(
        functools.partial(_ffn_kernel, k_shift=k_shift, n_side=len(side), nf=nf, tf=tf),
        grid=(n_tok // tm, nf),
        in_specs=[
            pl.BlockSpec((tm, d), lambda i, j: (i, 0)),
            pl.BlockSpec((1, N_MOD, d), lambda i, j: (row_of_tile(i, tm), 0, 0)),
            pl.BlockSpec((1, d), lambda i, j: (0, 0)),
            pl.BlockSpec(memory_space=pl.ANY),
            pl.BlockSpec(memory_space=pl.ANY),
        ] + side_specs,
        out_specs=[pl.BlockSpec((tm, d), lambda i, j: (i, 0))] + side_specs,
        out_shape=[jax.ShapeDtypeStruct((n_tok, d), F32)] + [jax.ShapeDtypeStruct(a.shape, BF16) for a in side],
        scratch_shapes=[
            pltpu.VMEM((tm, d), BF16),
            pltpu.VMEM((FFN_SLOTS, d, tf), BF16),
            pltpu.VMEM((FFN_SLOTS, d, tf), BF16),
            pltpu.VMEM((FFN_SLOTS, tf, d), BF16),
            pltpu.SemaphoreType.DMA((3, FFN_SLOTS)),
        ],
        compiler_params=_cparams("arbitrary", "arbitrary"),
        name="ffn",
    )(x, mod3, norm_w, wi, wo, *side)
    return out if side else out[0]


W_IN_SEGS = (("mq", 512), ("mk", 512), ("mv", 1024), ("mo", 1024), ("mg", M_GATES), ("aq", 1024),
             ("ak", 256), ("av", 256), ("gm", 2048), ("ga", 2048))
W_IN_OFF = {}
for _name, _size in W_IN_SEGS:
    W_IN_OFF[_name] = (sum(s for _, s in W_IN_SEGS[:len(W_IN_OFF)]), _size)
IN_COLS = sum(s for _, s in W_IN_SEGS)
MAIN_ORDER = ("gm", "ga", "mv", "mo", "aq", "mq", "mk")
REGROUP_ROWS = 128


def _regroup_kernel(w_ref, main_ref, aux_ref):
    w = w_ref[0]
    t0 = W_IN_OFF["aq"][0]
    tail = w[:, t0:]

    def seg(name):
        start, size = W_IN_OFF[name]
        return (tail[:, start - t0:start - t0 + size] if start >= t0 else w[:, start:start + size]).astype(BF16)

    col = 0
    for name in MAIN_ORDER:
        size = W_IN_OFF[name][1]
        main_ref[:, col:col + size] = seg(name)
        col += size
    g0 = W_IN_OFF["mg"][0]
    lane = lax.broadcasted_iota(jnp.int32, (1, GATE_LANES), 1)
    aux_ref[:, :GATE_LANES] = jnp.where(lane < M_GATES, w[:, g0:g0 + GATE_LANES], 0.0).astype(BF16)
    aux_ref[:, GATE_LANES:GATE_LANES + A_KV_W] = seg("ak")
    aux_ref[:, GATE_LANES + A_KV_W:] = seg("av")


def _regroup_w_in(w, l):
    _, d, n = w.shape
    assert n == IN_COLS and W_IN_OFF["mg"][0] % GATE_LANES == 0
    return pl.pallas_call(
        _regroup_kernel,
        grid=(d // REGROUP_ROWS,),
        in_specs=[pl.BlockSpec((1, REGROUP_ROWS, n), lambda i: (l, i, 0))],
        out_specs=[
            pl.BlockSpec((REGROUP_ROWS, Z_COLS), lambda i: (i, 0)),
            pl.BlockSpec((REGROUP_ROWS, AUX_COLS), lambda i: (i, 0)),
        ],
        out_shape=[jax.ShapeDtypeStruct((d, Z_COLS), BF16), jax.ShapeDtypeStruct((d, AUX_COLS), BF16)],
        compiler_params=_cparams("parallel"),
        name="regroup_w_in",
    )(w)


def _proj_kernel(x_ref, mod_ref, nw_ref, w_ref, waux_ref, z_ref, g_ref, kv_ref, h_ref):
    j = pl.program_id(1)

    @pl.when(j == 0)
    def _():
        h = _norm_modulate(x_ref[...], nw_ref[...], mod_ref[0, 3:4, :], mod_ref[0, 4:5, :]).astype(BF16)
        h_ref[...] = h
        aux = jnp.dot(h, waux_ref[...], preferred_element_type=F32)
        g_ref[...] = aux[:, :GATE_LANES]
        kv_ref[...] = aux[:, GATE_LANES:].astype(BF16)

    half = z_ref.shape[1] // 2
    h = h_ref[...]
    z_ref[:, :half] = jnp.dot(h, w_ref[:, :half], preferred_element_type=F32).astype(BF16)
    z_ref[:, half:] = jnp.dot(h, w_ref[:, half:], preferred_element_type=F32).astype(BF16)


def _mixer_proj(x, mod3, row_of_tile, norm_w, w_main, w_aux, tm):
    n_tok, d = x.shape
    tn = PROJ_TN
    once = functools.partial(pl.BlockSpec, pipeline_mode=pl.Buffered(1))
    return pl.pallas_call(
        _proj_kernel,
        grid=(n_tok // tm, Z_COLS // tn),
        in_specs=[
            pl.BlockSpec((tm, d), lambda i, j: (i, 0)),
            pl.BlockSpec((1, N_MOD, d), lambda i, j: (row_of_tile(i, tm), 0, 0)),
            pl.BlockSpec((1, d), lambda i, j: (0, 0)),
            pl.BlockSpec((d, tn), lambda i, j: (0, j)),
            once((d, AUX_COLS), lambda i, j: (0, 0)),
        ],
        out_specs=[
            pl.BlockSpec((tm, tn), lambda i, j: (i, j)),
            pl.BlockSpec((tm, GATE_LANES), lambda i, j: (i, 0)),
            pl.BlockSpec((tm, 2 * A_KV_W), lambda i, j: (i, 0)),
        ],
        out_shape=[
            jax.ShapeDtypeStruct((n_tok, Z_COLS), BF16),
            jax.ShapeDtypeStruct((n_tok, GATE_LANES), F32),
            jax.ShapeDtypeStruct((n_tok, 2 * A_KV_W), BF16),
        ],
        scratch_shapes=[pltpu.VMEM((tm, d), BF16)],
        compiler_params=_cparams("parallel", "arbitrary"),
        name="mixer_proj",
    )(x, mod3, norm_w, w_main, w_aux)


def _scan_lanes(x, op, fill, reverse):
    n = x.shape[-1]
    lane = lax.broadcasted_iota(jnp.int32, x.shape, 1)
    s = 1
    while s < n:
        if reverse:
            x = op(x, jnp.where(lane < n - s, pltpu.roll(x, n - s, 1), fill))
        else:
            x = op(x, jnp.where(lane >= s, pltpu.roll(x, s, 1), fill))
        s *= 2
    return x


def _dir_scan(x, op, fill, fwd_rows):
    return jnp.where(fwd_rows, _scan_lanes(x, op, fill, False), _scan_lanes(x, op, fill, True))


def _log_sigmoid(x):
    return jnp.minimum(x, 0.0) - jnp.log1p(jnp.exp(-jnp.abs(x)))


def _mlstm_kernel(*refs, n_tok, has_state):
    if has_state:
        (q_ref, k_ref, v_ref, og_ref, gi_ref, gf_ref, bi_ref, bf_ref, nw_ref, c0_ref, n0_ref, m0_ref,
         o_ref) = refs
    else:
        (q_ref, k_ref, v_ref, og_ref, gi_ref, gf_ref, bi_ref, bf_ref, nw_ref,
         o_ref, c_out, n_out, m_out) = refs
    blk = min(MLSTM_BLOCK, n_tok)
    nq = n_tok // blk

    fwd_rows = lax.broadcasted_iota(jnp.int32, (GATE_ROWS, 1), 0) < M_HEADS
    ig = gi_ref[0] + bi_ref[...]
    cum_f = _dir_scan(_log_sigmoid(gf_ref[0] + bf_ref[...]), jnp.add, 0.0, fwd_rows)
    a = ig - cum_f
    m0 = m0_ref[0] if has_state else jnp.zeros((GATE_ROWS, 1), F32)
    mx = jnp.maximum(_dir_scan(a, jnp.maximum, -jnp.inf, fwd_rows), m0)
    stats = [-mx * LOG2E, -(cum_f + mx) * LOG2E]
    a2 = a * LOG2E
    m0_2 = m0 * LOG2E
    if not has_state:
        mx_end = jnp.where(fwd_rows, mx[:, n_tok - 1:n_tok], mx[:, 0:1])
        cum_end = jnp.where(fwd_rows, cum_f[:, n_tok - 1:n_tok], cum_f[:, 0:1])
        stats.append(jnp.exp(a - mx_end))
        m_out[0] = cum_end + mx_end
    pad = jnp.zeros((XPOSE_ROWS - GATE_ROWS * len(stats), n_tok), F32)
    cols = jnp.concatenate(stats + [pad], axis=0).T

    row_i = lax.broadcasted_iota(jnp.int32, (blk, blk), 0)
    col_i = lax.broadcasted_iota(jnp.int32, (blk, blk), 1)
    visible = (col_i <= row_i, col_i >= row_i)

    for h in range(M_HEADS):
        qk_cols = slice(h * M_QK, (h + 1) * M_QK)
        v_cols = slice(h * M_V, (h + 1) * M_V)
        qf = q_ref[0, :, qk_cols].astype(F32) * (M_QK ** -0.5)
        q = qf.astype(BF16)
        k = k_ref[0, :, qk_cols]
        v = v_ref[0, :, v_cols]
        for qi in range(nq):
            rows = slice(qi * blk, (qi + 1) * blk)
            h_sum = None
            for d in range(2):
                r = M_HEADS * d + h
                lo, hi = (0, (qi + 1) * blk) if d == 0 else (qi * blk, n_tok)
                dg = hi - lo - blk if d == 0 else 0
                u_col = cols[rows, r:r + 1]
                neg_mt_col = cols[rows, GATE_ROWS + r:GATE_ROWS + r + 1]
                dm = u_col + a2[r:r + 1, lo:hi]
                parts = [dm[:, :dg]] if dg > 0 else []
                parts.append(jnp.where(visible[d], dm[:, dg:dg + blk], -jnp.inf))
                if dg + blk < hi - lo:
                    parts.append(dm[:, dg + blk:])
                dm = jnp.concatenate(parts, axis=1) if len(parts) > 1 else parts[0]
                s = lax.dot_general(q[rows], k[lo:hi], NT_DIMS, preferred_element_type=F32) * jnp.exp2(dm)
                num = jnp.dot(s.astype(BF16), v[lo:hi], preferred_element_type=F32)
                den = jnp.sum(s, axis=1, keepdims=True)
                if has_state:
                    decay = jnp.exp2(u_col + m0_2[r:r + 1, :])
                    num = num + decay * lax.dot_general(q[rows], c0_ref[0, r].astype(BF16), NT_DIMS,
                                                        preferred_element_type=F32)
                    den = den + decay * jnp.sum(qf[rows] * n0_ref[0, r:r + 1, :], axis=1, keepdims=True)
                h_dir = num / jnp.maximum(jnp.abs(den), jnp.exp2(neg_mt_col))
                h_sum = h_dir if h_sum is None else h_sum + h_dir
            hn = h_sum * lax.rsqrt(jnp.mean(h_sum * h_sum, axis=-1, keepdims=True) + NORM_EPS) * nw_ref[:, v_cols]
            o_ref[0, rows, v_cols] = (hn * _sigmoid(og_ref[0, rows, v_cols].astype(F32))).astype(BF16)
        if not has_state:
            kf = k.astype(F32)
            vf = v.astype(F32)
            for d in range(2):
                r = M_HEADS * d + h
                wk_col = cols[:, 2 * GATE_ROWS + r:2 * GATE_ROWS + r + 1]
                c_out[0, r] = lax.dot_general((vf * wk_col).astype(BF16), k, TN_DIMS, preferred_element_type=F32)
                n_out[0, r:r + 1, :] = jnp.sum(kf * wk_col, axis=0, keepdims=True)


def _mlstm(z3, gate_i, gate_f, bias_i, bias_f, norm_w, states):
    bsz, n_tok, _ = z3.shape
    has_state = states is not None
    qk_w, v_w = M_HEADS * M_QK, M_HEADS * M_V
    in_specs = [
        pl.BlockSpec((1, n_tok, qk_w), lambda b: (b, 0, Z_MQ // qk_w)),
        pl.BlockSpec((1, n_tok, qk_w), lambda b: (b, 0, Z_MK // qk_w)),
        pl.BlockSpec((1, n_tok, v_w), lambda b: (b, 0, Z_MV // v_w)),
        pl.BlockSpec((1, n_tok, v_w), lambda b: (b, 0, Z_MO // v_w)),
        pl.BlockSpec((1, GATE_ROWS, n_tok), lambda b: (b, 0, 0)),
        pl.BlockSpec((1, GATE_ROWS, n_tok), lambda b: (b, 0, 0)),
        pl.BlockSpec((GATE_ROWS, 1), lambda b: (0, 0)),
        pl.BlockSpec((GATE_ROWS, 1), lambda b: (0, 0)),
        pl.BlockSpec((1, v_w), lambda b: (0, 0)),
    ]
    state_specs = [
        pl.BlockSpec((1, GATE_ROWS, M_V, M_QK), lambda b: (b, 0, 0, 0)),
        pl.BlockSpec((1, GATE_ROWS, M_QK), lambda b: (b, 0, 0)),
        pl.BlockSpec((1, GATE_ROWS, 1), lambda b: (b, 0, 0)),
    ]
    out_specs = [pl.BlockSpec((1, n_tok, v_w), lambda b: (b, 0, 0))]
    out_shape = [jax.ShapeDtypeStruct((bsz, n_tok, v_w), BF16)]
    args = [z3, z3, z3, z3, gate_i, gate_f, bias_i, bias_f, norm_w]
    if has_state:
        in_specs += state_specs
        args += list(states)
    else:
        out_specs += state_specs
        out_shape += [
            jax.ShapeDtypeStruct((bsz, GATE_ROWS, M_V, M_QK), F32),
            jax.ShapeDtypeStruct((bsz, GATE_ROWS, M_QK), F32),
            jax.ShapeDtypeStruct((bsz, GATE_ROWS, 1), F32),
        ]
    return pl.pallas_call(
        functools.partial(_mlstm_kernel, n_tok=n_tok, has_state=has_state),
        grid=(bsz,),
        in_specs=in_specs,
        out_specs=out_specs,
        out_shape=out_shape,
        compiler_params=_cparams("parallel"),
        name="mlstm",
    )(*args)


def _group_rms(x, w_row, ones_bd):
    x2 = x * x
    hi = x2.astype(BF16)
    lo = (x2 - hi.astype(F32)).astype(BF16)
    ss = jnp.dot(hi, ones_bd, preferred_element_type=F32) + jnp.dot(lo, ones_bd, preferred_element_type=F32)
    return x * lax.rsqrt(ss * (1.0 / A_HD) + NORM_EPS) * w_row


def _rope(x, cos, sin_signed, swap):
    hi = x.astype(BF16)
    lo = (x - hi.astype(F32)).astype(BF16)
    partner = jnp.dot(hi, swap, preferred_element_type=F32) + jnp.dot(lo, swap, preferred_element_type=F32)
    return x * cos + partner * sin_signed


def _attend_block(q, sink_col, k_loc, v_loc, kinds, masks, k_ctx=None, v_ctx=None):
    s_loc = lax.dot_general(q, k_loc, NT_DIMS, preferred_element_type=F32)
    tiles = []
    for j, kind in enumerate(kinds):
        t = s_loc[:, j * WINDOW:(j + 1) * WINDOW]
        tiles.append(t if kind == "cur" else jnp.where(masks[kind], t, -jnp.inf))
    n_loc = len(tiles)
    if k_ctx is not None:
        s_ctx = lax.dot_general(q, k_ctx, NT_DIMS, preferred_element_type=F32)
        tiles += [s_ctx[:, j * WINDOW:(j + 1) * WINDOW] for j in range(k_ctx.shape[0] // WINDOW)]
    tile_max = tiles[0]
    for t in tiles[1:]:
        tile_max = jnp.maximum(tile_max, t)
    m = jnp.maximum(jnp.max(tile_max, axis=1, keepdims=True), sink_col)
    p = [jnp.exp2(t - m).astype(BF16) for t in tiles]
    acc = jnp.dot(jnp.concatenate(p[:n_loc], axis=1), v_loc, preferred_element_type=F32)
    if k_ctx is not None:
        acc = acc + jnp.dot(jnp.concatenate(p[n_loc:], axis=1), v_ctx, preferred_element_type=F32)
    den = acc[:, A_HD:A_HD + 1] + jnp.exp2(sink_col - m)
    return acc[:, :A_HD] / den


def _attn_kernel(*refs, n_tok, latent):
    if latent:
        (sink_ref, q_ref, k_ref, v_ref, qw_ref, kw_ref, cos_ref, sin_ref, ck_ref, cv_ref,
         o_ref, qh_ref, kg_ref, va_ref) = refs
    else:
        (sink_ref, q_ref, k_ref, v_ref, qw_ref, kw_ref,
         o_ref, ko_ref, vo_ref) = refs

    sq_r = lax.broadcasted_iota(jnp.int32, (A_KV_W, A_KV_W), 0)
    sq_c = lax.broadcasted_iota(jnp.int32, (A_KV_W, A_KV_W), 1)
    ones_bd = jnp.where(sq_r // A_HD == sq_c // A_HD, 1.0, 0.0).astype(BF16)
    partner_of = jnp.where((sq_c % 32) < 16, sq_c + 16, sq_c - 16)
    swap = jnp.where(sq_r == partner_of, 1.0, 0.0).astype(BF16)

    k_all = _group_rms(k_ref[0].astype(F32), kw_ref[...], ones_bd)
    v_all = v_ref[0]
    if latent:
        k_all = _rope(k_all, cos_ref[...], sin_ref[...], swap)
    else:
        ko_ref[0] = k_all
        vo_ref[0] = v_all.astype(F32)

    q_blk = WINDOW if latent else n_tok
    m_rows = A_GROUP * q_blk
    head_of_row = lax.broadcasted_iota(jnp.int32, (m_rows, 1), 0) // q_blk
    r_in = lax.broadcasted_iota(jnp.int32, (m_rows, WINDOW), 0) % WINDOW
    c_in = lax.broadcasted_iota(jnp.int32, (m_rows, WINDOW), 1)
    masks = {"prev": c_in >= r_in, "next": c_in <= r_in}
    ones_v = jnp.ones((n_tok, A_HD), BF16)

    def sink_column(g):
        col = jnp.zeros((m_rows, 1), F32)
        for a in range(A_GROUP):
            col = jnp.where(head_of_row == a, sink_ref[A_GROUP * g + a] * LOG2E, col)
        return col

    def unstack(o):
        return jnp.concatenate([o[a * q_blk:(a + 1) * q_blk, :] for a in range(A_GROUP)], axis=1).astype(BF16)

    for g in range(A_KV):
        g_cols = slice(A_KV_W * g, A_KV_W * (g + 1))
        h_cols = slice(A_HD * g, A_HD * (g + 1))
        qg = _group_rms(q_ref[0, :, g_cols].astype(F32), qw_ref[...], ones_bd)
        if latent:
            qg = _rope(qg, cos_ref[...], sin_ref[...], swap)
        qg = qg * (A_HD ** -0.5 * LOG2E)
        kg = k_all[:, h_cols].astype(BF16)
        va = jnp.concatenate([v_all[:, h_cols], ones_v], axis=1)
        sink_col = sink_column(g)
        if not latent:
            q_stack = jnp.concatenate([qg[:, A_HD * a:A_HD * (a + 1)] for a in range(A_GROUP)], axis=0)
            o = _attend_block(q_stack.astype(BF16), sink_col, kg, va, ["cur"] * (n_tok // WINDOW), masks)
            o_ref[0, :, g_cols] = unstack(o)
            continue

        for a in range(A_GROUP):
            qh_ref[a] = qg[:, A_HD * a:A_HD * (a + 1)].astype(BF16)
        kg_ref[...] = kg
        va_ref[...] = va
        n_ctx = ck_ref.shape[1]
        ckg = ck_ref[0][:, h_cols].astype(BF16)
        cva = jnp.concatenate([cv_ref[0][:, h_cols].astype(BF16), jnp.ones((n_ctx, A_HD), BF16)], axis=1)
        nb = n_tok // WINDOW

        def q_block(i, kinds, k_start, g_cols=g_cols, sink_col=sink_col, ckg=ckg, cva=cva, qg=qg, kg=kg, va=va):
            n_keys = len(kinds) * WINDOW
            if isinstance(i, int):
                r0 = i * WINDOW
                q_stack = jnp.concatenate([qg[r0:r0 + WINDOW, A_HD * a:A_HD * (a + 1)] for a in range(A_GROUP)],
                                          axis=0).astype(BF16)
                k_loc = kg[k_start:k_start + n_keys]
                v_loc = va[k_start:k_start + n_keys]
            else:
                r0 = pl.multiple_of(i * WINDOW, WINDOW)
                k_start = pl.multiple_of(k_start, WINDOW)
                q_stack = jnp.concatenate([qh_ref[a, pl.ds(r0, WINDOW), :] for a in range(A_GROUP)], axis=0)
                k_loc = kg_ref[pl.ds(k_start, n_keys), :]
                v_loc = va_ref[pl.ds(k_start, n_keys), :]
            o = _attend_block(q_stack, sink_col, k_loc, v_loc, kinds, masks, ckg, cva)
            o_ref[0, pl.ds(r0, WINDOW), g_cols] = unstack(o)

        def interior(i, carry, q_block=q_block):
            q_block(i, ["prev", "cur", "next"], (i - 1) * WINDOW)
            return carry

        lax.fori_loop(1, nb - 1, interior, 0, unroll=nb - 2)
        q_block(0, ["cur", "next"], 0)
        q_block(nb - 1, ["prev", "cur"], (nb - 2) * WINDOW)


def _attention(z3, kv3, sink, q_norm_w, k_norm_w, rope_tabs, cache):
    bsz, n_tok, _ = z3.shape
    latent = cache is not None
    in_specs = [
        pl.BlockSpec(memory_space=pltpu.SMEM),
        pl.BlockSpec((1, n_tok, A_Q_W), lambda b: (b, 0, Z_AQ // A_Q_W)),
        pl.BlockSpec((1, n_tok, A_KV_W), lambda b: (b, 0, 0)),
        pl.BlockSpec((1, n_tok, A_KV_W), lambda b: (b, 0, 1)),
        pl.BlockSpec((1, A_KV_W), lambda b: (0, 0)),
        pl.BlockSpec((1, A_KV_W), lambda b: (0, 0)),
    ]
    args = [sink, z3, kv3, kv3, q_norm_w, k_norm_w]
    out_specs = [pl.BlockSpec((1, n_tok, A_Q_W), lambda b: (b, 0, 0))]
    out_shape = [jax.ShapeDtypeStruct((bsz, n_tok, A_Q_W), BF16)]
    scratch = []
    if latent:
        assert n_tok // WINDOW >= 3
        n_ctx = cache[0].shape[1]
        in_specs += [
            pl.BlockSpec((n_tok, A_KV_W), lambda b: (0, 0)),
            pl.BlockSpec((n_tok, A_KV_W), lambda b: (0, 0)),
            pl.BlockSpec((1, n_ctx, A_KV_W), lambda b: (b, 0, 0)),
            pl.BlockSpec((1, n_ctx, A_KV_W), lambda b: (b, 0, 0)),
        ]
        args += [rope_tabs[0], rope_tabs[1], cache[0], cache[1]]
        scratch = [
            pltpu.VMEM((A_GROUP, n_tok, A_HD), BF16),
            pltpu.VMEM((n_tok, A_HD), BF16),
            pltpu.VMEM((n_tok, 2 * A_HD), BF16),
        ]
    else:
        out_specs += [pl.BlockSpec((1, n_tok, A_KV_W), lambda b: (b, 0, 0))] * 2
        out_shape += [jax.ShapeDtypeStruct((bsz, n_tok, A_KV_W), F32)] * 2
    return pl.pallas_call(
        functools.partial(_attn_kernel, n_tok=n_tok, latent=latent),
        grid=(bsz,),
        in_specs=in_specs,
        out_specs=out_specs,
        out_shape=out_shape,
        scratch_shapes=scratch,
        compiler_params=_cparams("parallel"),
        name="attention",
    )(*args)


def _merge_kernel(x_ref, mod_ref, hm_ref, ha_ref, gm_ref, ga_ref, wpm_ref, wpa_ref, wo_ref, o_ref):
    pm = jnp.dot(hm_ref[...], wpm_ref[...], preferred_element_type=F32)
    pa = jnp.dot(ha_ref[...], wpa_ref[...], preferred_element_type=F32)
    u = _sigmoid(gm_ref[...].astype(F32)) * pm + _sigmoid(ga_ref[...].astype(F32)) * pa
    mix = jnp.dot(u.astype(BF16), wo_ref[...], preferred_element_type=F32)
    o_ref[...] = x_ref[...] + mod_ref[0, 5:6, :] * mix


def _merge(x, mod3, row_of_tile, hm, ha, z, w_proj_m, w_proj_a, w_out, tm):
    n_tok, d = x.shape
    resident = functools.partial(pl.BlockSpec, pipeline_mode=pl.Buffered(1))
    return pl.pallas_call(
        _merge_kernel,
        grid=(n_tok // tm,),
        in_specs=[
            pl.BlockSpec((tm, d), lambda i: (i, 0)),
            pl.BlockSpec((1, N_MOD, d), lambda i: (row_of_tile(i, tm), 0, 0)),
            pl.BlockSpec((tm, M_HEADS * M_V), lambda i: (i, 0)),
            pl.BlockSpec((tm, A_Q_W), lambda i: (i, 0)),
            pl.BlockSpec((tm, d), lambda i: (i, Z_GM // D_MODEL)),
            pl.BlockSpec((tm, d), lambda i: (i, Z_GA // D_MODEL)),
            resident((M_HEADS * M_V, d), lambda i: (0, 0)),
            resident((A_Q_W, d), lambda i: (0, 0)),
            resident((d, d), lambda i: (0, 0)),
        ],
        out_specs=pl.BlockSpec((tm, d), lambda i: (i, 0)),
        out_shape=jax.ShapeDtypeStruct((n_tok, d), F32),
        compiler_params=_cparams("parallel"),
        name="merge",
    )(x, mod3, hm, ha, z, z, w_proj_m, w_proj_a, w_out)


def _rope_tables(n_tok):
    nf = A_HD // 4
    inv = ROPE_THETA ** (-jnp.arange(nf, dtype=F32) / nf)
    tok = jnp.arange(n_tok)
    pos = jnp.stack([tok // GRID_W, tok % GRID_W], axis=1).astype(F32)
    ang = pos[:, :, None] * inv
    cos = jnp.cos(ang)
    sin = jnp.sin(ang)
    cos_h = jnp.concatenate([cos, cos], axis=-1).reshape(n_tok, A_HD)
    sin_h = jnp.concatenate([-sin, sin], axis=-1).reshape(n_tok, A_HD)
    return jnp.tile(cos_h, (1, A_KV)), jnp.tile(sin_h, (1, A_KV))


TM_FFN = 512
TM_PROJ = 1024
TM_MERGE = 256


def _trunk(x, mod3, row_of_tile, wts, states, rope_tabs, cache):
    bsz, n_tok, d = x.shape
    x2 = x.reshape(bsz * n_tok, d)
    z, gates, kv = _mixer_proj(x2, mod3, row_of_tile, wts["norm_mix"], wts["w_main"], wts["w_aux"], TM_PROJ)
    z3 = z.reshape(bsz, n_tok, Z_COLS)
    g = jnp.transpose(gates[:, :M_GATES].reshape(bsz, n_tok, 2, 2, M_HEADS), (0, 2, 3, 4, 1))
    gate_i = g[:, :, 0].reshape(bsz, GATE_ROWS, n_tok)
    gate_f = g[:, :, 1].reshape(bsz, GATE_ROWS, n_tok)
    m_out = _mlstm(z3, gate_i, gate_f, wts["bias_i"], wts["bias_f"], wts["mlstm_norm"], states)
    a_out = _attention(z3, kv.reshape(bsz, n_tok, 2 * A_KV_W), wts["sink"], wts["q_norm"], wts["k_norm"],
                       rope_tabs, cache)
    hm = m_out[0].reshape(bsz * n_tok, M_HEADS * M_V)
    ha = a_out[0].reshape(bsz * n_tok, A_Q_W)
    x2 = _merge(x2, mod3, row_of_tile, hm, ha, z, wts["w_proj_m"], wts["w_proj_a"], wts["w_out"], TM_MERGE)
    x2 = _ffn(x2, mod3, row_of_tile, wts["norm2"], wts["wi2"], wts["wo2"], 6, TM_FFN)
    return x2.reshape(bsz, n_tok, d), m_out[1:], a_out[1:]


def kernel(x_prompt, x_sample, cache_attn_k, cache_attn_v, state_mlstm_C, state_mlstm_n, state_mlstm_m, c, c_ctx, ada_w, ada_b, norm_ffn1_w, ffn1_wi, ffn1_wo, norm_mix_w, w_in, mlstm_gate_b, mlstm_norm_w, attn_q_norm_w, attn_k_norm_w, attn_sink, w_proj_m, w_proj_a, w_out, norm_ffn2_w, ffn2_wi, ffn2_wo):
    bp, tp, d = x_prompt.shape
    bs, ts, _ = x_sample.shape
    n_ctx = cache_attn_k.shape[2]
    l = 0

    n_rows = 16
    cond = jnp.concatenate([c_ctx[None, :], c, jnp.zeros((n_rows - 1 - bs, d), F32)], axis=0)
    mod3 = _modulation(cond, ada_w[l], ada_b[l][None, :]).reshape(n_rows, N_MOD, d)

    w_main, w_aux = _regroup_w_in(w_in, l)
    gate_b = mlstm_gate_b[l].reshape(2, 2, M_HEADS)
    norm1, wi1, wo1 = norm_ffn1_w[l][None, :], ffn1_wi[l].astype(BF16), ffn1_wo[l].astype(BF16)
    ctx_row = lambda i, tm: 0
    lat_row = lambda i, tm: 1 + (i * tm) // ts

    x1_prompt = _ffn(x_prompt.reshape(bp * tp, d), mod3, ctx_row, norm1, wi1, wo1, 0, TM_FFN).reshape(bp, tp, d)
    x1_sample, wi2, wo2, w_out_b, w_pm_b, w_pa_b = _ffn(
        x_sample.reshape(bs * ts, d), mod3, lat_row, norm1, wi1, wo1, 0, TM_FFN,
        side=(ffn2_wi[l], ffn2_wo[l], w_out[l], w_proj_m[l], w_proj_a[l]))
    x1_sample = x1_sample.reshape(bs, ts, d)
    wts = dict(
        norm_mix=norm_mix_w[l][None, :], w_main=w_main, w_aux=w_aux,
        bias_i=gate_b[:, 0].reshape(GATE_ROWS, 1), bias_f=gate_b[:, 1].reshape(GATE_ROWS, 1),
        mlstm_norm=mlstm_norm_w[l][None, :],
        sink=attn_sink[l], q_norm=jnp.tile(attn_q_norm_w[l], A_KV)[None, :],
        k_norm=jnp.tile(attn_k_norm_w[l], A_KV)[None, :],
        w_proj_m=w_pm_b, w_proj_a=w_pa_b, w_out=w_out_b,
        norm2=norm_ffn2_w[l][None, :], wi2=wi2, wo2=wo2,
    )

    y_prompt, (c_new, n_new, m_new), (k_new, v_new) = _trunk(x1_prompt, mod3, ctx_row, wts, None, None, None)
    new_attn_k = k_new.reshape(bp, 1, tp, A_KV, A_HD)
    new_attn_v = v_new.reshape(bp, 1, tp, A_KV, A_HD)
    new_c = c_new.reshape(bp, 1, 2, M_HEADS, M_V, M_QK)
    new_n = n_new.reshape(bp, 1, 2, M_HEADS, M_QK)
    new_m = m_new.reshape(bp, 1, 2, M_HEADS)

    states = (state_mlstm_C[:, l].reshape(bs, GATE_ROWS, M_V, M_QK),
              state_mlstm_n[:, l].reshape(bs, GATE_ROWS, M_QK),
              state_mlstm_m[:, l].reshape(bs, GATE_ROWS, 1))
    cache = (cache_attn_k[:, l].reshape(bs, n_ctx, A_KV_W), cache_attn_v[:, l].reshape(bs, n_ctx, A_KV_W))
    y_sample, _, _ = _trunk(x1_sample, mod3, lat_row, wts, states, _rope_tables(ts), cache)

    return (y_prompt, y_sample, new_attn_k, new_attn_v, new_c, new_n, new_m)
```

```python
import functools

import jax
import jax.numpy as jnp
from jax import lax
from jax.experimental import pallas as pl
from jax.experimental.pallas import tpu as pltpu

F32 = jnp.float32
BF16 = jnp.bfloat16
LOG2E = 1.4426950408889634

D_MODEL = 2048
D_FF = 5632
N_MOD = 9
NORM_EPS = 1e-6
GRID_W = 64
ROPE_THETA = 10000.0
M_HEADS = 4
M_QK = 128
M_V = 256
M_GATES = 4 * M_HEADS
A_HEADS = 16
A_KV = 4
A_GROUP = A_HEADS // A_KV
A_HD = 64
A_Q_W = A_HEADS * A_HD
A_KV_W = A_KV * A_HD
WINDOW = 128

Z_GM, Z_GA, Z_MV, Z_MO, Z_AQ, Z_MQ, Z_MK = 0, 2048, 4096, 5120, 6144, 7168, 7680
Z_COLS = 8192
GATE_LANES = 128
AUX_COLS = GATE_LANES + 2 * A_KV_W
GATE_ROWS = 2 * M_HEADS

FFN_TF = 512
PROJ_TN = 1024
MLSTM_BLOCK = 256
XPOSE_ROWS = 128

VMEM_LIMIT = 56 * 1024 * 1024
NT_DIMS = (((1,), (1,)), ((), ()))
TN_DIMS = (((0,), (0,)), ((), ()))


def _cparams(*sem):
    return pltpu.CompilerParams(dimension_semantics=sem, vmem_limit_bytes=VMEM_LIMIT)


def _sigmoid(x):
    return 1.0 / (1.0 + jnp.exp(-x))


def _norm_modulate(x, norm_w, shift, scale):
    ms = jnp.mean(x * x, axis=-1, keepdims=True)
    y = x * lax.rsqrt(ms + NORM_EPS) * norm_w
    return y * (1.0 + scale) + shift


def _mod_kernel(c_ref, w_ref, b_ref, o_ref):
    c = c_ref[...]
    s = (c * _sigmoid(c)).astype(BF16)
    o_ref[...] = jnp.dot(s, w_ref[...].astype(BF16), preferred_element_type=F32) + b_ref[...]


def _modulation(cond, ada_w, ada_b, tn=1024):
    rows, d = cond.shape
    n = ada_w.shape[1]
    return pl.pallas_call(
        _mod_kernel,
        grid=(n // tn,),
        in_specs=[
            pl.BlockSpec((rows, d), lambda j: (0, 0)),
            pl.BlockSpec((d, tn), lambda j: (0, j)),
            pl.BlockSpec((1, tn), lambda j: (0, j)),
        ],
        out_specs=pl.BlockSpec((rows, tn), lambda j: (0, j)),
        out_shape=jax.ShapeDtypeStruct((rows, n), F32),
        compiler_params=_cparams("arbitrary"),
        name="modulation",
    )(cond, ada_w, ada_b)


SIDE_ROWS = 16


FFN_SLOTS = 3


def _ffn_kernel(x_ref, mod_ref, nw_ref, wi_hbm, wo_hbm, *rest, k_shift, n_side, nf, tf):
    side_in, o_ref, side_out = rest[:n_side], rest[n_side], rest[n_side + 1:2 * n_side + 1]
    h_ref, wg_buf, wu_buf, wo_buf, sem = rest[2 * n_side + 1:]
    i = pl.program_id(0)
    j = pl.program_id(1)
    n_steps = pl.num_programs(0) * nf
    s = i * nf + j

    def chunk_copies(step, slot):
        c = step % nf
        g_cols = pl.ds(pl.multiple_of(c * tf, tf), tf)
        u_cols = pl.ds(pl.multiple_of((c + nf) * tf, tf), tf)
        return (pltpu.make_async_copy(wi_hbm.at[:, g_cols], wg_buf.at[slot], sem.at[0, slot]),
                pltpu.make_async_copy(wi_hbm.at[:, u_cols], wu_buf.at[slot], sem.at[1, slot]),
                pltpu.make_async_copy(wo_hbm.at[g_cols, :], wo_buf.at[slot], sem.at[2, slot]))

    @pl.when(s == 0)
    def _():
        for first in range(FFN_SLOTS - 1):
            for cp in chunk_copies(first, first):
                cp.start()

    ahead = s + FFN_SLOTS - 1

    @pl.when(ahead < n_steps)
    def _():
        for cp in chunk_copies(ahead, ahead % FFN_SLOTS):
            cp.start()

    for src, dst in zip(side_in, side_out):
        dst[...] = src[...].astype(BF16)

    @pl.when(j == 0)
    def _():
        h = _norm_modulate(x_ref[...], nw_ref[...], mod_ref[0, k_shift:k_shift + 1, :],
                           mod_ref[0, k_shift + 1:k_shift + 2, :])
        h_ref[...] = h.astype(BF16)
        o_ref[...] = x_ref[...]

    slot = s % FFN_SLOTS
    for cp in chunk_copies(s, slot):
        cp.wait()

    h = h_ref[...]
    g = jnp.dot(h, wg_buf[slot], preferred_element_type=F32)
    u = jnp.dot(h, wu_buf[slot], preferred_element_type=F32)
    a = (g * _sigmoid(g) * u).astype(BF16)
    half_gate = 0.5 * mod_ref[0, k_shift + 2:k_shift + 3, :]
    o_ref[...] += half_gate * jnp.dot(a, wo_buf[slot], preferred_element_type=F32)


def _ffn(x, mod3, row_of_tile, norm_w, wi, wo, k_shift, tm, side=()):
    n_tok, d = x.shape
    tf = FFN_TF
    nf = D_FF // tf
    n_steps = (n_tok // tm) * nf
    side_specs = []
    for arr in side:
        r, c = arr.shape
        rows = SIDE_ROWS * pl.cdiv(r, SIDE_ROWS * n_steps)
        assert r % rows == 0
        last = r // rows - 1
        side_specs.append(pl.BlockSpec((rows, c), lambda i, j, last=last: (jnp.minimum(i * nf + j, last), 0)))
    assert n_steps >= FFN_SLOTS - 1
    out = pl.pallas_call(
        functools.partial(_ffn_kernel, k_shift=k_shift, n_side=len(side), nf=nf, tf=tf),
        grid=(n_tok // tm, nf),
        in_specs=[
            pl.BlockSpec((tm, d), lambda i, j: (i, 0)),
            pl.BlockSpec((1, N_MOD, d), lambda i, j: (row_of_tile(i, tm), 0, 0)),
            pl.BlockSpec((1, d), lambda i, j: (0, 0)),
            pl.BlockSpec(memory_space=pl.ANY),
            pl.BlockSpec(memory_space=pl.ANY),
        ] + side_specs,
        out_specs=[pl.BlockSpec((tm, d), lambda i, j: (i, 0))] + side_specs,
        out_shape=[jax.ShapeDtypeStruct((n_tok, d), F32)] + [jax.ShapeDtypeStruct(a.shape, BF16) for a in side],
        scratch_shapes=[
            pltpu.VMEM((tm, d), BF16),
            pltpu.VMEM((FFN_SLOTS, d, tf), BF16),
            pltpu.VMEM((FFN_SLOTS, d, tf), BF16),
            pltpu.VMEM((FFN_SLOTS, tf, d), BF16),
            pltpu.SemaphoreType.DMA((3, FFN_SLOTS)),
        ],
        compiler_params=_cparams("arbitrary", "arbitrary"),
        name="ffn",
    )(x, mod3, norm_w, wi, wo, *side)
    return out if side else out[0]


W_IN_SEGS = (("mq", 512), ("mk", 512), ("mv", 1024), ("mo", 1024), ("mg", M_GATES), ("aq", 1024),
             ("ak", 256), ("av", 256), ("gm", 2048), ("ga", 2048))
W_IN_OFF = {}
for _name, _size in W_IN_SEGS:
    W_IN_OFF[_name] = (sum(s for _, s in W_IN_SEGS[:len(W_IN_OFF)]), _size)
IN_COLS = sum(s for _, s in W_IN_SEGS)
MAIN_ORDER = ("gm", "ga", "mv", "mo", "aq", "mq", "mk")
REGROUP_ROWS = 128


def _regroup_kernel(w_ref, main_ref, aux_ref):
    w = w_ref[0]
    t0 = W_IN_OFF["aq"][0]
    tail = w[:, t0:]

    def seg(name):
        start, size = W_IN_OFF[name]
        return (tail[:, start - t0:start - t0 + size] if start >= t0 else w[:, start:start + size]).astype(BF16)

    col = 0
    for name in MAIN_ORDER:
        size = W_IN_OFF[name][1]
        main_ref[:, col:col + size] = seg(name)
        col += size
    g0 = W_IN_OFF["mg"][0]
    lane = lax.broadcasted_iota(jnp.int32, (1, GATE_LANES), 1)
    aux_ref[:, :GATE_LANES] = jnp.where(lane < M_GATES, w[:, g0:g0 + GATE_LANES], 0.0).astype(BF16)
    aux_ref[:, GATE_LANES:GATE_LANES + A_KV_W] = seg("ak")
    aux_ref[:, GATE_LANES + A_KV_W:] = seg("av")


def _regroup_w_in(w, l):
    _, d, n = w.shape
    assert n == IN_COLS and W_IN_OFF["mg"][0] % GATE_LANES == 0
    return pl.pallas_call(
        _regroup_kernel,
        grid=(d // REGROUP_ROWS,),
        in_specs=[pl.BlockSpec((1, REGROUP_ROWS, n), lambda i: (l, i, 0))],
        out_specs=[
            pl.BlockSpec((REGROUP_ROWS, Z_COLS), lambda i: (i, 0)),
            pl.BlockSpec((REGROUP_ROWS, AUX_COLS), lambda i: (i, 0)),
        ],
        out_shape=[jax.ShapeDtypeStruct((d, Z_COLS), BF16), jax.ShapeDtypeStruct((d, AUX_COLS), BF16)],
        compiler_params=_cparams("parallel"),
        name="regroup_w_in",
    )(w)


def _proj_kernel(x_ref, mod_ref, nw_ref, w_ref, waux_ref, z_ref, g_ref, kv_ref, h_ref):
    j = pl.program_id(1)

    @pl.when(j == 0)
    def _():
        h = _norm_modulate(x_ref[...], nw_ref[...], mod_ref[0, 3:4, :], mod_ref[0, 4:5, :]).astype(BF16)
        h_ref[...] = h
        aux = jnp.dot(h, waux_ref[...], preferred_element_type=F32)
        g_ref[...] = aux[:, :GATE_LANES]
        kv_ref[...] = aux[:, GATE_LANES:].astype(BF16)

    half = z_ref.shape[1] // 2
    h = h_ref[...]
    z_ref[:, :half] = jnp.dot(h, w_ref[:, :half], preferred_element_type=F32).astype(BF16)
    z_ref[:, half:] = jnp.dot(h, w_ref[:, half:], preferred_element_type=F32).astype(BF16)


def _mixer_proj(x, mod3, row_of_tile, norm_w, w_main, w_aux, tm):
    n_tok, d = x.shape
    tn = PROJ_TN
    once = functools.partial(pl.BlockSpec, pipeline_mode=pl.Buffered(1))
    return pl.pallas_call(
        _proj_kernel,
        grid=(n_tok // tm, Z_COLS // tn),
        in_specs=[
            pl.BlockSpec((tm, d), lambda i, j: (i, 0)),
            pl.BlockSpec((1, N_MOD, d), lambda i, j: (row_of_tile(i, tm), 0, 0)),
            pl.BlockSpec((1, d), lambda i, j: (0, 0)),
            pl.BlockSpec((d, tn), lambda i, j: (0, j)),
            once((d, AUX_COLS), lambda i, j: (0, 0)),
        ],
        out_specs=[
            pl.BlockSpec((tm, tn), lambda i, j: (i, j)),
            pl.BlockSpec((tm, GATE_LANES), lambda i, j: (i, 0)),
            pl.BlockSpec((tm, 2 * A_KV_W), lambda i, j: (i, 0)),
        ],
        out_shape=[
            jax.ShapeDtypeStruct((n_tok, Z_COLS), BF16),
            jax.ShapeDtypeStruct((n_tok, GATE_LANES), F32),
            jax.ShapeDtypeStruct((n_tok, 2 * A_KV_W), BF16),
        ],
        scratch_shapes=[pltpu.VMEM((tm, d), BF16)],
        compiler_params=_cparams("parallel", "arbitrary"),
        name="mixer_proj",
    )(x, mod3, norm_w, w_main, w_aux)


def _scan_lanes(x, op, fill, reverse):
    n = x.shape[-1]
    lane = lax.broadcasted_iota(jnp.int32, x.shape, 1)
    s = 1
    while s < n:
        if reverse:
            x = op(x, jnp.where(lane < n - s, pltpu.roll(x, n - s, 1), fill))
        else:
            x = op(x, jnp.where(lane >= s, pltpu.roll(x, s, 1), fill))
        s *= 2
    return x


def _dir_scan(x, op, fill, fwd_rows):
    return jnp.where(fwd_rows, _scan_lanes(x, op, fill, False), _scan_lanes(x, op, fill, True))


def _log_sigmoid(x):
    return jnp.minimum(x, 0.0) - jnp.log1p(jnp.exp(-jnp.abs(x)))


def _mlstm_kernel(*refs, n_tok, has_state):
    if has_state:
        (q_ref, k_ref, v_ref, og_ref, gi_ref, gf_ref, bi_ref, bf_ref, nw_ref, c0_ref, n0_ref, m0_ref,
         o_ref) = refs
    else:
        (q_ref, k_ref, v_ref, og_ref, gi_ref, gf_ref, bi_ref, bf_ref, nw_ref,
         o_ref, c_out, n_out, m_out) = refs
    blk = min(MLSTM_BLOCK, n_tok)
    nq = n_tok // blk

    fwd_rows = lax.broadcasted_iota(jnp.int32, (GATE_ROWS, 1), 0) < M_HEADS
    ig = gi_ref[0] + bi_ref[...]
    cum_f = _dir_scan(_log_sigmoid(gf_ref[0] + bf_ref[...]), jnp.add, 0.0, fwd_rows)
    a = ig - cum_f
    m0 = m0_ref[0] if has_state else jnp.zeros((GATE_ROWS, 1), F32)
    mx = jnp.maximum(_dir_scan(a, jnp.maximum, -jnp.inf, fwd_rows), m0)
    stats = [-mx * LOG2E, -(cum_f + mx) * LOG2E]
    a2 = a * LOG2E
    m0_2 = m0 * LOG2E
    if not has_state:
        mx_end = jnp.where(fwd_rows, mx[:, n_tok - 1:n_tok], mx[:, 0:1])
        cum_end = jnp.where(fwd_rows, cum_f[:, n_tok - 1:n_tok], cum_f[:, 0:1])
        stats.append(jnp.exp(a - mx_end))
        m_out[0] = cum_end + mx_end
    pad = jnp.zeros((XPOSE_ROWS - GATE_ROWS * len(stats), n_tok), F32)
    cols = jnp.concatenate(stats + [pad], axis=0).T

    row_i = lax.broadcasted_iota(jnp.int32, (blk, blk), 0)
    col_i = lax.broadcasted_iota(jnp.int32, (blk, blk), 1)
    visible = (col_i <= row_i, col_i >= row_i)

    for h in range(M_HEADS):
        qk_cols = slice(h * M_QK, (h + 1) * M_QK)
        v_cols = slice(h * M_V, (h + 1) * M_V)
        qf = q_ref[0, :, qk_cols].astype(F32) * (M_QK ** -0.5)
        q = qf.astype(BF16)
        k = k_ref[0, :, qk_cols]
        v = v_ref[0, :, v_cols]
        for qi in range(nq):
            rows = slice(qi * blk, (qi + 1) * blk)
            h_sum = None
            for d in range(2):
                r = M_HEADS * d + h
                lo, hi = (0, (qi + 1) * blk) if d == 0 else (qi * blk, n_tok)
                dg = hi - lo - blk if d == 0 else 0
                u_col = cols[rows, r:r + 1]
                neg_mt_col = cols[rows, GATE_ROWS + r:GATE_ROWS + r + 1]
                dm = u_col + a2[r:r + 1, lo:hi]
                parts = [dm[:, :dg]] if dg > 0 else []
                parts.append(jnp.where(visible[d], dm[:, dg:dg + blk], -jnp.inf))
                if dg + blk < hi - lo:
                    parts.append(dm[:, dg + blk:])
                dm = jnp.concatenate(parts, axis=1) if len(parts) > 1 else parts[0]
                s = lax.dot_general(q[rows], k[lo:hi], NT_DIMS, preferred_element_type=F32) * jnp.exp2(dm)
                num = jnp.dot(s.astype(BF16), v[lo:hi], preferred_element_type=F32)
                den = jnp.sum(s, axis=1, keepdims=True)
                if has_state:
                    decay = jnp.exp2(u_col + m0_2[r:r + 1, :])
                    num = num + decay * lax.dot_general(q[rows], c0_ref[0, r].astype(BF16), NT_DIMS,
                                                        preferred_element_type=F32)
                    den = den + decay * jnp.sum(qf[rows] * n0_ref[0, r:r + 1, :], axis=1, keepdims=True)
                h_dir = num / jnp.maximum(jnp.abs(den), jnp.exp2(neg_mt_col))
                h_sum = h_dir if h_sum is None else h_sum + h_dir
            hn = h_sum * lax.rsqrt(jnp.mean(h_sum * h_sum, axis=-1, keepdims=True) + NORM_EPS) * nw_ref[:, v_cols]
            o_ref[0, rows, v_cols] = (hn * _sigmoid(og_ref[0, rows, v_cols].astype(F32))).astype(BF16)
        if not has_state:
            kf = k.astype(F32)
            vf = v.astype(F32)
            for d in range(2):
                r = M_HEADS * d + h
                wk_col = cols[:, 2 * GATE_ROWS + r:2 * GATE_ROWS + r + 1]
                c_out[0, r] = lax.dot_general((vf * wk_col).astype(BF16), k, TN_DIMS, preferred_element_type=F32)
                n_out[0, r:r + 1, :] = jnp.sum(kf * wk_col, axis=0, keepdims=True)


def _mlstm(z3, gate_i, gate_f, bias_i, bias_f, norm_w, states):
    bsz, n_tok, _ = z3.shape
    has_state = states is not None
    qk_w, v_w = M_HEADS * M_QK, M_HEADS * M_V
    in_specs = [
        pl.BlockSpec((1, n_tok, qk_w), lambda b: (b, 0, Z_MQ // qk_w)),
        pl.BlockSpec((1, n_tok, qk_w), lambda b: (b, 0, Z_MK // qk_w)),
        pl.BlockSpec((1, n_tok, v_w), lambda b: (b, 0, Z_MV // v_w)),
        pl.BlockSpec((1, n_tok, v_w), lambda b: (b, 0, Z_MO // v_w)),
        pl.BlockSpec((1, GATE_ROWS, n_tok), lambda b: (b, 0, 0)),
        pl.BlockSpec((1, GATE_ROWS, n_tok), lambda b: (b, 0, 0)),
        pl.BlockSpec((GATE_ROWS, 1), lambda b: (0, 0)),
        pl.BlockSpec((GATE_ROWS, 1), lambda b: (0, 0)),
        pl.BlockSpec((1, v_w), lambda b: (0, 0)),
    ]
    state_specs = [
        pl.BlockSpec((1, GATE_ROWS, M_V, M_QK), lambda b: (b, 0, 0, 0)),
        pl.BlockSpec((1, GATE_ROWS, M_QK), lambda b: (b, 0, 0)),
        pl.BlockSpec((1, GATE_ROWS, 1), lambda b: (b, 0, 0)),
    ]
    out_specs = [pl.BlockSpec((1, n_tok, v_w), lambda b: (b, 0, 0))]
    out_shape = [jax.ShapeDtypeStruct((bsz, n_tok, v_w), BF16)]
    args = [z3, z3, z3, z3, gate_i, gate_f, bias_i, bias_f, norm_w]
    if has_state:
        in_specs += state_specs
        args += list(states)
    else:
        out_specs += state_specs
        out_shape += [
            jax.ShapeDtypeStruct((bsz, GATE_ROWS, M_V, M_QK), F32),
            jax.ShapeDtypeStruct((bsz, GATE_ROWS, M_QK), F32),
            jax.ShapeDtypeStruct((bsz, GATE_ROWS, 1), F32),
        ]
    return pl.pallas_call(
        functools.partial(_mlstm_kernel, n_tok=n_tok, has_state=has_state),
        grid=(bsz,),
        in_specs=in_specs,
        out_specs=out_specs,
        out_shape=out_shape,
        compiler_params=_cparams("parallel"),
        name="mlstm",
    )(*args)


def _group_rms(x, w_row, ones_bd):
    x2 = x * x
    hi = x2.astype(BF16)
    lo = (x2 - hi.astype(F32)).astype(BF16)
    ss = jnp.dot(hi, ones_bd, preferred_element_type=F32) + jnp.dot(lo, ones_bd, preferred_element_type=F32)
    return x * lax.rsqrt(ss * (1.0 / A_HD) + NORM_EPS) * w_row


def _rope(x, cos, sin_signed, swap):
    hi = x.astype(BF16)
    lo = (x - hi.astype(F32)).astype(BF16)
    partner = jnp.dot(hi, swap, preferred_element_type=F32) + jnp.dot(lo, swap, preferred_element_type=F32)
    return x * cos + partner * sin_signed


def _attend_block(q, sink_col, k_loc, v_loc, kinds, masks, k_ctx=None, v_ctx=None):
    s_loc = lax.dot_general(q, k_loc, NT_DIMS, preferred_element_type=F32)
    tiles = []
    for j, kind in enumerate(kinds):
        t = s_loc[:, j * WINDOW:(j + 1) * WINDOW]
        tiles.append(t if kind == "cur" else jnp.where(masks[kind], t, -jnp.inf))
    n_loc = len(tiles)
    if k_ctx is not None:
        s_ctx = lax.dot_general(q, k_ctx, NT_DIMS, preferred_element_type=F32)
        tiles += [s_ctx[:, j * WINDOW:(j + 1) * WINDOW] for j in range(k_ctx.shape[0] // WINDOW)]
    tile_max = tiles[0]
    for t in tiles[1:]:
        tile_max = jnp.maximum(tile_max, t)
    m = jnp.maximum(jnp.max(tile_max, axis=1, keepdims=True), sink_col)
    p = [jnp.exp2(t - m).astype(BF16) for t in tiles]
    acc = jnp.dot(jnp.concatenate(p[:n_loc], axis=1), v_loc, preferred_element_type=F32)
    if k_ctx is not None:
        acc = acc + jnp.dot(jnp.concatenate(p[n_loc:], axis=1), v_ctx, preferred_element_type=F32)
    den = acc[:, A_HD:A_HD + 1] + jnp.exp2(sink_col - m)
    return acc[:, :A_HD] / den


def _attn_kernel(*refs, n_tok, latent):
    if latent:
        (sink_ref, q_ref, k_ref, v_ref, qw_ref, kw_ref, cos_ref, sin_ref, ck_ref, cv_ref,
         o_ref) = refs
    else:
        (sink_ref, q_ref, k_ref, v_ref, qw_ref, kw_ref,
         o_ref, ko_ref, vo_ref) = refs

    sq_r = lax.broadcasted_iota(jnp.int32, (A_KV_W, A_KV_W), 0)
    sq_c = lax.broadcasted_iota(jnp.int32, (A_KV_W, A_KV_W), 1)
    ones_bd = jnp.where(sq_r // A_HD == sq_c // A_HD, 1.0, 0.0).astype(BF16)
    partner_of = jnp.where((sq_c % 32) < 16, sq_c + 16, sq_c - 16)
    swap = jnp.where(sq_r == partner_of, 1.0, 0.0).astype(BF16)

    k_all = _group_rms(k_ref[0].astype(F32), kw_ref[...], ones_bd)
    v_all = v_ref[0]
    if latent:
        k_all = _rope(k_all, cos_ref[...], sin_ref[...], swap)
    else:
        ko_ref[0] = k_all
        vo_ref[0] = v_all.astype(F32)

    q_blk = WINDOW if latent else n_tok
    m_rows = A_GROUP * q_blk
    head_of_row = lax.broadcasted_iota(jnp.int32, (m_rows, 1), 0) // q_blk
    r_in = lax.broadcasted_iota(jnp.int32, (m_rows, WINDOW), 0) % WINDOW
    c_in = lax.broadcasted_iota(jnp.int32, (m_rows, WINDOW), 1)
    masks = {"prev": c_in >= r_in, "next": c_in <= r_in}
    ones_v = jnp.ones((n_tok, A_HD), BF16)

    def sink_column(g):
        col = jnp.zeros((m_rows, 1), F32)
        for a in range(A_GROUP):
            col = jnp.where(head_of_row == a, sink_ref[A_GROUP * g + a] * LOG2E, col)
        return col

    def unstack(o):
        return jnp.concatenate([o[a * q_blk:(a + 1) * q_blk, :] for a in range(A_GROUP)], axis=1).astype(BF16)

    for g in range(A_KV):
        g_cols = slice(A_KV_W * g, A_KV_W * (g + 1))
        h_cols = slice(A_HD * g, A_HD * (g + 1))
        qg = _group_rms(q_ref[0, :, g_cols].astype(F32), qw_ref[...], ones_bd)
        if latent:
            qg = _rope(qg, cos_ref[...], sin_ref[...], swap)
        qg = qg * (A_HD ** -0.5 * LOG2E)
        kg = k_all[:, h_cols].astype(BF16)
        va = jnp.concatenate([v_all[:, h_cols], ones_v], axis=1)
        sink_col = sink_column(g)
        if not latent:
            q_stack = jnp.concatenate([qg[:, A_HD * a:A_HD * (a + 1)] for a in range(A_GROUP)], axis=0)
            o = _attend_block(q_stack.astype(BF16), sink_col, kg, va, ["cur"] * (n_tok // WINDOW), masks)
            o_ref[0, :, g_cols] = unstack(o)
            continue

        n_ctx = ck_ref.shape[1]
        ckg = ck_ref[0][:, h_cols].astype(BF16)
        cva = jnp.concatenate([cv_ref[0][:, h_cols].astype(BF16), jnp.ones((n_ctx, A_HD), BF16)], axis=1)
        nb = n_tok // WINDOW
        for i in range(nb):
            kinds = (["prev"] if i > 0 else []) + ["cur"] + (["next"] if i < nb - 1 else [])
            k0 = max(i - 1, 0) * WINDOW
            rows = slice(i * WINDOW, (i + 1) * WINDOW)
            q_stack = jnp.concatenate([qg[rows, A_HD * a:A_HD * (a + 1)] for a in range(A_GROUP)],
                                      axis=0).astype(BF16)
            n_keys = len(kinds) * WINDOW
            o = _attend_block(q_stack, sink_col, kg[k0:k0 + n_keys], va[k0:k0 + n_keys], kinds, masks, ckg, cva)
            o_ref[0, rows, g_cols] = unstack(o)


def _attention(z3, kv3, sink, q_norm_w, k_norm_w, rope_tabs, cache):
    bsz, n_tok, _ = z3.shape
    latent = cache is not None
    in_specs = [
        pl.BlockSpec(memory_space=pltpu.SMEM),
        pl.BlockSpec((1, n_tok, A_Q_W), lambda b: (b, 0, Z_AQ // A_Q_W)),
        pl.BlockSpec((1, n_tok, A_KV_W), lambda b: (b, 0, 0)),
        pl.BlockSpec((1, n_tok, A_KV_W), lambda b: (b, 0, 1)),
        pl.BlockSpec((1, A_KV_W), lambda b: (0, 0)),
        pl.BlockSpec((1, A_KV_W), lambda b: (0, 0)),
    ]
    args = [sink, z3, kv3, kv3, q_norm_w, k_norm_w]
    out_specs = [pl.BlockSpec((1, n_tok, A_Q_W), lambda b: (b, 0, 0))]
    out_shape = [jax.ShapeDtypeStruct((bsz, n_tok, A_Q_W), BF16)]
    if latent:
        assert n_tok // WINDOW >= 2
        n_ctx = cache[0].shape[1]
        in_specs += [
            pl.BlockSpec((n_tok, A_KV_W), lambda b: (0, 0)),
            pl.BlockSpec((n_tok, A_KV_W), lambda b: (0, 0)),
            pl.BlockSpec((1, n_ctx, A_KV_W), lambda b: (b, 0, 0)),
            pl.BlockSpec((1, n_ctx, A_KV_W), lambda b: (b, 0, 0)),
        ]
        args += [rope_tabs[0], rope_tabs[1], cache[0], cache[1]]
    else:
        out_specs += [pl.BlockSpec((1, n_tok, A_KV_W), lambda b: (b, 0, 0))] * 2
        out_shape += [jax.ShapeDtypeStruct((bsz, n_tok, A_KV_W), F32)] * 2
    return pl.pallas_call(
        functools.partial(_attn_kernel, n_tok=n_tok, latent=latent),
        grid=(bsz,),
        in_specs=in_specs,
        out_specs=out_specs,
        out_shape=out_shape,
        compiler_params=_cparams("parallel"),
        name="attention",
    )(*args)


def _merge_kernel(x_ref, mod_ref, hm_ref, ha_ref, gm_ref, ga_ref, wpm_ref, wpa_ref, wo_ref, o_ref):
    pm = jnp.dot(hm_ref[...], wpm_ref[...], preferred_element_type=F32)
    pa = jnp.dot(ha_ref[...], wpa_ref[...], preferred_element_type=F32)
    u = _sigmoid(gm_ref[...].astype(F32)) * pm + _sigmoid(ga_ref[...].astype(F32)) * pa
    mix = jnp.dot(u.astype(BF16), wo_ref[...], preferred_element_type=F32)
    o_ref[...] = x_ref[...] + mod_ref[0, 5:6, :] * mix


def _merge(x, mod3, row_of_tile, hm, ha, z, w_proj_m, w_proj_a, w_out, tm):
    n_tok, d = x.shape
    resident = functools.partial(pl.BlockSpec, pipeline_mode=pl.Buffered(1))
    return pl.pallas_call(
        _merge_kernel,
        grid=(n_tok // tm,),
        in_specs=[
            pl.BlockSpec((tm, d), lambda i: (i, 0)),
            pl.BlockSpec((1, N_MOD, d), lambda i: (row_of_tile(i, tm), 0, 0)),
            pl.BlockSpec((tm, M_HEADS * M_V), lambda i: (i, 0)),
            pl.BlockSpec((tm, A_Q_W), lambda i: (i, 0)),
            pl.BlockSpec((tm, d), lambda i: (i, Z_GM // D_MODEL)),
            pl.BlockSpec((tm, d), lambda i: (i, Z_GA // D_MODEL)),
            resident((M_HEADS * M_V, d), lambda i: (0, 0)),
            resident((A_Q_W, d), lambda i: (0, 0)),
            resident((d, d), lambda i: (0, 0)),
        ],
        out_specs=pl.BlockSpec((tm, d), lambda i: (i, 0)),
        out_shape=jax.ShapeDtypeStruct((n_tok, d), F32),
        compiler_params=_cparams("parallel"),
        name="merge",
    )(x, mod3, hm, ha, z, z, w_proj_m, w_proj_a, w_out)


def _rope_tables(n_tok):
    nf = A_HD // 4
    inv = ROPE_THETA ** (-jnp.arange(nf, dtype=F32) / nf)
    tok = jnp.arange(n_tok)
    pos = jnp.stack([tok // GRID_W, tok % GRID_W], axis=1).astype(F32)
    ang = pos[:, :, None] * inv
    cos = jnp.cos(ang)
    sin = jnp.sin(ang)
    cos_h = jnp.concatenate([cos, cos], axis=-1).reshape(n_tok, A_HD)
    sin_h = jnp.concatenate([-sin, sin], axis=-1).reshape(n_tok, A_HD)
    return jnp.tile(cos_h, (1, A_KV)), jnp.tile(sin_h, (1, A_KV))


TM_FFN = 512
TM_PROJ = 1024
TM_MERGE = 256


def _trunk(x, mod3, row_of_tile, wts, states, rope_tabs, cache):
    bsz, n_tok, d = x.shape
    x2 = x.reshape(bsz * n_tok, d)
    z, gates, kv = _mixer_proj(x2, mod3, row_of_tile, wts["norm_mix"], wts["w_main"], wts["w_aux"], TM_PROJ)
    z3 = z.reshape(bsz, n_tok, Z_COLS)
    g = jnp.transpose(gates[:, :M_GATES].reshape(bsz, n_tok, 2, 2, M_HEADS), (0, 2, 3, 4, 1))
    gate_i = g[:, :, 0].reshape(bsz, GATE_ROWS, n_tok)
    gate_f = g[:, :, 1].reshape(bsz, GATE_ROWS, n_tok)
    m_out = _mlstm(z3, gate_i, gate_f, wts["bias_i"], wts["bias_f"], wts["mlstm_norm"], states)
    a_out = _attention(z3, kv.reshape(bsz, n_tok, 2 * A_KV_W), wts["sink"], wts["q_norm"], wts["k_norm"],
                       rope_tabs, cache)
    hm = m_out[0].reshape(bsz * n_tok, M_HEADS * M_V)
    ha = a_out[0].reshape(bsz * n_tok, A_Q_W)
    x2 = _merge(x2, mod3, row_of_tile, hm, ha, z, wts["w_proj_m"], wts["w_proj_a"], wts["w_out"], TM_MERGE)
    x2 = _ffn(x2, mod3, row_of_tile, wts["norm2"], wts["wi2"], wts["wo2"], 6, TM_FFN)
    return x2.reshape(bsz, n_tok, d), m_out[1:], a_out[1:]


def kernel(x_prompt, x_sample, cache_attn_k, cache_attn_v, state_mlstm_C, state_mlstm_n, state_mlstm_m, c, c_ctx, ada_w, ada_b, norm_ffn1_w, ffn1_wi, ffn1_wo, norm_mix_w, w_in, mlstm_gate_b, mlstm_norm_w, attn_q_norm_w, attn_k_norm_w, attn_sink, w_proj_m, w_proj_a, w_out, norm_ffn2_w, ffn2_wi, ffn2_wo):
    bp, tp, d = x_prompt.shape
    bs, ts, _ = x_sample.shape
    n_ctx = cache_attn_k.shape[2]
    l = 0

    n_rows = 16
    cond = jnp.concatenate([c_ctx[None, :], c, jnp.zeros((n_rows - 1 - bs, d), F32)], axis=0)
    mod3 = _modulation(cond, ada_w[l], ada_b[l][None, :]).reshape(n_rows, N_MOD, d)

    w_main, w_aux = _regroup_w_in(w_in, l)
    gate_b = mlstm_gate_b[l].reshape(2, 2, M_HEADS)
    norm1, wi1, wo1 = norm_ffn1_w[l][None, :], ffn1_wi[l].astype(BF16), ffn1_wo[l].astype(BF16)
    ctx_row = lambda i, tm: 0
    lat_row = lambda i, tm: 1 + (i * tm) // ts

    x1_prompt = _ffn(x_prompt.reshape(bp * tp, d), mod3, ctx_row, norm1, wi1, wo1, 0, TM_FFN).reshape(bp, tp, d)
    x1_sample, wi2, wo2, w_out_b, w_pm_b, w_pa_b = _ffn(
        x_sample.reshape(bs * ts, d), mod3, lat_row, norm1, wi1, wo1, 0, TM_FFN,
        side=(ffn2_wi[l], ffn2_wo[l], w_out[l], w_proj_m[l], w_proj_a[l]))
    x1_sample = x1_sample.reshape(bs, ts, d)
    wts = dict(
        norm_mix=norm_mix_w[l][None, :], w_main=w_main, w_aux=w_aux,
        bias_i=gate_b[:, 0].reshape(GATE_ROWS, 1), bias_f=gate_b[:, 1].reshape(GATE_ROWS, 1),
        mlstm_norm=mlstm_norm_w[l][None, :],
        sink=attn_sink[l], q_norm=jnp.tile(attn_q_norm_w[l], A_KV)[None, :],
        k_norm=jnp.tile(attn_k_norm_w[l], A_KV)[None, :],
        w_proj_m=w_pm_b, w_proj_a=w_pa_b, w_out=w_out_b,
        norm2=norm_ffn2_w[l][None, :], wi2=wi2, wo2=wo2,
    )

    y_prompt, (c_new, n_new, m_new), (k_new, v_new) = _trunk(x1_prompt, mod3, ctx_row, wts, None, None, None)
    new_attn_k = k_new.reshape(bp, 1, tp, A_KV, A_HD)
    new_attn_v = v_new.reshape(bp, 1, tp, A_KV, A_HD)
    new_c = c_new.reshape(bp, 1, 2, M_HEADS, M_V, M_QK)
    new_n = n_new.reshape(bp, 1, 2, M_HEADS, M_QK)
    new_m = m_new.reshape(bp, 1, 2, M_HEADS)

    states = (state_mlstm_C[:, l].reshape(bs, GATE_ROWS, M_V, M_QK),
              state_mlstm_n[:, l].reshape(bs, GATE_ROWS, M_QK),
              state_mlstm_m[:, l].reshape(bs, GATE_ROWS, 1))
    cache = (cache_attn_k[:, l].reshape(bs, n_ctx, A_KV_W), cache_attn_v[:, l].reshape(bs, n_ctx, A_KV_W))
    y_sample, _, _ = _trunk(x1_sample, mod3, lat_row, wts, states, _rope_tables(ts), cache)

    return (y_prompt, y_sample, new_attn_k, new_attn_v, new_c, new_n, new_m)
```
